```python
import jax, jax.numpy as jnp
from jax import lax
import numpy as np

D_MODEL = 2048
BATCH = 16
SEQ = 2048
DEPTH = 1
DEC_BATCH = 128
DEC_SEQ = 8
PAST_LEN = 16384
PAGE_SIZE = 128

HEAD_DIM = 64
ROT_DIM = HEAD_DIM // 4
ROPE_THETA = 500000.0
NORM_EPS = 1e-6
BLOCK = 128

A_HEADS = 16
A_KV_HEADS = 4
A_WINDOW = 128
B_PATTERNS = ((128, 1), (512, 4), (2048, 16))
B_GROUPS = 3
B_HEADS_PER_GROUP = 8
B_HEADS = B_GROUPS * B_HEADS_PER_GROUP

A_Q_W = A_HEADS * HEAD_DIM
A_KV_W = A_KV_HEADS * HEAD_DIM
B_W = B_HEADS * HEAD_DIM
B_OUT_W = B_HEADS_PER_GROUP * HEAD_DIM
IN_COLS = A_Q_W + 2 * A_KV_W + 3 * B_W + 2 * D_MODEL

N_EXPERT_GROUPS = 4
EXPERTS_PER_GROUP = 8
N_EXPERTS = N_EXPERT_GROUPS * EXPERTS_PER_GROUP
TOP_K_IN_GROUP = 2
D_EXPERT = 512

kernel_name = 'hybrid_swa_dilated_hmoe_step'


def _rmsnorm(x, g):
    xf = x.astype(jnp.float32)
    y = xf * lax.rsqrt(jnp.mean(xf * xf, axis=-1, keepdims=True) + NORM_EPS)
    return (y * g.astype(jnp.float32)).astype(x.dtype)


def _rope(x, pos):
    half = ROT_DIM // 2
    inv_freq = ROPE_THETA ** (-jnp.arange(half, dtype=jnp.float32) / half)
    ang = pos.astype(jnp.float32)[:, None] * inv_freq[None, :]
    ang = ang.reshape((ang.shape[0],) + (1,) * (x.ndim - 3) + (half,))
    cos, sin = jnp.cos(ang), jnp.sin(ang)
    xr = x[..., :ROT_DIM].astype(jnp.float32)
    x1, x2 = xr[..., :half], xr[..., half:]
    rot = jnp.concatenate([x1 * cos - x2 * sin, x2 * cos + x1 * sin], axis=-1)
    return jnp.concatenate([rot.astype(x.dtype), x[..., ROT_DIM:]], axis=-1)


def _project(h, pos, w_in, q_norm_a, k_norm_a, q_norm_b, k_norm_b):
    N, L, _ = h.shape
    z = jnp.einsum('nld,dc->nlc', h, w_in)
    widths = (A_Q_W, A_KV_W, A_KV_W, B_W, B_W, B_W, D_MODEL, D_MODEL)
    offs = np.cumsum(widths)[:-1].tolist()
    qa, ka, va, qb, kb, vb, ga, gb = jnp.split(z, offs, axis=-1)
    qa = _rope(_rmsnorm(qa.reshape(N, L, A_HEADS, HEAD_DIM), q_norm_a), pos)
    ka = _rope(_rmsnorm(ka.reshape(N, L, A_KV_HEADS, HEAD_DIM), k_norm_a), pos)
    va = va.reshape(N, L, A_KV_HEADS, HEAD_DIM)
    shp = (N, L, B_GROUPS, B_HEADS_PER_GROUP, HEAD_DIM)
    qb = _rope(_rmsnorm(qb.reshape(shp), q_norm_b), pos)
    kb = _rope(_rmsnorm(kb.reshape(shp), k_norm_b), pos)
    vb = vb.reshape(shp)
    return qa, ka, va, qb, kb, vb, ga, gb


def _sink_softmax(s, mask, sink):
    s = jnp.where(mask, s, -jnp.inf)
    m = jnp.max(s, axis=-1)
    if sink is not None:
        m = jnp.maximum(m, sink)
    p = jnp.exp(s - m[..., None])
    denom = jnp.sum(p, axis=-1)
    if sink is not None:
        denom = denom + jnp.exp(sink - m)
    return p / denom[..., None], m + jnp.log(denom)


def _banded_attention(q, k, v, max_dist, sinks):
    N, L, Hq, D = q.shape
    Hkv = k.shape[2]
    G = Hq // Hkv
    nb = L // BLOCK
    qb = q.reshape(N, nb, BLOCK, Hkv, G, D)

    def band(x):
        xp = jnp.pad(x, ((0, 0), (BLOCK, 0), (0, 0), (0, 0))).reshape(N, nb + 1, BLOCK, Hkv, D)
        return jnp.concatenate([xp[:, :-1], xp[:, 1:]], axis=2)

    kb, vb = band(k), band(v)
    s = jnp.einsum('nbqhgd,nbkhd->nbhgqk', qb, kb, preferred_element_type=jnp.float32) * (D ** -0.5)
    qi = jnp.arange(BLOCK)[:, None] + BLOCK
    kj = jnp.arange(2 * BLOCK)[None, :]
    dist = qi - kj
    kpos = jnp.arange(nb)[:, None, None] * BLOCK + kj[None] - BLOCK
    mask = ((dist >= 0) & (dist <= max_dist))[None] & (kpos >= 0)
    mask = mask[None, :, None, None]
    sink = None if sinks is None else sinks.astype(jnp.float32).reshape(Hkv, G)[None, None, :, :, None]
    p, lse = _sink_softmax(s, mask, sink)
    out = jnp.einsum('nbhgqk,nbkhd->nbqhgd', p.astype(v.dtype), vb, preferred_element_type=jnp.float32)
    out = out.reshape(N, L, Hq, D).astype(q.dtype)
    lse = lse.transpose(0, 1, 4, 2, 3).reshape(N, L, Hq)
    return out, lse


def _dilated_prompt(q, k, v, dil, n_off):
    N, S, H, D = q.shape
    L = S // dil
    Lp = -(-L // BLOCK) * BLOCK

    def to_res(x):
        x = x.reshape(N, L, dil, H, D).transpose(0, 2, 1, 3, 4).reshape(N * dil, L, H, D)
        return jnp.pad(x, ((0, 0), (0, Lp - L), (0, 0), (0, 0)))

    out, lse = _banded_attention(to_res(q), to_res(k), to_res(v), n_off - 1, None)
    out = out[:, :L].reshape(N, dil, L, H, D).transpose(0, 2, 1, 3, 4).reshape(N, S, H, D)
    lse = lse[:, :L].reshape(N, dil, L, H).transpose(0, 2, 1, 3).reshape(N, S, H)
    return out, lse


def _window_sample(q, k_new, v_new, buf, max_dist, sinks):
    N, T, Hq, D = q.shape
    Hkv = k_new.shape[2]
    G = Hq // Hkv
    W = buf.shape[1]
    k = jnp.concatenate([buf[:, :, 0], k_new], axis=1)
    v = jnp.concatenate([buf[:, :, 1], v_new], axis=1)
    s = jnp.einsum('nthgd,nkhd->nhgtk', q.reshape(N, T, Hkv, G, D), k,
                   preferred_element_type=jnp.float32) * (D ** -0.5)
    dist = (W + jnp.arange(T))[:, None] - jnp.arange(W + T)[None, :]
    mask = ((dist >= 0) & (dist <= max_dist))[None, None, None]
    sink = sinks.astype(jnp.float32).reshape(Hkv, G)[None, :, :, None]
    p, _ = _sink_softmax(s, mask, sink)
    out = jnp.einsum('nhgtk,nkhd->nthgd', p.astype(v.dtype), v, preferred_element_type=jnp.float32)
    new_buf = jnp.stack([k, v], axis=2)[:, T:]
    return out.reshape(N, T, Hq, D).astype(q.dtype), new_buf


def _dilated_sample(q, k_new, v_new, buf, dil, n_off):
    N, T, H, D = q.shape
    W = buf.shape[1]
    k = jnp.concatenate([buf[:, :, 0], k_new], axis=1)
    v = jnp.concatenate([buf[:, :, 1], v_new], axis=1)
    idx = (W + jnp.arange(T))[:, None] - dil * jnp.arange(n_off)[None, :]
    valid = idx >= 0
    idx = jnp.maximum(idx, 0)
    kg = k[:, idx]
    vg = v[:, idx]
    s = jnp.einsum('nthd,ntjhd->nhtj', q, kg, preferred_element_type=jnp.float32) * (D ** -0.5)
    p, lse = _sink_softmax(s, valid[None, None], None)
    out = jnp.einsum('nhtj,ntjhd->nthd', p.astype(v.dtype), vg, preferred_element_type=jnp.float32)
    new_buf = jnp.stack([k, v], axis=2)[:, T:]
    return out.astype(q.dtype), lse.transpose(0, 2, 1), new_buf


def _combine_dilations(outs, lses):
    o = jnp.stack(outs, axis=0).astype(jnp.float32)
    w = jax.nn.softmax(jnp.stack(lses, axis=0), axis=0)
    return jnp.sum(o * w[..., None], axis=0).astype(outs[0].dtype)


def _hier_moe(h, w_group_router, b_group_router, w_expert_router, b_expert_router,
              w_gate, w_up, w_down):
    N, L, D = h.shape
    t = h.reshape(N * L, D)
    T = t.shape[0]
    g_logits = jnp.dot(t, w_group_router, preferred_element_type=jnp.float32) + b_group_router.astype(jnp.float32)
    g_prob = jax.nn.softmax(g_logits, axis=-1)
    g_idx = jnp.argmax(g_logits, axis=-1)
    g_w = jnp.take_along_axis(g_prob, g_idx[:, None], axis=1)
    e_logits = jnp.dot(t, w_expert_router, preferred_element_type=jnp.float32) + b_expert_router.astype(jnp.float32)
    e_logits = e_logits.reshape(T, N_EXPERT_GROUPS, EXPERTS_PER_GROUP)
    e_logits = jnp.take_along_axis(e_logits, g_idx[:, None, None], axis=1)[:, 0]
    e_prob = jax.nn.softmax(e_logits, axis=-1)
    top_w, top_i = lax.top_k(e_prob, TOP_K_IN_GROUP)
    top_w = top_w / jnp.sum(top_w, axis=-1, keepdims=True)
    expert_id = g_idx[:, None] * EXPERTS_PER_GROUP + top_i
    combine = jnp.sum(jax.nn.one_hot(expert_id, N_EXPERTS, dtype=jnp.float32)
                      * (g_w * top_w)[..., None], axis=1)
    out = jnp.zeros((T, D), jnp.float32)
    for e in range(N_EXPERTS):
        a = jax.nn.silu(t @ w_gate[e]) * (t @ w_up[e])
        out = out + combine[:, e:e + 1] * jnp.dot(a, w_down[e], preferred_element_type=jnp.float32)
    return out.astype(h.dtype).reshape(N, L, D)


def _merge_and_channel_mix(x, oa, ob, ga, gb, w_branch_a, w_branch_b, w_out, g_ffn_norm,
                           w_group_router, b_group_router, w_expert_router, b_expert_router,
                           w_expert_gate, w_expert_up, w_expert_down):
    N, L, _ = x.shape
    ya = oa.reshape(N, L, A_Q_W) @ w_branch_a
    yb = ob.reshape(N, L, B_OUT_W) @ w_branch_b
    merged = jax.nn.sigmoid(ga) * ya + jax.nn.sigmoid(gb) * yb
    x = x + merged @ w_out
    h = _rmsnorm(x, g_ffn_norm)
    return x + _hier_moe(h, w_group_router, b_group_router, w_expert_router, b_expert_router,
                         w_expert_gate, w_expert_up, w_expert_down)


def setup_inputs(seed: int = 0) -> dict:
    key = jax.random.key(seed)
    ks = jax.random.split(key, 24)
    f32 = jnp.float32

    def normal(k, shape, scale=1.0):
        return scale * jax.random.normal(k, shape, f32)

    def gain(k, shape):
        return 1.0 + 0.05 * jax.random.normal(k, shape, f32)

    wa = min(A_WINDOW, PAST_LEN)
    wb = [min(w, PAST_LEN) for w, _ in B_PATTERNS]
    return {
        'x_prompt': normal(ks[0], (BATCH, SEQ, D_MODEL)),
        'x_sample': normal(ks[1], (DEC_BATCH, DEC_SEQ, D_MODEL)),
        'cache_a_kv': normal(ks[2], (DEPTH, DEC_BATCH, wa, 2, A_KV_HEADS, HEAD_DIM)),
        'cache_b1_kv': normal(ks[3], (DEPTH, DEC_BATCH, wb[0], 2, B_HEADS_PER_GROUP, HEAD_DIM)),
        'cache_b2_kv': normal(ks[4], (DEPTH, DEC_BATCH, wb[1], 2, B_HEADS_PER_GROUP, HEAD_DIM)),
        'cache_b3_kv': normal(ks[5], (DEPTH, DEC_BATCH, wb[2], 2, B_HEADS_PER_GROUP, HEAD_DIM)),
        'g_attn_norm': gain(ks[6], (DEPTH, D_MODEL)),
        'w_in': normal(ks[7], (DEPTH, D_MODEL, IN_COLS), D_MODEL ** -0.5),
        'q_norm_a': gain(ks[8], (DEPTH, HEAD_DIM)),
        'k_norm_a': gain(ks[9], (DEPTH, HEAD_DIM)),
        'q_norm_b': gain(ks[10], (DEPTH, HEAD_DIM)),
        'k_norm_b': gain(ks[11], (DEPTH, HEAD_DIM)),
        'sinks_a': normal(ks[12], (DEPTH, A_HEADS), 0.5),
        'w_branch_a': normal(ks[13], (DEPTH, A_Q_W, D_MODEL), A_Q_W ** -0.5),
        'w_branch_b': normal(ks[14], (DEPTH, B_OUT_W, D_MODEL), B_OUT_W ** -0.5),
        'w_out': normal(ks[15], (DEPTH, D_MODEL, D_MODEL), D_MODEL ** -0.5),
        'g_ffn_norm': gain(ks[16], (DEPTH, D_MODEL)),
        'w_group_router': normal(ks[17], (DEPTH, D_MODEL, N_EXPERT_GROUPS), D_MODEL ** -0.5),
        'b_group_router': normal(ks[18], (DEPTH, N_EXPERT_GROUPS), 0.01),
        'w_expert_router': normal(ks[19], (DEPTH, D_MODEL, N_EXPERTS), D_MODEL ** -0.5),
        'b_expert_router': normal(ks[20], (DEPTH, N_EXPERTS), 0.01),
        'w_expert_gate': normal(ks[21], (DEPTH, N_EXPERTS, D_MODEL, D_EXPERT), D_MODEL ** -0.5),
        'w_expert_up': normal(ks[22], (DEPTH, N_EXPERTS, D_MODEL, D_EXPERT), D_MODEL ** -0.5),
        'w_expert_down': normal(ks[23], (DEPTH, N_EXPERTS, D_EXPERT, D_MODEL), D_EXPERT ** -0.5),
    }


def reference(x_prompt, x_sample, cache_a_kv, cache_b1_kv, cache_b2_kv, cache_b3_kv,
              g_attn_norm, w_in, q_norm_a, k_norm_a, q_norm_b, k_norm_b, sinks_a,
              w_branch_a, w_branch_b, w_out, g_ffn_norm, w_group_router, b_group_router,
              w_expert_router, b_expert_router, w_expert_gate, w_expert_up, w_expert_down):
    S = x_prompt.shape[1]
    T = x_sample.shape[1]
    pos_p = jnp.arange(S, dtype=jnp.int32)
    pos_s = PAST_LEN + jnp.arange(T, dtype=jnp.int32)
    b_caches = (cache_b1_kv, cache_b2_kv, cache_b3_kv)
    new_a_p, new_a_s = [], []
    new_b_p = ([], [], [])
    new_b_s = ([], [], [])
    xp, xs = x_prompt, x_sample
    for l in range(DEPTH):
        norms = (q_norm_a[l], k_norm_a[l], q_norm_b[l], k_norm_b[l])
        merge_params = (w_branch_a[l], w_branch_b[l], w_out[l], g_ffn_norm[l],
                        w_group_router[l], b_group_router[l], w_expert_router[l], b_expert_router[l],
                        w_expert_gate[l], w_expert_up[l], w_expert_down[l])

        qa, ka, va, qb, kb, vb, ga, gb = _project(_rmsnorm(xp, g_attn_norm[l]), pos_p, w_in[l], *norms)
        oa, _ = _banded_attention(qa, ka, va, A_WINDOW - 1, sinks_a[l])
        new_a_p.append(jnp.stack([ka, va], axis=2)[:, S - min(A_WINDOW, S):])
        outs, lses = [], []
        for g, (win, dil) in enumerate(B_PATTERNS):
            o, lse = _dilated_prompt(qb[:, :, g], kb[:, :, g], vb[:, :, g], dil, win // dil + 1)
            outs.append(o)
            lses.append(lse)
            new_b_p[g].append(jnp.stack([kb[:, :, g], vb[:, :, g]], axis=2)[:, S - min(win, S):])
        ob = _combine_dilations(outs, lses)
        xp = _merge_and_channel_mix(xp, oa, ob, ga, gb, *merge_params)

        qa, ka, va, qb, kb, vb, ga, gb = _project(_rmsnorm(xs, g_attn_norm[l]), pos_s, w_in[l], *norms)
        oa, buf_a = _window_sample(qa, ka, va, cache_a_kv[l], A_WINDOW - 1, sinks_a[l])
        new_a_s.append(buf_a)
        outs, lses = [], []
        for g, (win, dil) in enumerate(B_PATTERNS):
            o, lse, buf_b = _dilated_sample(qb[:, :, g], kb[:, :, g], vb[:, :, g], b_caches[g][l], dil, win // dil + 1)
            outs.append(o)
            lses.append(lse)
            new_b_s[g].append(buf_b)
        ob = _combine_dilations(outs, lses)
        xs = _merge_and_channel_mix(xs, oa, ob, ga, gb, *merge_params)

    return (xp, xs,
            jnp.stack(new_a_p), jnp.stack(new_a_s),
            jnp.stack(new_b_p[0]), jnp.stack(new_b_s[0]),
            jnp.stack(new_b_p[1]), jnp.stack(new_b_s[1]),
            jnp.stack(new_b_p[2]), jnp.stack(new_b_s[2]))
```

```python
import functools

import jax
import jax.numpy as jnp
import numpy as np
from jax import lax
from jax.experimental import pallas as pl
from jax.experimental.pallas import tpu as pltpu

F32 = jnp.float32
BF16 = jnp.bfloat16

D_MODEL = 2048
HEAD_DIM = 64
ROT_DIM = HEAD_DIM // 4
ROPE_THETA = 500000.0
NORM_EPS = 1e-6
BLOCK = 128
PAST_LEN = 16384

A_HEADS = 16
A_KV_HEADS = 4
A_WINDOW = 128
B_PATTERNS = ((128, 1), (512, 4), (2048, 16))
B_GROUPS = 3
B_HPG = 8
A_Q_W = A_HEADS * HEAD_DIM
A_KV_W = A_KV_HEADS * HEAD_DIM
B_GW = B_HPG * HEAD_DIM
B_W = B_GROUPS * B_GW

N_EXPERT_GROUPS = 4
EXPERTS_PER_GROUP = 8
N_EXPERTS = N_EXPERT_GROUPS * EXPERTS_PER_GROUP
D_EXPERT = 512

LANES = 128
SUBLANES = 8
VMEM_LIMIT = 56 * 1024 * 1024
SAMPLE_T = 8

MOE_TILE = 256
ROW_TILE = 256


def _cparams(sem, vmem=None):
    return pltpu.CompilerParams(dimension_semantics=sem, vmem_limit_bytes=vmem)


def _rmsnorm(x, g):
    return x * lax.rsqrt(jnp.mean(x * x, axis=-1, keepdims=True) + NORM_EPS) * g


def _split2(v):
    hi = v.astype(BF16)
    lo = (v - hi.astype(F32)).astype(BF16)
    return hi, lo


def _split3(v):
    hi = v.astype(BF16)
    r = v - hi.astype(F32)
    mid = r.astype(BF16)
    lo = (r - mid.astype(F32)).astype(BF16)
    return hi, mid, lo


def _dot(a, b):
    return jnp.dot(a, b, preferred_element_type=F32)


def _dot_nt(a, b):
    return lax.dot_general(a, b, (((1,), (1,)), ((), ())), preferred_element_type=F32)


def _rope_tables(pos):
    half = ROT_DIM // 2
    inv_freq = ROPE_THETA ** (-jnp.arange(half, dtype=F32) / half)
    ang = pos.astype(F32)[:, None] * inv_freq[None, :]
    cos, sin = jnp.cos(ang), jnp.sin(ang)
    p = pos.shape[0]
    ones = jnp.ones((p, HEAD_DIM - ROT_DIM), F32)
    z8 = jnp.zeros((p, half), F32)
    z48 = jnp.zeros((p, HEAD_DIM - ROT_DIM), F32)
    c = jnp.concatenate([cos, cos, ones], axis=1)
    s_up = jnp.concatenate([-sin, z8, z48], axis=1)
    s_dn = jnp.concatenate([z8, sin, z48], axis=1)
    rep = LANES // HEAD_DIM
    return jnp.tile(c, (1, rep)), jnp.tile(s_up, (1, rep)), jnp.tile(s_dn, (1, rep))


def _proj_body(x_ref, g_ref, w_ref, gain_ref, c_ref, su_ref, sd_ref, bd_ref, *out_refs, kinds, widths):
    h = _rmsnorm(x_ref[...], g_ref[...]).astype(BF16)
    bd = bd_ref[...]
    col = 0
    for o_ref, width in zip(out_refs, widths):
        z_all = _dot(h, w_ref[:, col:col + width])
        for c in range(width // LANES):
            z = z_all[:, c * LANES:(c + 1) * LANES]
            kind = kinds[(col // LANES) + c]
            if kind >= 0:
                hi, lo = _split2(z * z)
                ss = _dot(hi, bd) + _dot(lo, bd)
                y = z * lax.rsqrt(ss * (1.0 / HEAD_DIM) + NORM_EPS) * gain_ref[kind:kind + 1, :]
                y = (y * c_ref[...] + pltpu.roll(y, LANES - ROT_DIM // 2, 1) * su_ref[...]
                     + pltpu.roll(y, ROT_DIM // 2, 1) * sd_ref[...])
                if kind % 2 == 0:
                    y = y * (HEAD_DIM ** -0.5)
                z = y
            o_ref[:, c * LANES:(c + 1) * LANES] = z
        col += width


def _proj_call(name, x2d, g, w, gains, tables, bd, kinds, widths, tm):
    t = x2d.shape[0]
    wtot = sum(widths)
    pos_blocks = tables[0].shape[0] // tm
    tab_spec = pl.BlockSpec((tm, LANES), lambda i: (i % pos_blocks, 0))
    return pl.pallas_call(
        functools.partial(_proj_body, kinds=kinds, widths=widths),
        name=name,
        grid=(t // tm,),
        in_specs=[
            pl.BlockSpec((tm, D_MODEL), lambda i: (i, 0)),
            pl.BlockSpec((1, D_MODEL), lambda i: (0, 0)),
            pl.BlockSpec((D_MODEL, wtot), lambda i: (0, 0)),
            pl.BlockSpec((4, LANES), lambda i: (0, 0)),
            tab_spec, tab_spec, tab_spec,
            pl.BlockSpec((LANES, LANES), lambda i: (0, 0)),
        ],
        out_specs=[pl.BlockSpec((tm, w_), lambda i: (i, 0)) for w_ in widths],
        out_shape=[jax.ShapeDtypeStruct((t, w_), F32) for w_ in widths],
        compiler_params=_cparams(("parallel",), VMEM_LIMIT),
    )(x2d, g, w, gains, *tables, bd)


def _band_softmax(s_c, s_p, mask_c, mask_p, sink):
    neg = -jnp.inf
    s_c = jnp.where(mask_c, s_c, neg)
    m = jnp.max(s_c, axis=1, keepdims=True)
    if s_p is not None:
        s_p = jnp.where(mask_p, s_p, neg)
        m = jnp.maximum(m, jnp.max(s_p, axis=1, keepdims=True))
    if sink is not None:
        m = jnp.maximum(m, sink)
    p_c = jnp.exp(s_c - m)
    denom = jnp.sum(p_c, axis=1, keepdims=True)
    p_p = None
    if s_p is not None:
        p_p = jnp.exp(s_p - m)
        denom = denom + jnp.sum(p_p, axis=1, keepdims=True)
    if sink is not None:
        denom = denom + jnp.exp(sink - m)
    return p_c, p_p, denom, m


def _attn_a_body(sink_ref, q_ref, kc_ref, kp_ref, o_ref):
    has_prev = pl.program_id(1) > 0
    qi = lax.broadcasted_iota(jnp.int32, (BLOCK, BLOCK), 0)
    kj = lax.broadcasted_iota(jnp.int32, (BLOCK, BLOCK), 1)
    mask_c = kj <= qi
    mask_p = jnp.logical_and(kj > qi, has_prev)
    low = lax.broadcasted_iota(jnp.int32, (BLOCK, LANES), 1) < HEAD_DIM
    groups = A_HEADS // A_KV_HEADS
    outs = {}
    for hk in range(A_KV_HEADS):
        colk = (hk // 2) * LANES
        k_low = (hk % 2) == 0
        sel = low if k_low else jnp.logical_not(low)
        kc = jnp.where(sel, kc_ref[:, colk:colk + LANES], 0.0).astype(BF16)
        kp = jnp.where(sel, kp_ref[:, colk:colk + LANES], 0.0).astype(BF16)
        vc = kc_ref[:, A_KV_W + colk:A_KV_W + colk + LANES].astype(BF16)
        vp = kp_ref[:, A_KV_W + colk:A_KV_W + colk + LANES].astype(BF16)
        for g in range(groups):
            j = hk * groups + g
            q_low = (j % 2) == 0
            q = q_ref[:, (j // 2) * LANES:(j // 2 + 1) * LANES]
            if q_low != k_low:
                q = pltpu.roll(q, HEAD_DIM, 1)
            q = q.astype(BF16)
            p_c, p_p, denom, _ = _band_softmax(_dot_nt(q, kc), _dot_nt(q, kp), mask_c, mask_p, sink_ref[j])
            o = (_dot(p_c.astype(BF16), vc) + _dot(p_p.astype(BF16), vp)) * (1.0 / denom)
            if q_low != k_low:
                o = pltpu.roll(o, HEAD_DIM, 1)
            outs[j] = o
    for c in range(A_HEADS // 2):
        o_ref[:, c * LANES:(c + 1) * LANES] = jnp.where(low, outs[2 * c], outs[2 * c + 1]).astype(o_ref.dtype)


def _attn_a_call(qa, kva, sinks, n, s):
    nb = s // BLOCK
    return pl.pallas_call(
        _attn_a_body,
        name="attn_a",
        grid=(n, nb),
        in_specs=[
            pl.BlockSpec(memory_space=pltpu.SMEM),
            pl.BlockSpec((BLOCK, A_Q_W), lambda i, b: (i * nb + b, 0)),
            pl.BlockSpec((BLOCK, 2 * A_KV_W), lambda i, b: (i * nb + b, 0)),
            pl.BlockSpec((BLOCK, 2 * A_KV_W), lambda i, b: (i * nb + jnp.maximum(b - 1, 0), 0)),
        ],
        out_specs=pl.BlockSpec((BLOCK, A_Q_W), lambda i, b: (i * nb + b, 0)),
        out_shape=jax.ShapeDtypeStruct((n * s, A_Q_W), BF16),
        compiler_params=_cparams(("parallel", "arbitrary")),
    )(sinks, qa, kva, kva)


def _attn_b_unit(q_ref, k_ref, v_ref, o_scr, l_scr, g, start_c, start_p, has_prev, dil, use_prev):
    def rows(ref, start):
        if dil == 1:
            return ref[pl.ds(start, BLOCK), :]
        return ref[pl.ds(start, BLOCK, stride=dil), :]

    qi = lax.broadcasted_iota(jnp.int32, (BLOCK, BLOCK), 0)
    kj = lax.broadcasted_iota(jnp.int32, (BLOCK, BLOCK), 1)
    mask_c = kj <= qi
    low = lax.broadcasted_iota(jnp.int32, (BLOCK, LANES), 1) < HEAD_DIM
    q = rows(q_ref, start_c).astype(BF16)
    kc = rows(k_ref, start_c)
    vc = rows(v_ref, start_c).astype(BF16)
    if use_prev:
        mask_p = jnp.logical_and(kj >= qi, has_prev)
        kp = rows(k_ref, start_p)
        vp = rows(v_ref, start_p).astype(BF16)
    halves_o, halves_l = [], []
    for half in range(2):
        sel = low if half == 0 else jnp.logical_not(low)
        kcz = jnp.where(sel, kc, 0.0).astype(BF16)
        s_p = _dot_nt(q, jnp.where(sel, kp, 0.0).astype(BF16)) if use_prev else None
        p_c, p_p, denom, m = _band_softmax(_dot_nt(q, kcz), s_p, mask_c, mask_p if use_prev else None, None)
        o = _dot(p_c.astype(BF16), vc)
        if use_prev:
            o = o + _dot(p_p.astype(BF16), vp)
        halves_o.append(o * (1.0 / denom))
        halves_l.append(jnp.broadcast_to(m + jnp.log(denom), (BLOCK, LANES)))
    o = jnp.where(low, halves_o[0], halves_o[1])
    lse = jnp.where(low, halves_l[0], halves_l[1])
    if dil == 1:
        o_scr[g, pl.ds(start_c, BLOCK), :] = o
        l_scr[g, pl.ds(start_c, BLOCK), :] = lse
    else:
        o_scr[g, pl.ds(start_c, BLOCK, stride=dil), :] = o
        l_scr[g, pl.ds(start_c, BLOCK, stride=dil), :] = lse


def _attn_b_body(q1, q2, q3, k1, v1, k2, v2, k3, v3, o_ref, o_scr, l_scr, *, s):
    qs, ks, vs = (q1, q2, q3), (k1, k2, k3), (v1, v2, v3)
    for g, (_, dil) in enumerate(B_PATTERNS):
        span = BLOCK * dil
        nblk = max(s // span, 1)
        use_prev = nblk > 1

        def unit(u, carry, g=g, dil=dil, span=span, use_prev=use_prev):
            r = u % dil
            b = u // dil
            start_c = b * span + r
            start_p = jnp.maximum(b - 1, 0) * span + r
            if dil == 1:
                start_c = pl.multiple_of(start_c, BLOCK)
                start_p = pl.multiple_of(start_p, BLOCK)
            _attn_b_unit(qs[g], ks[g], vs[g], o_scr, l_scr, g, start_c, start_p, b > 0, dil, use_prev)
            return carry

        lax.fori_loop(0, nblk * dil, unit, 0)

    chunk = 256

    def comb(c, carry):
        r0 = pl.multiple_of(c * chunk, chunk)
        l0, l1, l2 = (l_scr[g, pl.ds(r0, chunk), :] for g in range(B_GROUPS))
        m = jnp.maximum(jnp.maximum(l0, l1), l2)
        w0, w1, w2 = jnp.exp(l0 - m), jnp.exp(l1 - m), jnp.exp(l2 - m)
        inv = 1.0 / (w0 + w1 + w2)
        acc = (o_scr[0, pl.ds(r0, chunk), :] * (w0 * inv) + o_scr[1, pl.ds(r0, chunk), :] * (w1 * inv)
               + o_scr[2, pl.ds(r0, chunk), :] * (w2 * inv))
        o_ref[pl.ds(r0, chunk), :] = acc.astype(o_ref.dtype)
        return carry

    lax.fori_loop(0, s // chunk, comb, 0)


def _attn_b_call(qb, kvs, n, s):
    pairs = B_GW // LANES
    in_specs = [pl.BlockSpec((s, LANES), lambda i, h, g=g: (i, g * pairs + h)) for g in range(B_GROUPS)]
    args = [qb, qb, qb]
    for kv in kvs:
        in_specs.append(pl.BlockSpec((s, LANES), lambda i, h: (i, h)))
        in_specs.append(pl.BlockSpec((s, LANES), lambda i, h: (i, pairs + h)))
        args += [kv, kv]
    return pl.pallas_call(
        functools.partial(_attn_b_body, s=s),
        name="attn_b",
        grid=(n, pairs),
        in_specs=in_specs,
        out_specs=pl.BlockSpec((s, LANES), lambda i, h: (i, h)),
        out_shape=jax.ShapeDtypeStruct((n * s, B_GW), BF16),
        scratch_shapes=[pltpu.VMEM((B_GROUPS, s, LANES), F32), pltpu.VMEM((B_GROUPS, s, LANES), F32)],
        compiler_params=_cparams(("parallel", "arbitrary"), VMEM_LIMIT),
    )(*args)


def _head_sum(prod, e):
    hi, lo = _split2(prod)
    return _dot(hi, e) + _dot(lo, e)


def _expand3(v, et):
    hi, mid, lo = _split3(v)
    return _dot(hi, et) + _dot(mid, et) + _dot(lo, et)


def _sample_group(q, newkv, cache_refs, dil, min_row, kw, e, et, xk, sink_row, o_scr, l_scr, slot):
    def widen(a):
        a = a.astype(BF16)
        return _dot(a, xk) if xk is not None else a.astype(F32)

    def cache_rows(chunks, rho):
        if dil == 1:
            parts = [r[...] for r in chunks]
        else:
            parts = [r[pl.ds(rho, BLOCK, stride=dil), :] for r in chunks]
        return parts[0] if len(parts) == 1 else jnp.concatenate(parts, axis=1)

    nk = kw // LANES
    newk = widen(newkv[:, :kw])
    newv = widen(newkv[:, kw:])
    ci = lax.broadcasted_iota(jnp.int32, (BLOCK, 1), 0)
    ni = lax.broadcasted_iota(jnp.int32, (SAMPLE_T, 1), 0)
    for t in range(SAMPLE_T):
        rho = t % dil
        kc = widen(cache_rows(cache_refs[:nk], rho))
        vc = widen(cache_rows(cache_refs[nk:], rho))
        qt = q[t:t + 1, :].astype(BF16).astype(F32)
        s_c = _head_sum(kc * qt, e)
        s_n = _head_sum(newk * qt, e)
        valid_c = ci >= (t // dil + min_row)
        valid_n = jnp.logical_and(ni <= t, (t - ni) % dil == 0)
        s_c = jnp.where(valid_c, s_c, -jnp.inf)
        s_n = jnp.where(valid_n, s_n, -jnp.inf)
        m = jnp.maximum(jnp.max(s_c, axis=0, keepdims=True), jnp.max(s_n, axis=0, keepdims=True))
        if sink_row is not None:
            m = jnp.maximum(m, sink_row)
        p_c = jnp.exp(s_c - m)
        p_n = jnp.exp(s_n - m)
        denom = jnp.sum(p_c, axis=0, keepdims=True) + jnp.sum(p_n, axis=0, keepdims=True)
        if sink_row is not None:
            denom = denom + jnp.exp(sink_row - m)
        inv = 1.0 / denom
        pe_c = _dot((p_c * inv).astype(BF16), et)
        pe_n = _dot((p_n * inv).astype(BF16), et)
        o = jnp.sum(pe_c * vc, axis=0, keepdims=True) + jnp.sum(pe_n * newv, axis=0, keepdims=True)
        o_scr[slot, t:t + 1, :] = o
        if l_scr is not None:
            l_scr[slot, t:t + 1, :] = m + jnp.log(denom)


def _shift_copy(cache_refs, newkv, out_ref, w):
    keep = w - SAMPLE_T
    chunk = 256
    nfull = keep // chunk
    rem = keep - nfull * chunk
    for c, cache_ref in enumerate(cache_refs):
        lanes = slice(c * LANES, (c + 1) * LANES)
        if nfull > 0:
            def body(k, carry, cache_ref=cache_ref, lanes=lanes):
                r0 = pl.multiple_of(k * chunk, chunk)
                out_ref[pl.ds(r0, chunk), lanes] = cache_ref[pl.ds(r0 + SAMPLE_T, chunk), :]
                return carry
            lax.fori_loop(0, nfull, body, 0)
        if rem > 0:
            out_ref[nfull * chunk:keep, lanes] = cache_ref[nfull * chunk + SAMPLE_T:w, :]
    out_ref[keep:w, :] = newkv


def _sample_body(*refs):
    n_a, n_b = 2 * A_KV_W // LANES, 2 * B_GW // LANES
    sink_ref, e16_ref, e16t_ref, xk_ref, e8_ref, e8t_ref, qa_ref, kva_ref, qb_ref = refs[:9]
    new_refs = refs[9:12]
    pos = 12
    ca_refs = refs[pos:pos + n_a]
    pos += n_a
    cache_refs = []
    for _ in range(B_GROUPS):
        cache_refs.append(refs[pos:pos + n_b])
        pos += n_b
    oa_ref, ob_ref, na_ref = refs[pos:pos + 3]
    out_refs = refs[pos + 3:pos + 6]
    oa_scr, ob_scr, lb_scr = refs[pos + 6:]

    kva = kva_ref[...]
    _sample_group(qa_ref[...], kva, ca_refs, 1, 1, A_KV_W, e16_ref[...], e16t_ref[...], xk_ref[...],
                  sink_ref[...], oa_scr, None, 0)
    oa_ref[...] = oa_scr[0]
    _shift_copy(ca_refs, kva, na_ref, A_WINDOW)

    qb = qb_ref[...]
    e8, e8t = e8_ref[...], e8t_ref[...]
    for g, (win, dil) in enumerate(B_PATTERNS):
        newkv = new_refs[g][...]
        _sample_group(qb[:, g * B_GW:(g + 1) * B_GW], newkv, cache_refs[g], dil, 0, B_GW, e8, e8t, None,
                      None, ob_scr, lb_scr, g)
        _shift_copy(cache_refs[g], newkv, out_refs[g], win)
    l0, l1, l2 = lb_scr[0], lb_scr[1], lb_scr[2]
    m = jnp.maximum(jnp.maximum(l0, l1), l2)
    w0, w1, w2 = jnp.exp(l0 - m), jnp.exp(l1 - m), jnp.exp(l2 - m)
    inv = 1.0 / (w0 + w1 + w2)
    ob_ref[...] = (ob_scr[0] * _expand3(w0 * inv, e8t) + ob_scr[1] * _expand3(w1 * inv, e8t)
                   + ob_scr[2] * _expand3(w2 * inv, e8t))


def _head_matrix(width):
    lane_head = np.arange(width) // HEAD_DIM
    e = (lane_head[:, None] == np.arange(LANES)[None, :]).astype(np.float32)
    return jnp.asarray(e, BF16), jnp.asarray(e.T, BF16)


def _sample_call(sink_row, qa, kva, qb, kvbs, caches, ns):
    e16, e16t = _head_matrix(A_Q_W)
    e8, e8t = _head_matrix(B_GW)
    src_head = np.arange(A_Q_W) // HEAD_DIM // (A_HEADS // A_KV_HEADS)
    src_lane = src_head * HEAD_DIM + np.arange(A_Q_W) % HEAD_DIM
    xk = jnp.asarray((np.arange(A_KV_W)[:, None] == src_lane[None, :]).astype(np.float32), BF16)

    def const(a):
        return pl.BlockSpec(a.shape, lambda i: (0,) * a.ndim)

    def rows(width):
        return pl.BlockSpec((SAMPLE_T, width), lambda i: (i, 0))

    def cache(c):
        return pl.BlockSpec((None,) + c.shape[1:], lambda i: (i, 0, 0))

    consts = [sink_row, e16, e16t, xk, e8, e8t]
    news = [qa, kva, qb, *kvbs]
    in_specs = [const(a) for a in consts] + [rows(a.shape[1]) for a in news]
    cache_args = []
    for c in caches:
        for k in range(c.shape[2] // LANES):
            in_specs.append(pl.BlockSpec((None, c.shape[1], LANES), lambda i, k=k: (i, 0, k)))
            cache_args.append(c)
    out_shape = ([jax.ShapeDtypeStruct((ns * SAMPLE_T, A_Q_W), F32), jax.ShapeDtypeStruct((ns * SAMPLE_T, B_GW), F32)]
                 + [jax.ShapeDtypeStruct(c.shape, F32) for c in caches])
    out_specs = [rows(A_Q_W), rows(B_GW)] + [cache(c) for c in caches]
    return pl.pallas_call(
        _sample_body,
        name="sample_attn",
        grid=(ns,),
        in_specs=in_specs,
        out_specs=out_specs,
        out_shape=out_shape,
        scratch_shapes=[pltpu.VMEM((1, SAMPLE_T, A_Q_W), F32), pltpu.VMEM((B_GROUPS, SAMPLE_T, B_GW), F32),
                        pltpu.VMEM((B_GROUPS, SAMPLE_T, LANES), F32)],
        compiler_params=_cparams(("parallel",), VMEM_LIMIT),
    )(*consts, *news, *cache_args)


def _merge_body(x_ref, g_ref, oa_ref, ob_ref, wga_ref, wgb_ref, wba_ref, wbb_ref, o_ref, h_scr):
    @pl.when(pl.program_id(1) == 0)
    def _():
        h_scr[...] = _rmsnorm(x_ref[...], g_ref[...]).astype(BF16)

    h = h_scr[...]
    ga = _dot(h, wga_ref[...])
    gb = _dot(h, wgb_ref[...])
    ya = _dot(oa_ref[...].astype(BF16), wba_ref[...])
    yb = _dot(ob_ref[...].astype(BF16), wbb_ref[...])
    o_ref[...] = (jax.nn.sigmoid(ga) * ya + jax.nn.sigmoid(gb) * yb).astype(o_ref.dtype)


def _merge_call(x2d, g, oa, ob, w_gates, w_ba, w_bb, tm, tn):
    t = x2d.shape[0]
    ncol = D_MODEL // tn
    return pl.pallas_call(
        _merge_body,
        name="merge",
        grid=(t // tm, ncol),
        in_specs=[
            pl.BlockSpec((tm, D_MODEL), lambda i, j: (i, 0)),
            pl.BlockSpec((1, D_MODEL), lambda i, j: (0, 0)),
            pl.BlockSpec((tm, A_Q_W), lambda i, j: (i, 0)),
            pl.BlockSpec((tm, B_GW), lambda i, j: (i, 0)),
            pl.BlockSpec((D_MODEL, tn), lambda i, j: (0, j)),
            pl.BlockSpec((D_MODEL, tn), lambda i, j: (0, ncol + j)),
            pl.BlockSpec((A_Q_W, tn), lambda i, j: (0, j)),
            pl.BlockSpec((B_GW, tn), lambda i, j: (0, j)),
        ],
        out_specs=pl.BlockSpec((tm, tn), lambda i, j: (i, j)),
        out_shape=jax.ShapeDtypeStruct((t, D_MODEL), BF16),
        scratch_shapes=[pltpu.VMEM((tm, D_MODEL), BF16)],
        compiler_params=_cparams(("parallel", "arbitrary"), VMEM_LIMIT),
    )(x2d, g, oa, ob, w_gates, w_gates, w_ba, w_bb)


def _route(logits):
    tm = logits.shape[0]
    lane = lax.broadcasted_iota(jnp.int32, (tm, LANES), 1)
    neg = -jnp.inf
    big = LANES

    def first_where(cond):
        return jnp.min(jnp.where(cond, lane, big), axis=1, keepdims=True)

    gl = jnp.where(lane < N_EXPERT_GROUPS, logits, neg)
    gmax = jnp.max(gl, axis=1, keepdims=True)
    gidx = first_where(gl == gmax)
    g_w = 1.0 / jnp.sum(jnp.exp(gl - gmax), axis=1, keepdims=True)
    lo = N_EXPERT_GROUPS + gidx * EXPERTS_PER_GROUP
    in_grp = jnp.logical_and(lane >= lo, lane < lo + EXPERTS_PER_GROUP)
    el = jnp.where(in_grp, logits, neg)
    ep = jnp.exp(el - jnp.max(el, axis=1, keepdims=True))
    prob = ep / jnp.sum(ep, axis=1, keepdims=True)
    prob = jnp.where(in_grp, prob, -1.0)
    p1 = jnp.max(prob, axis=1, keepdims=True)
    i1 = first_where(prob == p1)
    prob2 = jnp.where(lane == i1, -1.0, prob)
    p2 = jnp.max(prob2, axis=1, keepdims=True)
    i2 = first_where(prob2 == p2)
    tot = p1 + p2
    c1 = g_w * (p1 / tot)
    c2 = g_w * (p2 / tot)
    e1 = (i1 - N_EXPERT_GROUPS).astype(F32)
    e2 = (i2 - N_EXPERT_GROUPS).astype(F32)
    return e1, e2, c1, c2


def _outproj_body(mp_ref, xp_ref, ms_ref, xs_ref, wo_ref, g_ref, wrh_ref, wrl_ref, br_ref, x1_ref, route_ref, *, n_prompt):
    def compute(m_ref, x_ref):
        x1 = x_ref[...] + _dot(m_ref[...], wo_ref[...])
        x1_ref[...] = x1
        hi, lo = _split2(_rmsnorm(x1, g_ref[...]))
        logits = _dot(hi, wrh_ref[...]) + _dot(hi, wrl_ref[...]) + _dot(lo, wrh_ref[...]) + br_ref[...]
        e1, e2, c1, c2 = _route(logits)
        lane = lax.broadcasted_iota(jnp.int32, logits.shape, 1)
        route_ref[...] = jnp.where(lane == 0, e1, jnp.where(lane == 1, e2, jnp.where(lane == 2, c1, jnp.where(lane == 3, c2, 0.0))))

    i = pl.program_id(0)
    pl.when(i < n_prompt)(lambda: compute(mp_ref, xp_ref))
    pl.when(i >= n_prompt)(lambda: compute(ms_ref, xs_ref))


def _outproj_call(merged_p, xp, merged_s, xs, w_o, g, wr_hi, wr_lo, br, tm):
    n_p, n_s = xp.shape[0] // tm, xs.shape[0] // tm
    t_all = xp.shape[0] + xs.shape[0]

    def p_map(i):
        return (jnp.minimum(i, n_p - 1), 0)

    def s_map(i):
        return (jnp.maximum(i - n_p, 0), 0)

    def const(shape):
        return pl.BlockSpec(shape, lambda i: (0, 0))

    return pl.pallas_call(
        functools.partial(_outproj_body, n_prompt=n_p),
        name="outproj",
        grid=(n_p + n_s,),
        in_specs=[
            pl.BlockSpec((tm, D_MODEL), p_map), pl.BlockSpec((tm, D_MODEL), p_map),
            pl.BlockSpec((tm, D_MODEL), s_map), pl.BlockSpec((tm, D_MODEL), s_map),
            const((D_MODEL, D_MODEL)), const((1, D_MODEL)), const((D_MODEL, LANES)), const((D_MODEL, LANES)),
            const((1, LANES)),
        ],
        out_specs=[pl.BlockSpec((tm, D_MODEL), lambda i: (i, 0)), pl.BlockSpec((tm, LANES), lambda i: (i, 0))],
        out_shape=[jax.ShapeDtypeStruct((t_all, D_MODEL), F32), jax.ShapeDtypeStruct((t_all, LANES), F32)],
        compiler_params=_cparams(("arbitrary",), VMEM_LIMIT),
    )(merged_p, xp, merged_s, xs, w_o, g, wr_hi, wr_lo, br)


def _row_copy(src, src_row, dst, dst_row, sem):
    return pltpu.make_async_copy(src.at[pl.ds(src_row, 1), :], dst.at[pl.ds(dst_row, 1), :], sem)


def _dispatch_body(pos_ref, pad_tile_ref, x1_ref, g_ref, xs_hbm, h_scr, sem, *, tile):
    tm = h_scr.shape[0]

    @pl.when(pl.program_id(0) == 0)
    def _():
        h_scr[...] = jnp.zeros_like(h_scr)

        def pad_copy(t, k):
            row = pl.multiple_of(t * tile + k * tm, tm)
            return pltpu.make_async_copy(h_scr, xs_hbm.at[pl.ds(row, tm), :], sem)

        n_used = pad_tile_ref[N_EXPERTS]
        for phase in ("start", "wait"):
            def expert_pad(e, carry, phase=phase):
                @pl.when(pad_tile_ref[e] >= 0)
                def _():
                    for k in range(tile // tm):
                        getattr(pad_copy(pad_tile_ref[e], k), phase)()
                return carry
            lax.fori_loop(0, N_EXPERTS, expert_pad, 0)

            def unused_tile(t, carry, phase=phase):
                @pl.when(t >= n_used)
                def _():
                    for k in range(tile // tm):
                        getattr(pad_copy(t, k), phase)()
                return carry
            lax.fori_loop(0, xs_hbm.shape[0] // tile, unused_tile, 0)

    h_scr[...] = _rmsnorm(x1_ref[...], g_ref[...])

    def issue(j, carry):
        _row_copy(h_scr, j, xs_hbm, pos_ref[0, 0, 2 * j], sem).start()
        _row_copy(h_scr, j, xs_hbm, pos_ref[0, 0, 2 * j + 1], sem).start()
        return carry

    lax.fori_loop(0, tm, issue, 0)

    def drain(j, carry):
        _row_copy(h_scr, j, xs_hbm, pos_ref[0, 0, 2 * j], sem).wait()
        _row_copy(h_scr, j, xs_hbm, pos_ref[0, 0, 2 * j + 1], sem).wait()
        return carry

    lax.fori_loop(0, tm, drain, 0)


def _dispatch_call(pos3, pad_tile, x1, g, rows, tm, tile):
    t = x1.shape[0]
    assert tile % tm == 0
    return pl.pallas_call(
        functools.partial(_dispatch_body, tile=tile),
        name="dispatch",
        grid=(t // tm,),
        in_specs=[
            pl.BlockSpec((1, 1, 2 * tm), lambda i: (i, 0, 0), memory_space=pltpu.SMEM),
            pl.BlockSpec(memory_space=pltpu.SMEM),
            pl.BlockSpec((tm, D_MODEL), lambda i: (i, 0)),
            pl.BlockSpec((1, D_MODEL), lambda i: (0, 0)),
        ],
        out_specs=pl.BlockSpec(memory_space=pl.ANY),
        out_shape=jax.ShapeDtypeStruct((rows, D_MODEL), F32),
        scratch_shapes=[pltpu.VMEM((tm, D_MODEL), F32), pltpu.SemaphoreType.DMA(())],
        compiler_params=_cparams(("arbitrary",)),
    )(pos3, pad_tile, x1, g)


def _moe_body(te_ref, tv_ref, nu_ref, xs_ref, wg_ref, wu_ref, wd_ref, ys_ref):
    i = pl.program_id(0)

    @pl.when(i < nu_ref[0])
    def _():
        row = lax.broadcasted_iota(jnp.int32, xs_ref.shape, 0)
        x = jnp.where(row < tv_ref[i], xs_ref[...], 0.0).astype(BF16)
        a = jax.nn.silu(_dot(x, wg_ref[0])) * _dot(x, wu_ref[0])
        ys_ref[...] = _dot(a.astype(BF16), wd_ref[0])

    @pl.when(i >= nu_ref[0])
    def _():
        ys_ref[...] = jnp.zeros_like(ys_ref)


def _moe_call(tile_expert, tile_valid, n_used, xs, w_gate, w_up, w_down, tm):
    rows = xs.shape[0]

    def row_map(i, te, tv, nu):
        return (jnp.minimum(i, nu[0] - 1), 0)

    def out_map(i, te, tv, nu):
        return (i, 0)

    def w_map(i, te, tv, nu):
        return (te[i], 0, 0)

    return pl.pallas_call(
        _moe_body,
        name="moe",
        grid_spec=pltpu.PrefetchScalarGridSpec(
            num_scalar_prefetch=3,
            grid=(rows // tm,),
            in_specs=[
                pl.BlockSpec((tm, D_MODEL), row_map),
                pl.BlockSpec((1, D_MODEL, D_EXPERT), w_map),
                pl.BlockSpec((1, D_MODEL, D_EXPERT), w_map),
                pl.BlockSpec((1, D_EXPERT, D_MODEL), w_map),
            ],
            out_specs=pl.BlockSpec((tm, D_MODEL), out_map),
        ),
        out_shape=jax.ShapeDtypeStruct((rows, D_MODEL), F32),
        compiler_params=_cparams(("arbitrary",), VMEM_LIMIT),
    )(tile_expert, tile_valid, n_used, xs, w_gate, w_up, w_down)


def _combine_body(pos_ref, x1_ref, route_ref, ys_hbm, y_ref, buf0, buf1, sem):
    tm = buf0.shape[0]

    def issue(j, carry):
        _row_copy(ys_hbm, pos_ref[0, 0, 2 * j], buf0, j, sem).start()
        _row_copy(ys_hbm, pos_ref[0, 0, 2 * j + 1], buf1, j, sem).start()
        return carry

    lax.fori_loop(0, tm, issue, 0)

    def drain(j, carry):
        _row_copy(ys_hbm, pos_ref[0, 0, 2 * j], buf0, j, sem).wait()
        _row_copy(ys_hbm, pos_ref[0, 0, 2 * j + 1], buf1, j, sem).wait()
        return carry

    lax.fori_loop(0, tm, drain, 0)
    route = route_ref[...]
    y_ref[...] = x1_ref[...] + (route[:, 2:3] * buf0[...] + route[:, 3:4] * buf1[...])


def _combine_call(name, pos3, x1_all, route, ys, t, row0, tm):
    off = row0 // tm
    return pl.pallas_call(
        _combine_body,
        name=name,
        grid=(t // tm,),
        in_specs=[
            pl.BlockSpec((1, 1, 2 * tm), lambda i: (i + off, 0, 0), memory_space=pltpu.SMEM),
            pl.BlockSpec((tm, D_MODEL), lambda i: (i + off, 0)),
            pl.BlockSpec((tm, LANES), lambda i: (i + off, 0)),
            pl.BlockSpec(memory_space=pl.ANY),
        ],
        out_specs=pl.BlockSpec((tm, D_MODEL), lambda i: (i, 0)),
        out_shape=jax.ShapeDtypeStruct((t, D_MODEL), F32),
        scratch_shapes=[pltpu.VMEM((tm, D_MODEL), F32), pltpu.VMEM((tm, D_MODEL), F32), pltpu.SemaphoreType.DMA(())],
        compiler_params=_cparams(("arbitrary",)),
    )(pos3, x1_all, route, ys)


def _routing_offsets(route, tm, n_tiles):
    e = route[:, :2].astype(jnp.int32)
    onehot = jnp.sum((e[:, :, None] == jnp.arange(N_EXPERTS, dtype=jnp.int32)[None, None, :]).astype(jnp.int32), axis=1)
    csum = jnp.cumsum(onehot, axis=0)
    rank = csum - onehot
    counts = csum[-1]
    tiles_e = (counts + tm - 1) // tm
    tile_end = jnp.cumsum(tiles_e)
    tile_start = tile_end - tiles_e
    pos = tile_start[e] * tm + jnp.take_along_axis(rank, e, axis=1)
    n_used = tile_end[-1]
    tile_id = jnp.minimum(jnp.arange(n_tiles, dtype=jnp.int32), n_used - 1)
    tile_expert = jnp.sum((tile_id[:, None] >= tile_end[None, :]).astype(jnp.int32), axis=1)
    tile_valid = jnp.clip(counts[tile_expert] - (tile_id - tile_start[tile_expert]) * tm, 0, tm)
    pad_tile = jnp.concatenate([jnp.where(tiles_e > 0, tile_end - 1, -1), n_used.reshape(1)])
    i32 = jnp.int32
    return (pos.astype(i32), tile_expert.astype(i32), tile_valid.astype(i32), n_used.reshape(1).astype(i32),
            pad_tile.astype(i32))


def _pick_tile(t, pref):
    tm = min(pref, t)
    assert t % tm == 0, (t, tm)
    return tm


def kernel(x_prompt, x_sample, cache_a_kv, cache_b1_kv, cache_b2_kv, cache_b3_kv, g_attn_norm, w_in, q_norm_a, k_norm_a, q_norm_b, k_norm_b, sinks_a, w_branch_a, w_branch_b, w_out, g_ffn_norm, w_group_router, b_group_router, w_expert_router, b_expert_router, w_expert_gate, w_expert_up, w_expert_down):
    n, s, _ = x_prompt.shape
    ns, ts, _ = x_sample.shape
    assert ts == SAMPLE_T and s == B_PATTERNS[-1][0] and x_prompt.shape[2] == D_MODEL
    assert g_attn_norm.shape[0] == 1, "single layer"
    tp, tsmp = n * s, ns * ts
    t_all = tp + tsmp
    xp = x_prompt.reshape(tp, D_MODEL)
    xs_in = x_sample.reshape(tsmp, D_MODEL)

    w = w_in[0]
    o_qa, o_ka, o_va = 0, A_Q_W, A_Q_W + A_KV_W
    o_qb = o_va + A_KV_W
    o_kb, o_vb = o_qb + B_W, o_qb + 2 * B_W
    o_g = o_vb + B_W
    cols_a = [w[:, o_qa:o_ka], w[:, o_ka:o_va], w[:, o_va:o_qb]]
    cols_b = []
    for g in range(B_GROUPS):
        cols_b += [w[:, o_kb + g * B_GW:o_kb + (g + 1) * B_GW], w[:, o_vb + g * B_GW:o_vb + (g + 1) * B_GW]]
    w_pa = jnp.concatenate(cols_a, axis=1).astype(BF16)
    w_pq = w[:, o_qb:o_kb].astype(BF16)
    w_pkv = jnp.concatenate(cols_b, axis=1).astype(BF16)
    w_gates = w[:, o_g:].astype(BF16)
    rep = LANES // HEAD_DIM
    gains = jnp.stack([jnp.tile(v[0], rep) for v in (q_norm_a, k_norm_a, q_norm_b, k_norm_b)]).astype(F32)
    bd = jnp.asarray((np.arange(LANES)[:, None] // HEAD_DIM == np.arange(LANES)[None, :] // HEAD_DIM).astype(np.float32), BF16)
    g_attn = g_attn_norm.astype(F32)
    g_ffn = g_ffn_norm.astype(F32)
    nq, nk = A_Q_W // LANES, A_KV_W // LANES
    kinds_a = (0,) * nq + (1,) * nk + (-1,) * nk
    kinds_q = (2,) * (B_W // LANES)
    kinds_kv = ((3,) * (B_GW // LANES) + (-1,) * (B_GW // LANES)) * B_GROUPS

    def project(x2d, pos, tm):
        tables = _rope_tables(pos)
        qa, kva = _proj_call("proj_a", x2d, g_attn, w_pa, gains, tables, bd, kinds_a, (A_Q_W, 2 * A_KV_W), tm)
        (qb,) = _proj_call("proj_qb", x2d, g_attn, w_pq, gains, tables, bd, kinds_q, (B_W,), tm)
        kvb = _proj_call("proj_kvb", x2d, g_attn, w_pkv, gains, tables, bd, kinds_kv, (2 * B_GW,) * B_GROUPS, tm)
        return qa, kva, qb, kvb

    tm_p = _pick_tile(tp, ROW_TILE)
    qa, kva, qb, kvb = project(xp, jnp.arange(s, dtype=jnp.int32), tm_p)
    oa_p = _attn_a_call(qa, kva, sinks_a[0].astype(F32), n, s)
    ob_p = _attn_b_call(qb, kvb, n, s)
    new_a_p = kva.reshape(n, s, 2, A_KV_HEADS, HEAD_DIM)[None, :, s - min(A_WINDOW, s):]
    new_b_p = [kvb[g].reshape(n, s, 2, B_HPG, HEAD_DIM)[None, :, s - min(win, s):] for g, (win, _) in enumerate(B_PATTERNS)]

    tm_s = _pick_tile(tsmp, ROW_TILE)
    pos_s = PAST_LEN + (jnp.arange(tm_s, dtype=jnp.int32) % ts)
    qa_s, kva_s, qb_s, kvb_s = project(xs_in, pos_s, tm_s)
    caches = [cache_a_kv[0].reshape(ns, A_WINDOW, 2 * A_KV_W)]
    caches += [c[0].reshape(ns, win, 2 * B_GW) for c, (win, _) in zip((cache_b1_kv, cache_b2_kv, cache_b3_kv), B_PATTERNS)]
    sink_row = jnp.zeros((1, LANES), F32).at[0, :A_HEADS].set(sinks_a[0].astype(F32))
    oa_s, ob_s, na, nb1, nb2, nb3 = _sample_call(sink_row, qa_s, kva_s, qb_s, kvb_s, caches, ns)
    new_a_s = na.reshape(1, ns, A_WINDOW, 2, A_KV_HEADS, HEAD_DIM)
    new_b_s = [c.reshape(1, ns, win, 2, B_HPG, HEAD_DIM) for c, (win, _) in zip((nb1, nb2, nb3), B_PATTERNS)]

    w_ba = w_branch_a[0].astype(BF16)
    w_bb = w_branch_b[0].astype(BF16)
    w_o = w_out[0].astype(BF16)
    wr = jnp.zeros((D_MODEL, LANES), F32)
    wr = wr.at[:, :N_EXPERT_GROUPS].set(w_group_router[0]).at[:, N_EXPERT_GROUPS:N_EXPERT_GROUPS + N_EXPERTS].set(w_expert_router[0])
    wr_hi = wr.astype(BF16)
    wr_lo = (wr - wr_hi.astype(F32)).astype(BF16)
    br = jnp.zeros((1, LANES), F32)
    br = br.at[0, :N_EXPERT_GROUPS].set(b_group_router[0]).at[0, N_EXPERT_GROUPS:N_EXPERT_GROUPS + N_EXPERTS].set(b_expert_router[0])

    tn = 512
    tm_mp, tm_ms = _pick_tile(tp, 512), _pick_tile(tsmp, 512)
    merged_p = _merge_call(xp, g_attn, oa_p, ob_p, w_gates, w_ba, w_bb, tm_mp, tn)
    merged_s = _merge_call(xs_in, g_attn, oa_s, ob_s, w_gates, w_ba, w_bb, tm_ms, tn)
    tm_o = _pick_tile(tsmp, ROW_TILE)
    assert tp % tm_o == 0
    x1_all, route = _outproj_call(merged_p, xp, merged_s, xs_in, w_o, g_ffn, wr_hi, wr_lo, br, tm_o)

    tm_e = MOE_TILE
    n_tiles = (2 * t_all + N_EXPERTS * (tm_e - 1)) // tm_e + 1
    pos, tile_expert, tile_valid, n_used, pad_tile = _routing_offsets(route, tm_e, n_tiles)
    tm_d = _pick_tile(tsmp, ROW_TILE)
    pos3 = pos.reshape(t_all // tm_d, 1, 2 * tm_d)
    xs_sorted = _dispatch_call(pos3, pad_tile, x1_all, g_ffn, n_tiles * tm_e, tm_d, tm_e)
    ys = _moe_call(tile_expert, tile_valid, n_used, xs_sorted, w_expert_gate[0].astype(BF16), w_expert_up[0].astype(BF16),
                   w_expert_down[0].astype(BF16), tm_e)
    y_p = _combine_call("combine_p", pos3, x1_all, route, ys, tp, 0, tm_d)
    y_s = _combine_call("combine_s", pos3, x1_all, route, ys, tsmp, tp, tm_d)

    return (y_p.reshape(n, s, D_MODEL), y_s.reshape(ns, ts, D_MODEL),
            new_a_p, new_a_s, new_b_p[0], new_b_s[0], new_b_p[1], new_b_s[1], new_b_p[2], new_b_s[2])
```

```python
import functools

import jax
import jax.numpy as jnp
import numpy as np
from jax import lax
from jax.experimental import pallas as pl
from jax.experimental.pallas import tpu as pltpu

F32 = jnp.float32
BF16 = jnp.bfloat16

D_MODEL = 2048
HEAD_DIM = 64
ROT_DIM = HEAD_DIM // 4
ROPE_THETA = 500000.0
NORM_EPS = 1e-6
BLOCK = 128
PAST_LEN = 16384

A_HEADS = 16
A_KV_HEADS = 4
A_WINDOW = 128
B_PATTERNS = ((128, 1), (512, 4), (2048, 16))
B_GROUPS = 3
B_HPG = 8
A_Q_W = A_HEADS * HEAD_DIM
A_KV_W = A_KV_HEADS * HEAD_DIM
B_GW = B_HPG * HEAD_DIM
B_W = B_GROUPS * B_GW

N_EXPERT_GROUPS = 4
EXPERTS_PER_GROUP = 8
N_EXPERTS = N_EXPERT_GROUPS * EXPERTS_PER_GROUP
D_EXPERT = 512

LANES = 128
SUBLANES = 8
VMEM_LIMIT = 56 * 1024 * 1024
SAMPLE_T = 8

MOE_TILE = 256
ROW_TILE = 256


def _cparams(sem, vmem=None):
    return pltpu.CompilerParams(dimension_semantics=sem, vmem_limit_bytes=vmem)


def _rmsnorm(x, g):
    return x * lax.rsqrt(jnp.mean(x * x, axis=-1, keepdims=True) + NORM_EPS) * g


def _split2(v):
    hi = v.astype(BF16)
    lo = (v - hi.astype(F32)).astype(BF16)
    return hi, lo


def _split3(v):
    hi = v.astype(BF16)
    r = v - hi.astype(F32)
    mid = r.astype(BF16)
    lo = (r - mid.astype(F32)).astype(BF16)
    return hi, mid, lo


def _dot(a, b):
    return jnp.dot(a, b, preferred_element_type=F32)


def _dot_nt(a, b):
    return lax.dot_general(a, b, (((1,), (1,)), ((), ())), preferred_element_type=F32)


def _rope_tables(pos):
    half = ROT_DIM // 2
    inv_freq = ROPE_THETA ** (-jnp.arange(half, dtype=F32) / half)
    ang = pos.astype(F32)[:, None] * inv_freq[None, :]
    cos, sin = jnp.cos(ang), jnp.sin(ang)
    p = pos.shape[0]
    ones = jnp.ones((p, HEAD_DIM - ROT_DIM), F32)
    z8 = jnp.zeros((p, half), F32)
    z48 = jnp.zeros((p, HEAD_DIM - ROT_DIM), F32)
    c = jnp.concatenate([cos, cos, ones], axis=1)
    s_up = jnp.concatenate([-sin, z8, z48], axis=1)
    s_dn = jnp.concatenate([z8, sin, z48], axis=1)
    rep = LANES // HEAD_DIM
    return jnp.tile(c, (1, rep)), jnp.tile(s_up, (1, rep)), jnp.tile(s_dn, (1, rep))


def _proj_body(x_ref, g_ref, w_ref, gain_ref, c_ref, su_ref, sd_ref, bd_ref, *out_refs, kinds, widths):
    h = _rmsnorm(x_ref[...], g_ref[...]).astype(BF16)
    bd = bd_ref[...]
    col = 0
    for o_ref, width in zip(out_refs, widths):
        z_all = _dot(h, w_ref[:, col:col + width])
        for c in range(width // LANES):
            z = z_all[:, c * LANES:(c + 1) * LANES]
            kind = kinds[(col // LANES) + c]
            if kind >= 0:
                hi, lo = _split2(z * z)
                ss = _dot(hi, bd) + _dot(lo, bd)
                y = z * lax.rsqrt(ss * (1.0 / HEAD_DIM) + NORM_EPS) * gain_ref[kind:kind + 1, :]
                y = (y * c_ref[...] + pltpu.roll(y, LANES - ROT_DIM // 2, 1) * su_ref[...]
                     + pltpu.roll(y, ROT_DIM // 2, 1) * sd_ref[...])
                if kind % 2 == 0:
                    y = y * (HEAD_DIM ** -0.5)
                z = y
            o_ref[:, c * LANES:(c + 1) * LANES] = z
        col += width


def _proj_call(name, x2d, g, w, gains, tables, bd, kinds, widths, tm):
    t = x2d.shape[0]
    wtot = sum(widths)
    pos_blocks = tables[0].shape[0] // tm
    tab_spec = pl.BlockSpec((tm, LANES), lambda i: (i % pos_blocks, 0))
    return pl.pallas_call(
        functools.partial(_proj_body, kinds=kinds, widths=widths),
        name=name,
        grid=(t // tm,),
        in_specs=[
            pl.BlockSpec((tm, D_MODEL), lambda i: (i, 0)),
            pl.BlockSpec((1, D_MODEL), lambda i: (0, 0)),
            pl.BlockSpec((D_MODEL, wtot), lambda i: (0, 0)),
            pl.BlockSpec((4, LANES), lambda i: (0, 0)),
            tab_spec, tab_spec, tab_spec,
            pl.BlockSpec((LANES, LANES), lambda i: (0, 0)),
        ],
        out_specs=[pl.BlockSpec((tm, w_), lambda i: (i, 0)) for w_ in widths],
        out_shape=[jax.ShapeDtypeStruct((t, w_), F32) for w_ in widths],
        compiler_params=_cparams(("parallel",), VMEM_LIMIT),
    )(x2d, g, w, gains, *tables, bd)


def _band_mask(rows, nkeys, max_dist, has_prev):
    qi = lax.broadcasted_iota(jnp.int32, (rows, nkeys), 0) & (BLOCK - 1)
    kk = lax.broadcasted_iota(jnp.int32, (rows, nkeys), 1)
    if nkeys == BLOCK:
        dist = qi - kk
        return jnp.logical_and(dist >= 0, dist <= max_dist)
    dist = qi + BLOCK - kk
    band = jnp.logical_and(dist >= 0, dist <= max_dist)
    return jnp.logical_and(band, jnp.logical_or(kk >= BLOCK, has_prev))


def _masked_softmax(s, mask, sink):
    s = jnp.where(mask, s, -jnp.inf)
    m = jnp.max(s, axis=1, keepdims=True)
    if sink is not None:
        m = jnp.maximum(m, sink)
    p = jnp.exp(s - m)
    denom = jnp.sum(p, axis=1, keepdims=True)
    if sink is not None:
        denom = denom + jnp.exp(sink - m)
    return p, denom, m


def _attn_a_body(sink_ref, q_ref, kc_ref, kp_ref, o_ref):
    has_prev = pl.program_id(1) > 0
    heads_per_pair = 2 * (A_HEADS // A_KV_HEADS)
    rows = heads_per_pair * BLOCK
    mask = _band_mask(rows, 2 * BLOCK, A_WINDOW - 1, has_prev)
    low = lax.broadcasted_iota(jnp.int32, (BLOCK, LANES), 1) < HEAD_DIM
    outs = {}
    for c in range(A_KV_HEADS // 2):
        lanes = slice(c * LANES, (c + 1) * LANES)
        vlanes = slice(A_KV_W + c * LANES, A_KV_W + (c + 1) * LANES)
        k2 = jnp.concatenate([kp_ref[:, lanes], kc_ref[:, lanes]], axis=0).astype(BF16)
        v2 = jnp.concatenate([kp_ref[:, vlanes], kc_ref[:, vlanes]], axis=0).astype(BF16)
        blocks, sinks = [], []
        for jj in range(heads_per_pair):
            j = c * heads_per_pair + jj
            want_low = jj < heads_per_pair // 2
            q = q_ref[:, (j // 2) * LANES:(j // 2 + 1) * LANES]
            if (j % 2 == 0) != want_low:
                q = pltpu.roll(q, HEAD_DIM, 1)
            blocks.append(jnp.where(low if want_low else jnp.logical_not(low), q, 0.0).astype(BF16))
            sinks.append(jnp.full((BLOCK, 1), sink_ref[j], F32))
        p, denom, _ = _masked_softmax(_dot_nt(jnp.concatenate(blocks, axis=0), k2), mask, jnp.concatenate(sinks, axis=0))
        o = _dot(p.astype(BF16), v2) * (1.0 / denom)
        for jj in range(heads_per_pair):
            j = c * heads_per_pair + jj
            ob = o[jj * BLOCK:(jj + 1) * BLOCK]
            if (j % 2 == 0) != (jj < heads_per_pair // 2):
                ob = pltpu.roll(ob, HEAD_DIM, 1)
            outs[j] = ob
    for c in range(A_HEADS // 2):
        o_ref[:, c * LANES:(c + 1) * LANES] = jnp.where(low, outs[2 * c], outs[2 * c + 1]).astype(o_ref.dtype)


def _attn_a_call(qa, kva, sinks, n, s):
    nb = s // BLOCK
    return pl.pallas_call(
        _attn_a_body,
        name="attn_a",
        grid=(n, nb),
        in_specs=[
            pl.BlockSpec(memory_space=pltpu.SMEM),
            pl.BlockSpec((BLOCK, A_Q_W), lambda i, b: (i * nb + b, 0)),
            pl.BlockSpec((BLOCK, 2 * A_KV_W), lambda i, b: (i * nb + b, 0)),
            pl.BlockSpec((BLOCK, 2 * A_KV_W), lambda i, b: (i * nb + jnp.maximum(b - 1, 0), 0)),
        ],
        out_specs=pl.BlockSpec((BLOCK, A_Q_W), lambda i, b: (i * nb + b, 0)),
        out_shape=jax.ShapeDtypeStruct((n * s, A_Q_W), BF16),
        compiler_params=_cparams(("parallel", "arbitrary")),
    )(sinks, qa, kva, kva)


def _attn_b_unit(q_ref, k_ref, v_ref, o_scr, l_scr, g, start_c, start_p, has_prev, dil, use_prev):
    def rows(ref, start):
        if dil == 1:
            return ref[pl.ds(start, BLOCK), :]
        return ref[pl.ds(start, BLOCK, stride=dil), :]

    low = lax.broadcasted_iota(jnp.int32, (BLOCK, LANES), 1) < HEAD_DIM
    q = rows(q_ref, start_c)
    lhs = jnp.concatenate([jnp.where(low, q, 0.0), jnp.where(low, 0.0, q)], axis=0).astype(BF16)
    if use_prev:
        k2 = jnp.concatenate([rows(k_ref, start_p), rows(k_ref, start_c)], axis=0).astype(BF16)
        v2 = jnp.concatenate([rows(v_ref, start_p), rows(v_ref, start_c)], axis=0).astype(BF16)
    else:
        k2 = rows(k_ref, start_c).astype(BF16)
        v2 = rows(v_ref, start_c).astype(BF16)
    mask = _band_mask(2 * BLOCK, k2.shape[0], BLOCK, has_prev)
    p, denom, m = _masked_softmax(_dot_nt(lhs, k2), mask, None)
    o2 = _dot(p.astype(BF16), v2) * (1.0 / denom)
    l2 = jnp.broadcast_to(m + jnp.log(denom), (2 * BLOCK, LANES))
    o = jnp.where(low, o2[:BLOCK], o2[BLOCK:])
    lse = jnp.where(low, l2[:BLOCK], l2[BLOCK:])
    if dil == 1:
        o_scr[g, pl.ds(start_c, BLOCK), :] = o
        l_scr[g, pl.ds(start_c, BLOCK), :] = lse
    else:
        o_scr[g, pl.ds(start_c, BLOCK, stride=dil), :] = o
        l_scr[g, pl.ds(start_c, BLOCK, stride=dil), :] = lse


def _attn_b_body(q1, q2, q3, k1, v1, k2, v2, k3, v3, o_ref, o_scr, l_scr, *, s):
    qs, ks, vs = (q1, q2, q3), (k1, k2, k3), (v1, v2, v3)
    for g, (_, dil) in enumerate(B_PATTERNS):
        span = BLOCK * dil
        nblk = max(s // span, 1)
        use_prev = nblk > 1

        def unit(u, carry, g=g, dil=dil, span=span, use_prev=use_prev):
            r = u % dil
            b = u // dil
            start_c = b * span + r
            start_p = jnp.maximum(b - 1, 0) * span + r
            if dil == 1:
                start_c = pl.multiple_of(start_c, BLOCK)
                start_p = pl.multiple_of(start_p, BLOCK)
            _attn_b_unit(qs[g], ks[g], vs[g], o_scr, l_scr, g, start_c, start_p, b > 0, dil, use_prev)
            return carry

        lax.fori_loop(0, nblk * dil, unit, 0, unroll=2)

    chunk = 256

    def comb(c, carry):
        r0 = pl.multiple_of(c * chunk, chunk)
        l0, l1, l2 = (l_scr[g, pl.ds(r0, chunk), :] for g in range(B_GROUPS))
        m = jnp.maximum(jnp.maximum(l0, l1), l2)
        w0, w1, w2 = jnp.exp(l0 - m), jnp.exp(l1 - m), jnp.exp(l2 - m)
        inv = 1.0 / (w0 + w1 + w2)
        acc = (o_scr[0, pl.ds(r0, chunk), :] * (w0 * inv) + o_scr[1, pl.ds(r0, chunk), :] * (w1 * inv)
               + o_scr[2, pl.ds(r0, chunk), :] * (w2 * inv))
        o_ref[pl.ds(r0, chunk), :] = acc.astype(o_ref.dtype)
        return carry

    lax.fori_loop(0, s // chunk, comb, 0)


def _attn_b_call(qb, kvs, n, s):
    pairs = B_GW // LANES
    in_specs = [pl.BlockSpec((s, LANES), lambda i, h, g=g: (i, g * pairs + h)) for g in range(B_GROUPS)]
    args = [qb, qb, qb]
    for kv in kvs:
        in_specs.append(pl.BlockSpec((s, LANES), lambda i, h: (i, h)))
        in_specs.append(pl.BlockSpec((s, LANES), lambda i, h: (i, pairs + h)))
        args += [kv, kv]
    return pl.pallas_call(
        functools.partial(_attn_b_body, s=s),
        name="attn_b",
        grid=(n, pairs),
        in_specs=in_specs,
        out_specs=pl.BlockSpec((s, LANES), lambda i, h: (i, h)),
        out_shape=jax.ShapeDtypeStruct((n * s, B_GW), BF16),
        scratch_shapes=[pltpu.VMEM((B_GROUPS, s, LANES), F32), pltpu.VMEM((B_GROUPS, s, LANES), F32)],
        compiler_params=_cparams(("parallel", "arbitrary"), VMEM_LIMIT),
    )(*args)


def _shift_insert(old_ref, new_ref, out_ref, w):
    ncol = w // LANES
    keep = lax.broadcasted_iota(jnp.int32, (BLOCK, LANES), 1) < LANES - SAMPLE_T

    def strip(k, carry):
        r0 = pl.multiple_of(k * BLOCK, BLOCK)
        x = old_ref[pl.ds(r0, BLOCK), :]
        pad = jnp.concatenate([jnp.zeros((BLOCK - SAMPLE_T, BLOCK), F32), new_ref[:, pl.ds(r0, BLOCK)]], axis=0)
        nxt = pad.T
        for j in reversed(range(ncol)):
            cur = pltpu.roll(x[:, j * LANES:(j + 1) * LANES], LANES - SAMPLE_T, 1)
            out_ref[pl.ds(r0, BLOCK), j * LANES:(j + 1) * LANES] = jnp.where(keep, cur, nxt)
            nxt = cur
        return carry

    lax.fori_loop(0, old_ref.shape[0] // BLOCK, strip, 0)


def _window_attention(lhs, buf_ref, old_ref, krow, vrow, w, dil, min_old, sink):
    r = lhs.shape[0]
    t_main = lax.broadcasted_iota(jnp.int32, (r, w), 0) & (SAMPLE_T - 1)
    delta = (w - SAMPLE_T) + t_main - lax.broadcasted_iota(jnp.int32, (r, w), 1)
    valid_main = jnp.logical_and(delta >= 0, (delta & (dil - 1)) == 0)
    t_old = lax.broadcasted_iota(jnp.int32, (r, LANES), 0) & (SAMPLE_T - 1)
    c_old = lax.broadcasted_iota(jnp.int32, (r, LANES), 1)
    valid_old = jnp.logical_and(jnp.logical_and(c_old < SAMPLE_T, c_old >= t_old + min_old),
                                ((w + t_old - c_old) & (dil - 1)) == 0)
    kt = buf_ref[krow:krow + LANES, :].astype(BF16)
    vt = buf_ref[vrow:vrow + LANES, :].astype(BF16)
    kx = old_ref[krow:krow + LANES, 0:LANES].astype(BF16)
    vx = old_ref[vrow:vrow + LANES, 0:LANES].astype(BF16)
    s = jnp.where(valid_main, _dot(lhs, kt), -jnp.inf)
    sx = jnp.where(valid_old, _dot(lhs, kx), -jnp.inf)
    m = jnp.maximum(jnp.max(s, axis=1, keepdims=True), jnp.max(sx, axis=1, keepdims=True))
    if sink is not None:
        m = jnp.maximum(m, sink)
    p = jnp.exp(s - m)
    px = jnp.exp(sx - m)
    denom = jnp.sum(p, axis=1, keepdims=True) + jnp.sum(px, axis=1, keepdims=True)
    if sink is not None:
        denom = denom + jnp.exp(sink - m)
    inv = 1.0 / denom
    o = _dot_nt((p * inv).astype(BF16), vt) + _dot_nt((px * inv).astype(BF16), vx)
    return o, m + jnp.log(denom)


def _sample_body(sink_ref, qa_ref, kva_ref, qb_ref, kvb1_ref, kvb2_ref, kvb3_ref, ca_ref, cb1_ref, cb2_ref, cb3_ref,
                 oa_ref, ob_ref, na_ref, nb1_ref, nb2_ref, nb3_ref):
    low = lax.broadcasted_iota(jnp.int32, (SAMPLE_T, LANES), 1) < HEAD_DIM

    _shift_insert(ca_ref, kva_ref, na_ref, A_WINDOW)
    qa = qa_ref[...]
    heads_per_pair = 2 * (A_HEADS // A_KV_HEADS)
    outs = {}
    for c in range(A_KV_HEADS // 2):
        blocks, sinks = [], []
        for jj in range(heads_per_pair):
            j = c * heads_per_pair + jj
            want_low = jj < heads_per_pair // 2
            q = qa[:, (j // 2) * LANES:(j // 2 + 1) * LANES]
            if (j % 2 == 0) != want_low:
                q = pltpu.roll(q, HEAD_DIM, 1)
            blocks.append(jnp.where(low if want_low else jnp.logical_not(low), q, 0.0))
            sinks.append(jnp.full((SAMPLE_T, 1), sink_ref[j], F32))
        o, _ = _window_attention(jnp.concatenate(blocks, axis=0).astype(BF16), na_ref, ca_ref, c * LANES,
                                 A_KV_W + c * LANES, A_WINDOW, 1, 1, jnp.concatenate(sinks, axis=0))
        for jj in range(heads_per_pair):
            j = c * heads_per_pair + jj
            ob = o[jj * SAMPLE_T:(jj + 1) * SAMPLE_T]
            if (j % 2 == 0) != (jj < heads_per_pair // 2):
                ob = pltpu.roll(ob, HEAD_DIM, 1)
            outs[j] = ob
    for c in range(A_HEADS // 2):
        oa_ref[:, c * LANES:(c + 1) * LANES] = jnp.where(low, outs[2 * c], outs[2 * c + 1])

    qb = qb_ref[...]
    new_refs = (kvb1_ref, kvb2_ref, kvb3_ref)
    old_refs = (cb1_ref, cb2_ref, cb3_ref)
    buf_refs = (nb1_ref, nb2_ref, nb3_ref)
    pairs = B_GW // LANES
    o_g, l_g = [], []
    for g, (win, dil) in enumerate(B_PATTERNS):
        _shift_insert(old_refs[g], new_refs[g], buf_refs[g], win)
        o_cols, l_cols = [], []
        for c in range(pairs):
            q = qb[:, g * B_GW + c * LANES:g * B_GW + (c + 1) * LANES]
            lhs = jnp.concatenate([jnp.where(low, q, 0.0), jnp.where(low, 0.0, q)], axis=0).astype(BF16)
            o, lse = _window_attention(lhs, buf_refs[g], old_refs[g], c * LANES, B_GW + c * LANES, win, dil, 0, None)
            lse = jnp.broadcast_to(lse, (2 * SAMPLE_T, LANES))
            o_cols.append(jnp.where(low, o[:SAMPLE_T], o[SAMPLE_T:]))
            l_cols.append(jnp.where(low, lse[:SAMPLE_T], lse[SAMPLE_T:]))
        o_g.append(jnp.concatenate(o_cols, axis=1))
        l_g.append(jnp.concatenate(l_cols, axis=1))
    m = jnp.maximum(jnp.maximum(l_g[0], l_g[1]), l_g[2])
    w0, w1, w2 = jnp.exp(l_g[0] - m), jnp.exp(l_g[1] - m), jnp.exp(l_g[2] - m)
    inv = 1.0 / (w0 + w1 + w2)
    ob_ref[...] = o_g[0] * (w0 * inv) + o_g[1] * (w1 * inv) + o_g[2] * (w2 * inv)


def _sample_call(sinks, qa, kva, qb, kvbs, caches, ns):
    def rows(width):
        return pl.BlockSpec((SAMPLE_T, width), lambda i: (i, 0))

    def cache(c):
        return pl.BlockSpec((None,) + c.shape[1:], lambda i: (i, 0, 0))

    news = [qa, kva, qb, *kvbs]
    in_specs = [pl.BlockSpec(memory_space=pltpu.SMEM)] + [rows(a.shape[1]) for a in news] + [cache(c) for c in caches]
    out_shape = ([jax.ShapeDtypeStruct((ns * SAMPLE_T, A_Q_W), F32), jax.ShapeDtypeStruct((ns * SAMPLE_T, B_GW), F32)]
                 + [jax.ShapeDtypeStruct(c.shape, F32) for c in caches])
    out_specs = [rows(A_Q_W), rows(B_GW)] + [cache(c) for c in caches]
    return pl.pallas_call(
        _sample_body,
        name="sample_attn",
        grid=(ns,),
        in_specs=in_specs,
        out_specs=out_specs,
        out_shape=out_shape,
        compiler_params=_cparams(("parallel",), VMEM_LIMIT),
    )(sinks, *news, *caches)


def _tail_transpose_body(x_ref, o_ref):
    o_ref[...] = x_ref[...].T


def _tail_transpose_call(name, kv, n, s, win):
    c = kv.shape[1]
    tt = min(win, 2 * LANES)
    per_seq, first = s // tt, (s - win) // tt
    return pl.pallas_call(
        _tail_transpose_body,
        name=name,
        grid=(n, win // tt),
        in_specs=[pl.BlockSpec((tt, c), lambda i, j: (i * per_seq + first + j, 0))],
        out_specs=pl.BlockSpec((None, c, tt), lambda i, j: (i, 0, j)),
        out_shape=jax.ShapeDtypeStruct((n, c, win), F32),
        compiler_params=_cparams(("parallel", "parallel")),
    )(kv)


def _merge_body(x_ref, g_ref, oa_ref, ob_ref, wga_ref, wgb_ref, wba_ref, wbb_ref, o_ref, h_scr):
    @pl.when(pl.program_id(1) == 0)
    def _():
        h_scr[...] = _rmsnorm(x_ref[...], g_ref[...]).astype(BF16)

    h = h_scr[...]
    ga = _dot(h, wga_ref[...])
    gb = _dot(h, wgb_ref[...])
    ya = _dot(oa_ref[...].astype(BF16), wba_ref[...])
    yb = _dot(ob_ref[...].astype(BF16), wbb_ref[...])
    o_ref[...] = (jax.nn.sigmoid(ga) * ya + jax.nn.sigmoid(gb) * yb).astype(o_ref.dtype)


def _merge_call(x2d, g, oa, ob, w_gates, w_ba, w_bb, tm, tn):
    t = x2d.shape[0]
    ncol = D_MODEL // tn
    return pl.pallas_call(
        _merge_body,
        name="merge",
        grid=(t // tm, ncol),
        in_specs=[
            pl.BlockSpec((tm, D_MODEL), lambda i, j: (i, 0)),
            pl.BlockSpec((1, D_MODEL), lambda i, j: (0, 0)),
            pl.BlockSpec((tm, A_Q_W), lambda i, j: (i, 0)),
            pl.BlockSpec((tm, B_GW), lambda i, j: (i, 0)),
            pl.BlockSpec((D_MODEL, tn), lambda i, j: (0, j)),
            pl.BlockSpec((D_MODEL, tn), lambda i, j: (0, ncol + j)),
            pl.BlockSpec((A_Q_W, tn), lambda i, j: (0, j)),
            pl.BlockSpec((B_GW, tn), lambda i, j: (0, j)),
        ],
        out_specs=pl.BlockSpec((tm, tn), lambda i, j: (i, j)),
        out_shape=jax.ShapeDtypeStruct((t, D_MODEL), BF16),
        scratch_shapes=[pltpu.VMEM((tm, D_MODEL), BF16)],
        compiler_params=_cparams(("parallel", "arbitrary"), VMEM_LIMIT),
    )(x2d, g, oa, ob, w_gates, w_gates, w_ba, w_bb)


def _route(logits):
    tm = logits.shape[0]
    lane = lax.broadcasted_iota(jnp.int32, (tm, LANES), 1)
    neg = -jnp.inf
    big = LANES

    def first_where(cond):
        return jnp.min(jnp.where(cond, lane, big), axis=1, keepdims=True)

    gl = jnp.where(lane < N_EXPERT_GROUPS, logits, neg)
    gmax = jnp.max(gl, axis=1, keepdims=True)
    gidx = first_where(gl == gmax)
    g_w = 1.0 / jnp.sum(jnp.exp(gl - gmax), axis=1, keepdims=True)
    lo = N_EXPERT_GROUPS + gidx * EXPERTS_PER_GROUP
    in_grp = jnp.logical_and(lane >= lo, lane < lo + EXPERTS_PER_GROUP)
    el = jnp.where(in_grp, logits, neg)
    ep = jnp.exp(el - jnp.max(el, axis=1, keepdims=True))
    prob = ep / jnp.sum(ep, axis=1, keepdims=True)
    prob = jnp.where(in_grp, prob, -1.0)
    p1 = jnp.max(prob, axis=1, keepdims=True)
    i1 = first_where(prob == p1)
    prob2 = jnp.where(lane == i1, -1.0, prob)
    p2 = jnp.max(prob2, axis=1, keepdims=True)
    i2 = first_where(prob2 == p2)
    tot = p1 + p2
    c1 = g_w * (p1 / tot)
    c2 = g_w * (p2 / tot)
    e1 = (i1 - N_EXPERT_GROUPS).astype(F32)
    e2 = (i2 - N_EXPERT_GROUPS).astype(F32)
    return e1, e2, c1, c2


def _outproj_body(mp_ref, xp_ref, ms_ref, xs_ref, wo_ref, g_ref, wrh_ref, wrl_ref, br_ref, x1_ref, route_ref, *, n_prompt):
    def compute(m_ref, x_ref):
        x1 = x_ref[...] + _dot(m_ref[...], wo_ref[...])
        x1_ref[...] = x1
        hi, lo = _split2(_rmsnorm(x1, g_ref[...]))
        logits = _dot(hi, wrh_ref[...]) + _dot(hi, wrl_ref[...]) + _dot(lo, wrh_ref[...]) + br_ref[...]
        e1, e2, c1, c2 = _route(logits)
        lane = lax.broadcasted_iota(jnp.int32, logits.shape, 1)
        route_ref[...] = jnp.where(lane == 0, e1, jnp.where(lane == 1, e2, jnp.where(lane == 2, c1, jnp.where(lane == 3, c2, 0.0))))

    i = pl.program_id(0)
    pl.when(i < n_prompt)(lambda: compute(mp_ref, xp_ref))
    pl.when(i >= n_prompt)(lambda: compute(ms_ref, xs_ref))


def _outproj_call(merged_p, xp, merged_s, xs, w_o, g, wr_hi, wr_lo, br, tm):
    n_p, n_s = xp.shape[0] // tm, xs.shape[0] // tm
    t_all = xp.shape[0] + xs.shape[0]

    def p_map(i):
        return (jnp.minimum(i, n_p - 1), 0)

    def s_map(i):
        return (jnp.maximum(i - n_p, 0), 0)

    def const(shape):
        return pl.BlockSpec(shape, lambda i: (0, 0))

    return pl.pallas_call(
        functools.partial(_outproj_body, n_prompt=n_p),
        name="outproj",
        grid=(n_p + n_s,),
        in_specs=[
            pl.BlockSpec((tm, D_MODEL), p_map), pl.BlockSpec((tm, D_MODEL), p_map),
            pl.BlockSpec((tm, D_MODEL), s_map), pl.BlockSpec((tm, D_MODEL), s_map),
            const((D_MODEL, D_MODEL)), const((1, D_MODEL)), const((D_MODEL, LANES)), const((D_MODEL, LANES)),
            const((1, LANES)),
        ],
        out_specs=[pl.BlockSpec((tm, D_MODEL), lambda i: (i, 0)), pl.BlockSpec((tm, LANES), lambda i: (i, 0))],
        out_shape=[jax.ShapeDtypeStruct((t_all, D_MODEL), F32), jax.ShapeDtypeStruct((t_all, LANES), F32)],
        compiler_params=_cparams(("arbitrary",), VMEM_LIMIT),
    )(merged_p, xp, merged_s, xs, w_o, g, wr_hi, wr_lo, br)


def _row_copy(src, src_row, dst, dst_row, sem):
    return pltpu.make_async_copy(src.at[pl.ds(src_row, 1), :], dst.at[pl.ds(dst_row, 1), :], sem)


def _dispatch_body(pos_ref, pad_tile_ref, x1_ref, g_ref, xs_hbm, h_scr, sem, *, tile):
    tm = h_scr.shape[0]

    @pl.when(pl.program_id(0) == 0)
    def _():
        h_scr[...] = jnp.zeros_like(h_scr)

        def pad_copy(t, k):
            row = pl.multiple_of(t * tile + k * tm, tm)
            return pltpu.make_async_copy(h_scr, xs_hbm.at[pl.ds(row, tm), :], sem)

        n_used = pad_tile_ref[N_EXPERTS]
        for phase in ("start", "wait"):
            def expert_pad(e, carry, phase=phase):
                @pl.when(pad_tile_ref[e] >= 0)
                def _():
                    for k in range(tile // tm):
                        getattr(pad_copy(pad_tile_ref[e], k), phase)()
                return carry
            lax.fori_loop(0, N_EXPERTS, expert_pad, 0)

            def unused_tile(t, carry, phase=phase):
                @pl.when(t >= n_used)
                def _():
                    for k in range(tile // tm):
                        getattr(pad_copy(t, k), phase)()
                return carry
            lax.fori_loop(0, xs_hbm.shape[0] // tile, unused_tile, 0)

    h_scr[...] = _rmsnorm(x1_ref[...], g_ref[...])

    def issue(j, carry):
        _row_copy(h_scr, j, xs_hbm, pos_ref[0, 0, 2 * j], sem).start()
        _row_copy(h_scr, j, xs_hbm, pos_ref[0, 0, 2 * j + 1], sem).start()
        return carry

    lax.fori_loop(0, tm, issue, 0)

    def drain(j, carry):
        _row_copy(h_scr, j, xs_hbm, pos_ref[0, 0, 2 * j], sem).wait()
        _row_copy(h_scr, j, xs_hbm, pos_ref[0, 0, 2 * j + 1], sem).wait()
        return carry

    lax.fori_loop(0, tm, drain, 0)


def _dispatch_call(pos3, pad_tile, x1, g, rows, tm, tile):
    t = x1.shape[0]
    assert tile % tm == 0
    return pl.pallas_call(
        functools.partial(_dispatch_body, tile=tile),
        name="dispatch",
        grid=(t // tm,),
        in_specs=[
            pl.BlockSpec((1, 1, 2 * tm), lambda i: (i, 0, 0), memory_space=pltpu.SMEM),
            pl.BlockSpec(memory_space=pltpu.SMEM),
            pl.BlockSpec((tm, D_MODEL), lambda i: (i, 0)),
            pl.BlockSpec((1, D_MODEL), lambda i: (0, 0)),
        ],
        out_specs=pl.BlockSpec(memory_space=pl.ANY),
        out_shape=jax.ShapeDtypeStruct((rows, D_MODEL), F32),
        scratch_shapes=[pltpu.VMEM((tm, D_MODEL), F32), pltpu.SemaphoreType.DMA(())],
        compiler_params=_cparams(("arbitrary",)),
    )(pos3, pad_tile, x1, g)


def _moe_body(te_ref, tv_ref, nu_ref, xs_ref, wg_ref, wu_ref, wd_ref, ys_ref):
    i = pl.program_id(0)

    @pl.when(i < nu_ref[0])
    def _():
        row = lax.broadcasted_iota(jnp.int32, xs_ref.shape, 0)
        x = jnp.where(row < tv_ref[i], xs_ref[...], 0.0).astype(BF16)
        a = jax.nn.silu(_dot(x, wg_ref[0])) * _dot(x, wu_ref[0])
        ys_ref[...] = _dot(a.astype(BF16), wd_ref[0])

    @pl.when(i >= nu_ref[0])
    def _():
        ys_ref[...] = jnp.zeros_like(ys_ref)


def _moe_call(tile_expert, tile_valid, n_used, xs, w_gate, w_up, w_down, tm):
    rows = xs.shape[0]

    def row_map(i, te, tv, nu):
        return (jnp.minimum(i, nu[0] - 1), 0)

    def out_map(i, te, tv, nu):
        return (i, 0)

    def w_map(i, te, tv, nu):
        return (te[i], 0, 0)

    return pl.pallas_call(
        _moe_body,
        name="moe",
        grid_spec=pltpu.PrefetchScalarGridSpec(
            num_scalar_prefetch=3,
            grid=(rows // tm,),
            in_specs=[
                pl.BlockSpec((tm, D_MODEL), row_map),
                pl.BlockSpec((1, D_MODEL, D_EXPERT), w_map),
                pl.BlockSpec((1, D_MODEL, D_EXPERT), w_map),
                pl.BlockSpec((1, D_EXPERT, D_MODEL), w_map),
            ],
            out_specs=pl.BlockSpec((tm, D_MODEL), out_map),
        ),
        out_shape=jax.ShapeDtypeStruct((rows, D_MODEL), F32),
        compiler_params=_cparams(("arbitrary",), VMEM_LIMIT),
    )(tile_expert, tile_valid, n_used, xs, w_gate, w_up, w_down)


def _combine_body(pos_ref, x1_ref, route_ref, ys_hbm, y_ref, buf0, buf1, sem):
    tm = buf0.shape[0]

    def issue(j, carry):
        _row_copy(ys_hbm, pos_ref[0, 0, 2 * j], buf0, j, sem).start()
        _row_copy(ys_hbm, pos_ref[0, 0, 2 * j + 1], buf1, j, sem).start()
        return carry

    lax.fori_loop(0, tm, issue, 0)

    def drain(j, carry):
        _row_copy(ys_hbm, pos_ref[0, 0, 2 * j], buf0, j, sem).wait()
        _row_copy(ys_hbm, pos_ref[0, 0, 2 * j + 1], buf1, j, sem).wait()
        return carry

    lax.fori_loop(0, tm, drain, 0)
    route = route_ref[...]
    y_ref[...] = x1_ref[...] + (route[:, 2:3] * buf0[...] + route[:, 3:4] * buf1[...])


def _combine_call(name, pos3, x1_all, route, ys, t, row0, tm):
    off = row0 // tm
    return pl.pallas_call(
        _combine_body,
        name=name,
        grid=(t // tm,),
        in_specs=[
            pl.BlockSpec((1, 1, 2 * tm), lambda i: (i + off, 0, 0), memory_space=pltpu.SMEM),
            pl.BlockSpec((tm, D_MODEL), lambda i: (i + off, 0)),
            pl.BlockSpec((tm, LANES), lambda i: (i + off, 0)),
            pl.BlockSpec(memory_space=pl.ANY),
        ],
        out_specs=pl.BlockSpec((tm, D_MODEL), lambda i: (i, 0)),
        out_shape=jax.ShapeDtypeStruct((t, D_MODEL), F32),
        scratch_shapes=[pltpu.VMEM((tm, D_MODEL), F32), pltpu.VMEM((tm, D_MODEL), F32), pltpu.SemaphoreType.DMA(())],
        compiler_params=_cparams(("arbitrary",)),
    )(pos3, x1_all, route, ys)


def _routing_offsets(route, tm, n_tiles):
    e = route[:, :2].astype(jnp.int32)
    onehot = jnp.sum((e[:, :, None] == jnp.arange(N_EXPERTS, dtype=jnp.int32)[None, None, :]).astype(jnp.int32), axis=1)
    csum = jnp.cumsum(onehot, axis=0)
    rank = csum - onehot
    counts = csum[-1]
    tiles_e = (counts + tm - 1) // tm
    tile_end = jnp.cumsum(tiles_e)
    tile_start = tile_end - tiles_e
    pos = tile_start[e] * tm + jnp.take_along_axis(rank, e, axis=1)
    n_used = tile_end[-1]
    tile_id = jnp.minimum(jnp.arange(n_tiles, dtype=jnp.int32), n_used - 1)
    tile_expert = jnp.sum((tile_id[:, None] >= tile_end[None, :]).astype(jnp.int32), axis=1)
    tile_valid = jnp.clip(counts[tile_expert] - (tile_id - tile_start[tile_expert]) * tm, 0, tm)
    pad_tile = jnp.concatenate([jnp.where(tiles_e > 0, tile_end - 1, -1), n_used.reshape(1)])
    i32 = jnp.int32
    return (pos.astype(i32), tile_expert.astype(i32), tile_valid.astype(i32), n_used.reshape(1).astype(i32),
            pad_tile.astype(i32))


def _pick_tile(t, pref):
    tm = min(pref, t)
    assert t % tm == 0, (t, tm)
    return tm


def kernel(x_prompt, x_sample, cache_a_kv, cache_b1_kv, cache_b2_kv, cache_b3_kv, g_attn_norm, w_in, q_norm_a, k_norm_a, q_norm_b, k_norm_b, sinks_a, w_branch_a, w_branch_b, w_out, g_ffn_norm, w_group_router, b_group_router, w_expert_router, b_expert_router, w_expert_gate, w_expert_up, w_expert_down):
    n, s, _ = x_prompt.shape
    ns, ts, _ = x_sample.shape
    assert ts == SAMPLE_T and s == B_PATTERNS[-1][0] and x_prompt.shape[2] == D_MODEL
    assert g_attn_norm.shape[0] == 1, "single layer"
    tp, tsmp = n * s, ns * ts
    t_all = tp + tsmp
    xp = x_prompt.reshape(tp, D_MODEL)
    xs_in = x_sample.reshape(tsmp, D_MODEL)

    w = w_in[0]
    o_qa, o_ka, o_va = 0, A_Q_W, A_Q_W + A_KV_W
    o_qb = o_va + A_KV_W
    o_kb, o_vb = o_qb + B_W, o_qb + 2 * B_W
    o_g = o_vb + B_W
    cols_a = [w[:, o_qa:o_ka], w[:, o_ka:o_va], w[:, o_va:o_qb]]
    cols_b = []
    for g in range(B_GROUPS):
        cols_b += [w[:, o_kb + g * B_GW:o_kb + (g + 1) * B_GW], w[:, o_vb + g * B_GW:o_vb + (g + 1) * B_GW]]
    w_pa = jnp.concatenate(cols_a, axis=1).astype(BF16)
    w_pq = w[:, o_qb:o_kb].astype(BF16)
    w_pkv = jnp.concatenate(cols_b, axis=1).astype(BF16)
    w_gates = w[:, o_g:].astype(BF16)
    rep = LANES // HEAD_DIM
    gains = jnp.stack([jnp.tile(v[0], rep) for v in (q_norm_a, k_norm_a, q_norm_b, k_norm_b)]).astype(F32)
    bd = jnp.asarray((np.arange(LANES)[:, None] // HEAD_DIM == np.arange(LANES)[None, :] // HEAD_DIM).astype(np.float32), BF16)
    g_attn = g_attn_norm.astype(F32)
    g_ffn = g_ffn_norm.astype(F32)
    nq, nk = A_Q_W // LANES, A_KV_W // LANES
    kinds_a = (0,) * nq + (1,) * nk + (-1,) * nk
    kinds_q = (2,) * (B_W // LANES)
    kinds_kv = ((3,) * (B_GW // LANES) + (-1,) * (B_GW // LANES)) * B_GROUPS

    def project(x2d, pos, tm):
        tables = _rope_tables(pos)
        qa, kva = _proj_call("proj_a", x2d, g_attn, w_pa, gains, tables, bd, kinds_a, (A_Q_W, 2 * A_KV_W), tm)
        (qb,) = _proj_call("proj_qb", x2d, g_attn, w_pq, gains, tables, bd, kinds_q, (B_W,), tm)
        kvb = _proj_call("proj_kvb", x2d, g_attn, w_pkv, gains, tables, bd, kinds_kv, (2 * B_GW,) * B_GROUPS, tm)
        return qa, kva, qb, kvb

    tm_p = _pick_tile(tp, ROW_TILE)
    qa, kva, qb, kvb = project(xp, jnp.arange(s, dtype=jnp.int32), tm_p)
    sinks = sinks_a[0].astype(F32)
    oa_p = _attn_a_call(qa, kva, sinks, n, s)
    ob_p = _attn_b_call(qb, kvb, n, s)

    def to_feature_major(c, heads):
        return jnp.transpose(c, (0, 2, 3, 4, 1)).reshape(c.shape[0], 2 * heads * HEAD_DIM, c.shape[1])

    def from_feature_major(c, heads):
        return jnp.transpose(c.reshape(c.shape[0], 2, heads, HEAD_DIM, c.shape[2]), (0, 4, 1, 2, 3))[None]

    new_a_p = from_feature_major(_tail_transpose_call("tail_a", kva, n, s, min(A_WINDOW, s)), A_KV_HEADS)
    new_b_p = [from_feature_major(_tail_transpose_call("tail_b%d" % g, kvb[g], n, s, min(win, s)), B_HPG)
               for g, (win, _) in enumerate(B_PATTERNS)]

    tm_s = _pick_tile(tsmp, ROW_TILE)
    pos_s = PAST_LEN + (jnp.arange(tm_s, dtype=jnp.int32) % ts)
    qa_s, kva_s, qb_s, kvb_s = project(xs_in, pos_s, tm_s)
    caches = [to_feature_major(cache_a_kv[0], A_KV_HEADS)]
    caches += [to_feature_major(c[0], B_HPG) for c in (cache_b1_kv, cache_b2_kv, cache_b3_kv)]
    oa_s, ob_s, na, nb1, nb2, nb3 = _sample_call(sinks, qa_s, kva_s, qb_s, kvb_s, caches, ns)
    new_a_s = from_feature_major(na, A_KV_HEADS)
    new_b_s = [from_feature_major(c, B_HPG) for c in (nb1, nb2, nb3)]

    w_ba = w_branch_a[0].astype(BF16)
    w_bb = w_branch_b[0].astype(BF16)
    w_o = w_out[0].astype(BF16)
    wr = jnp.zeros((D_MODEL, LANES), F32)
    wr = wr.at[:, :N_EXPERT_GROUPS].set(w_group_router[0]).at[:, N_EXPERT_GROUPS:N_EXPERT_GROUPS + N_EXPERTS].set(w_expert_router[0])
    wr_hi = wr.astype(BF16)
    wr_lo = (wr - wr_hi.astype(F32)).astype(BF16)
    br = jnp.zeros((1, LANES), F32)
    br = br.at[0, :N_EXPERT_GROUPS].set(b_group_router[0]).at[0, N_EXPERT_GROUPS:N_EXPERT_GROUPS + N_EXPERTS].set(b_expert_router[0])

    tn = 512
    tm_mp, tm_ms = _pick_tile(tp, 512), _pick_tile(tsmp, 512)
    merged_p = _merge_call(xp, g_attn, oa_p, ob_p, w_gates, w_ba, w_bb, tm_mp, tn)
    merged_s = _merge_call(xs_in, g_attn, oa_s, ob_s, w_gates, w_ba, w_bb, tm_ms, tn)
    tm_o = _pick_tile(tsmp, ROW_TILE)
    assert tp % tm_o == 0
    x1_all, route = _outproj_call(merged_p, xp, merged_s, xs_in, w_o, g_ffn, wr_hi, wr_lo, br, tm_o)

    tm_e = MOE_TILE
    n_tiles = (2 * t_all + N_EXPERTS * (tm_e - 1)) // tm_e + 1
    pos, tile_expert, tile_valid, n_used, pad_tile = _routing_offsets(route, tm_e, n_tiles)
    tm_d = _pick_tile(tsmp, ROW_TILE)
    pos3 = pos.reshape(t_all // tm_d, 1, 2 * tm_d)
    xs_sorted = _dispatch_call(pos3, pad_tile, x1_all, g_ffn, n_tiles * tm_e, tm_d, tm_e)
    ys = _moe_call(tile_expert, tile_valid, n_used, xs_sorted, w_expert_gate[0].astype(BF16), w_expert_up[0].astype(BF16),
                   w_expert_down[0].astype(BF16), tm_e)
    y_p = _combine_call("combine_p", pos3, x1_all, route, ys, tp, 0, tm_d)
    y_s = _combine_call("combine_s", pos3, x1_all, route, ys, tsmp, tp, tm_d)

    return (y_p.reshape(n, s, D_MODEL), y_s.reshape(ns, ts, D_MODEL),
            new_a_p, new_a_s, new_b_p[0], new_b_s[0], new_b_p[1], new_b_s[1], new_b_p[2], new_b_s[2])
```

```python
import functools

import jax
import jax.numpy as jnp
import numpy as np
from jax import lax
from jax.experimental import pallas as pl
from jax.experimental.pallas import tpu as pltpu

F32 = jnp.float32
BF16 = jnp.bfloat16

D_MODEL = 2048
HEAD_DIM = 64
ROT_DIM = HEAD_DIM // 4
ROPE_THETA = 500000.0
NORM_EPS = 1e-6
BLOCK = 128
PAST_LEN = 16384

A_HEADS = 16
A_KV_HEADS = 4
A_WINDOW = 128
B_PATTERNS = ((128, 1), (512, 4), (2048, 16))
B_GROUPS = 3
B_HPG = 8
A_Q_W = A_HEADS * HEAD_DIM
A_KV_W = A_KV_HEADS * HEAD_DIM
B_GW = B_HPG * HEAD_DIM
B_W = B_GROUPS * B_GW

N_EXPERT_GROUPS = 4
EXPERTS_PER_GROUP = 8
N_EXPERTS = N_EXPERT_GROUPS * EXPERTS_PER_GROUP
D_EXPERT = 512

LANES = 128
SUBLANES = 8
VMEM_LIMIT = 56 * 1024 * 1024
SAMPLE_T = 8

MOE_TILE = 512
ROW_TILE = 256


def _cparams(sem, vmem=None):
    return pltpu.CompilerParams(dimension_semantics=sem, vmem_limit_bytes=vmem)


def _rmsnorm(x, g):
    return x * lax.rsqrt(jnp.mean(x * x, axis=-1, keepdims=True) + NORM_EPS) * g


def _split2(v):
    hi = v.astype(BF16)
    lo = (v - hi.astype(F32)).astype(BF16)
    return hi, lo


def _split3(v):
    hi = v.astype(BF16)
    r = v - hi.astype(F32)
    mid = r.astype(BF16)
    lo = (r - mid.astype(F32)).astype(BF16)
    return hi, mid, lo


def _dot(a, b):
    return jnp.dot(a, b, preferred_element_type=F32)


def _multiple_of(x, m):
    return x if isinstance(x, int) else pl.multiple_of(x, m)


def _dot_nt(a, b):
    return lax.dot_general(a, b, (((1,), (1,)), ((), ())), preferred_element_type=F32)


def _rope_tables(pos):
    half = ROT_DIM // 2
    inv_freq = ROPE_THETA ** (-jnp.arange(half, dtype=F32) / half)
    ang = pos.astype(F32)[:, None] * inv_freq[None, :]
    cos, sin = jnp.cos(ang), jnp.sin(ang)
    p = pos.shape[0]
    ones = jnp.ones((p, HEAD_DIM - ROT_DIM), F32)
    z8 = jnp.zeros((p, half), F32)
    z48 = jnp.zeros((p, HEAD_DIM - ROT_DIM), F32)
    c = jnp.concatenate([cos, cos, ones], axis=1)
    s_up = jnp.concatenate([-sin, z8, z48], axis=1)
    s_dn = jnp.concatenate([z8, sin, z48], axis=1)
    rep = LANES // HEAD_DIM
    return jnp.tile(c, (1, rep)), jnp.tile(s_up, (1, rep)), jnp.tile(s_dn, (1, rep))


def _proj_body(x_ref, g_ref, w_ref, gain_ref, c_ref, su_ref, sd_ref, bd_ref, *out_refs, kinds, widths):
    h = _rmsnorm(x_ref[...], g_ref[...]).astype(BF16)
    bd = bd_ref[...]
    col = 0
    for o_ref, width in zip(out_refs, widths):
        z_all = _dot(h, w_ref[:, col:col + width])
        for c in range(width // LANES):
            z = z_all[:, c * LANES:(c + 1) * LANES]
            kind = kinds[(col // LANES) + c]
            if kind >= 0:
                hi, lo = _split2(z * z)
                ss = _dot(hi, bd) + _dot(lo, bd)
                y = z * lax.rsqrt(ss * (1.0 / HEAD_DIM) + NORM_EPS) * gain_ref[kind:kind + 1, :]
                y = (y * c_ref[...] + pltpu.roll(y, LANES - ROT_DIM // 2, 1) * su_ref[...]
                     + pltpu.roll(y, ROT_DIM // 2, 1) * sd_ref[...])
                if kind % 2 == 0:
                    y = y * (HEAD_DIM ** -0.5)
                z = y
            o_ref[:, c * LANES:(c + 1) * LANES] = z
        col += width


def _proj_call(name, x2d, g, w, gains, tables, bd, kinds, widths, tm):
    t = x2d.shape[0]
    wtot = sum(widths)
    pos_blocks = tables[0].shape[0] // tm
    tab_spec = pl.BlockSpec((tm, LANES), lambda i: (i % pos_blocks, 0))
    return pl.pallas_call(
        functools.partial(_proj_body, kinds=kinds, widths=widths),
        name=name,
        grid=(t // tm,),
        in_specs=[
            pl.BlockSpec((tm, D_MODEL), lambda i: (i, 0)),
            pl.BlockSpec((1, D_MODEL), lambda i: (0, 0)),
            pl.BlockSpec((D_MODEL, wtot), lambda i: (0, 0)),
            pl.BlockSpec((4, LANES), lambda i: (0, 0)),
            tab_spec, tab_spec, tab_spec,
            pl.BlockSpec((LANES, LANES), lambda i: (0, 0)),
        ],
        out_specs=[pl.BlockSpec((tm, w_), lambda i: (i, 0)) for w_ in widths],
        out_shape=[jax.ShapeDtypeStruct((t, w_), F32) for w_ in widths],
        compiler_params=_cparams(("parallel",), VMEM_LIMIT),
    )(x2d, g, w, gains, *tables, bd)


def _band_mask(rows, nkeys, max_dist, has_prev):
    qi = lax.broadcasted_iota(jnp.int32, (rows, nkeys), 0) & (BLOCK - 1)
    kk = lax.broadcasted_iota(jnp.int32, (rows, nkeys), 1)
    if nkeys == BLOCK:
        dist = qi - kk
        return jnp.logical_and(dist >= 0, dist <= max_dist)
    dist = qi + BLOCK - kk
    band = jnp.logical_and(dist >= 0, dist <= max_dist)
    return jnp.logical_and(band, jnp.logical_or(kk >= BLOCK, has_prev))


UNIT_ROWS = 2 * BLOCK


def _band_bias(nkeys, max_dist, has_prev):
    return jnp.where(_band_mask(UNIT_ROWS, nkeys, max_dist, has_prev), 0.0, -jnp.inf).astype(F32)


def _softmax_stage(s_ref, bias, sink, p_ref, t_ref):
    s = s_ref[...] + bias
    m = jnp.max(s, axis=1, keepdims=True)
    if sink is not None:
        m = jnp.maximum(m, sink)
    p = jnp.exp(s - m)
    denom = jnp.sum(p, axis=1, keepdims=True)
    if sink is not None:
        denom = denom + jnp.exp(sink - m)
    p_ref[...] = p.astype(BF16)
    t_ref[:, :LANES] = jnp.broadcast_to(1.0 / denom, (UNIT_ROWS, LANES))
    t_ref[:, LANES:] = jnp.broadcast_to(m + jnp.log(denom), (UNIT_ROWS, LANES))


def _pipeline(n_units, qk, sm, pv):
    qk(0, 0)
    qk(1, 1)
    sm(0, 0)

    def body(j, carry):
        i = 2 * j
        pv(i - 2, 0)
        sm(i - 1, 1)
        qk(i, 0)
        pv(i - 1, 1)
        sm(i, 0)
        qk(i + 1, 1)
        return carry

    lax.fori_loop(1, n_units // 2, body, 0)
    pv(n_units - 2, 0)
    sm(n_units - 1, 1)
    pv(n_units - 1, 1)


def _attn_a_body(sink_ref, q_ref, kv_ref, o_ref, s_scr, p_scr, t_scr, bias_scr, *, s):
    nb = s // BLOCK
    low = lax.broadcasted_iota(jnp.int32, (BLOCK, LANES), 1) < HEAD_DIM
    high = jnp.logical_not(low)
    top = lax.broadcasted_iota(jnp.int32, (UNIT_ROWS, 1), 0) < BLOCK
    bias_scr[0] = _band_bias(2 * BLOCK, A_WINDOW - 1, False)
    bias_scr[1] = _band_bias(2 * BLOCK, A_WINDOW - 1, True)
    pairs = A_KV_HEADS // 2
    per_pair = A_HEADS // A_KV_HEADS
    units = [(c, hp) for c in range(pairs) for hp in range(per_pair)]

    def rows2(b, lanes):
        start_c = _multiple_of(b * BLOCK, BLOCK)
        start_p = _multiple_of(jnp.maximum(b - 1, 0) * BLOCK, BLOCK)
        return jnp.concatenate([kv_ref[pl.ds(start_p, BLOCK), lanes], kv_ref[pl.ds(start_c, BLOCK), lanes]], axis=0)

    def qk(b, slot):
        start_c = _multiple_of(b * BLOCK, BLOCK)
        for c in range(pairs):
            k2 = rows2(b, slice(c * LANES, (c + 1) * LANES)).astype(BF16)
            for hp in range(per_pair):
                col = c * per_pair + hp
                q = q_ref[pl.ds(start_c, BLOCK), col * LANES:(col + 1) * LANES]
                qr = pltpu.roll(q, HEAD_DIM, 1)
                if hp < per_pair // 2:
                    lhs = jnp.concatenate([jnp.where(low, q, 0.0), jnp.where(low, qr, 0.0)], axis=0)
                else:
                    lhs = jnp.concatenate([jnp.where(high, qr, 0.0), jnp.where(high, q, 0.0)], axis=0)
                s_scr[slot, units.index((c, hp))] = _dot_nt(lhs.astype(BF16), k2)

    def sm(b, slot):
        bias = bias_scr[jnp.minimum(b, 1)]
        for ui, (c, hp) in enumerate(units):
            col = c * per_pair + hp
            sink = jnp.where(top, sink_ref[2 * col], sink_ref[2 * col + 1])
            _softmax_stage(s_scr.at[slot, ui], bias, sink, p_scr.at[slot, ui], t_scr.at[slot, ui])

    def pv(b, slot):
        start_c = _multiple_of(b * BLOCK, BLOCK)
        for c in range(pairs):
            v2 = rows2(b, slice(A_KV_W + c * LANES, A_KV_W + (c + 1) * LANES)).astype(BF16)
            for hp in range(per_pair):
                ui = units.index((c, hp))
                col = c * per_pair + hp
                o2 = _dot(p_scr[slot, ui], v2) * t_scr[slot, ui, :, :LANES]
                if hp < per_pair // 2:
                    o = jnp.where(low, o2[:BLOCK], pltpu.roll(o2[BLOCK:], HEAD_DIM, 1))
                else:
                    o = jnp.where(low, pltpu.roll(o2[:BLOCK], HEAD_DIM, 1), o2[BLOCK:])
                o_ref[pl.ds(start_c, BLOCK), col * LANES:(col + 1) * LANES] = o.astype(o_ref.dtype)

    _pipeline(nb, qk, sm, pv)


def _attn_a_call(qa, kva, sinks, n, s):
    n_units = (A_KV_HEADS // 2) * (A_HEADS // A_KV_HEADS)
    return pl.pallas_call(
        functools.partial(_attn_a_body, s=s),
        name="attn_a",
        grid=(n,),
        in_specs=[
            pl.BlockSpec(memory_space=pltpu.SMEM),
            pl.BlockSpec((s, A_Q_W), lambda i: (i, 0)),
            pl.BlockSpec((s, 2 * A_KV_W), lambda i: (i, 0)),
        ],
        out_specs=pl.BlockSpec((s, A_Q_W), lambda i: (i, 0)),
        out_shape=jax.ShapeDtypeStruct((n * s, A_Q_W), BF16),
        scratch_shapes=[pltpu.VMEM((2, n_units, UNIT_ROWS, 2 * BLOCK), F32),
                        pltpu.VMEM((2, n_units, UNIT_ROWS, 2 * BLOCK), BF16),
                        pltpu.VMEM((2, n_units, UNIT_ROWS, 2 * LANES), F32),
                        pltpu.VMEM((2, UNIT_ROWS, 2 * BLOCK), F32)],
        compiler_params=_cparams(("parallel",), VMEM_LIMIT),
    )(sinks, qa, kva)


def _attn_b_body(q1, q2, q3, k1, v1, k2, v2, k3, v3, o_ref, o_scr, l_scr, s_scr, p_scr, t_scr, bias_scr, *, s):
    low = lax.broadcasted_iota(jnp.int32, (BLOCK, LANES), 1) < HEAD_DIM
    bias_scr[0] = _band_bias(2 * BLOCK, BLOCK, False)
    bias_scr[1] = _band_bias(2 * BLOCK, BLOCK, True)
    bias_cur = _band_bias(BLOCK, BLOCK, False)
    for g, ((_, dil), q_ref, k_ref, v_ref) in enumerate(zip(B_PATTERNS, (q1, q2, q3), (k1, k2, k3), (v1, v2, v3))):
        span = BLOCK * dil
        nblk = max(s // span, 1)
        use_prev = nblk > 1
        nk = 2 * BLOCK if use_prev else BLOCK

        def rows(ref, start, dil=dil):
            if dil == 1:
                return ref[pl.ds(_multiple_of(start, BLOCK), BLOCK), :]
            return ref[pl.ds(start, BLOCK, stride=dil), :]

        def starts(u, dil=dil, span=span):
            b = u // dil
            return b * span + u % dil, jnp.maximum(b - 1, 0) * span + u % dil, b

        def rows2(ref, u, rows=rows, starts=starts, use_prev=use_prev):
            start_c, start_p, _ = starts(u)
            if use_prev:
                return jnp.concatenate([rows(ref, start_p), rows(ref, start_c)], axis=0)
            return rows(ref, start_c)

        def qk(u, slot, q_ref=q_ref, k_ref=k_ref, rows=rows, starts=starts, rows2=rows2, nk=nk):
            q = rows(q_ref, starts(u)[0])
            lhs = jnp.concatenate([jnp.where(low, q, 0.0), jnp.where(low, 0.0, q)], axis=0).astype(BF16)
            s_scr[slot, :, :nk] = _dot_nt(lhs, rows2(k_ref, u).astype(BF16))

        def sm(u, slot, starts=starts, use_prev=use_prev, nk=nk):
            bias = bias_scr[jnp.minimum(starts(u)[2], 1)] if use_prev else bias_cur
            _softmax_stage(s_scr.at[slot, :, :nk], bias, None, p_scr.at[slot, :, :nk], t_scr.at[slot])

        def pv(u, slot, g=g, dil=dil, v_ref=v_ref, starts=starts, rows2=rows2, nk=nk):
            o2 = _dot(p_scr[slot, :, :nk], rows2(v_ref, u).astype(BF16)) * t_scr[slot, :, :LANES]
            l2 = t_scr[slot, :, LANES:]
            o = jnp.where(low, o2[:BLOCK], o2[BLOCK:])
            lse = jnp.where(low, l2[:BLOCK], l2[BLOCK:])
            start_c = starts(u)[0]
            if dil == 1:
                idx = pl.ds(_multiple_of(start_c, BLOCK), BLOCK)
            else:
                idx = pl.ds(start_c, BLOCK, stride=dil)
            o_scr[g, idx, :] = o
            l_scr[g, idx, :] = lse

        _pipeline(nblk * dil, qk, sm, pv)

    chunk = 256

    def comb(c, carry):
        r0 = pl.multiple_of(c * chunk, chunk)
        l0, l1, l2 = (l_scr[g, pl.ds(r0, chunk), :] for g in range(B_GROUPS))
        m = jnp.maximum(jnp.maximum(l0, l1), l2)
        w0, w1, w2 = jnp.exp(l0 - m), jnp.exp(l1 - m), jnp.exp(l2 - m)
        inv = 1.0 / (w0 + w1 + w2)
        acc = (o_scr[0, pl.ds(r0, chunk), :] * (w0 * inv) + o_scr[1, pl.ds(r0, chunk), :] * (w1 * inv)
               + o_scr[2, pl.ds(r0, chunk), :] * (w2 * inv))
        o_ref[pl.ds(r0, chunk), :] = acc.astype(o_ref.dtype)
        return carry

    lax.fori_loop(0, s // chunk, comb, 0)


def _attn_b_call(qb, kvs, n, s):
    pairs = B_GW // LANES
    in_specs = [pl.BlockSpec((s, LANES), lambda i, h, g=g: (i, g * pairs + h)) for g in range(B_GROUPS)]
    args = [qb, qb, qb]
    for kv in kvs:
        in_specs.append(pl.BlockSpec((s, LANES), lambda i, h: (i, h)))
        in_specs.append(pl.BlockSpec((s, LANES), lambda i, h: (i, pairs + h)))
        args += [kv, kv]
    return pl.pallas_call(
        functools.partial(_attn_b_body, s=s),
        name="attn_b",
        grid=(n, pairs),
        in_specs=in_specs,
        out_specs=pl.BlockSpec((s, LANES), lambda i, h: (i, h)),
        out_shape=jax.ShapeDtypeStruct((n * s, B_GW), BF16),
        scratch_shapes=[pltpu.VMEM((B_GROUPS, s, LANES), F32), pltpu.VMEM((B_GROUPS, s, LANES), F32),
                        pltpu.VMEM((2, UNIT_ROWS, 2 * BLOCK), F32), pltpu.VMEM((2, UNIT_ROWS, 2 * BLOCK), BF16),
                        pltpu.VMEM((2, UNIT_ROWS, 2 * LANES), F32), pltpu.VMEM((2, UNIT_ROWS, 2 * BLOCK), F32)],
        compiler_params=_cparams(("parallel", "arbitrary"), VMEM_LIMIT),
    )(*args)


def _shift_insert(old_ref, new_ref, out_ref, w):
    ncol = w // LANES
    keep = lax.broadcasted_iota(jnp.int32, (BLOCK, LANES), 1) < LANES - SAMPLE_T

    def strip(k, carry):
        r0 = pl.multiple_of(k * BLOCK, BLOCK)
        x = old_ref[pl.ds(r0, BLOCK), :]
        pad = jnp.concatenate([jnp.zeros((BLOCK - SAMPLE_T, BLOCK), F32), new_ref[:, pl.ds(r0, BLOCK)]], axis=0)
        nxt = pad.T
        for j in reversed(range(ncol)):
            cur = pltpu.roll(x[:, j * LANES:(j + 1) * LANES], LANES - SAMPLE_T, 1)
            out_ref[pl.ds(r0, BLOCK), j * LANES:(j + 1) * LANES] = jnp.where(keep, cur, nxt)
            nxt = cur
        return carry

    lax.fori_loop(0, old_ref.shape[0] // BLOCK, strip, 0)


def _window_attention(lhs, buf_ref, old_ref, krow, vrow, w, dil, min_old, sink):
    r = lhs.shape[0]
    t_main = lax.broadcasted_iota(jnp.int32, (r, w), 0) & (SAMPLE_T - 1)
    delta = (w - SAMPLE_T) + t_main - lax.broadcasted_iota(jnp.int32, (r, w), 1)
    valid_main = jnp.logical_and(delta >= 0, (delta & (dil - 1)) == 0)
    t_old = lax.broadcasted_iota(jnp.int32, (r, LANES), 0) & (SAMPLE_T - 1)
    c_old = lax.broadcasted_iota(jnp.int32, (r, LANES), 1)
    valid_old = jnp.logical_and(jnp.logical_and(c_old < SAMPLE_T, c_old >= t_old + min_old),
                                ((w + t_old - c_old) & (dil - 1)) == 0)
    kt = buf_ref[krow:krow + LANES, :].astype(BF16)
    vt = buf_ref[vrow:vrow + LANES, :].astype(BF16)
    kx = old_ref[krow:krow + LANES, 0:LANES].astype(BF16)
    vx = old_ref[vrow:vrow + LANES, 0:LANES].astype(BF16)
    s = jnp.where(valid_main, _dot(lhs, kt), -jnp.inf)
    sx = jnp.where(valid_old, _dot(lhs, kx), -jnp.inf)
    m = jnp.maximum(jnp.max(s, axis=1, keepdims=True), jnp.max(sx, axis=1, keepdims=True))
    if sink is not None:
        m = jnp.maximum(m, sink)
    p = jnp.exp(s - m)
    px = jnp.exp(sx - m)
    denom = jnp.sum(p, axis=1, keepdims=True) + jnp.sum(px, axis=1, keepdims=True)
    if sink is not None:
        denom = denom + jnp.exp(sink - m)
    inv = 1.0 / denom
    o = _dot_nt((p * inv).astype(BF16), vt) + _dot_nt((px * inv).astype(BF16), vx)
    return o, m + jnp.log(denom)


def _sample_body(sink_ref, qa_ref, kva_ref, qb_ref, kvb1_ref, kvb2_ref, kvb3_ref, ca_ref, cb1_ref, cb2_ref, cb3_ref,
                 oa_ref, ob_ref, na_ref, nb1_ref, nb2_ref, nb3_ref):
    low = lax.broadcasted_iota(jnp.int32, (SAMPLE_T, LANES), 1) < HEAD_DIM

    _shift_insert(ca_ref, kva_ref, na_ref, A_WINDOW)
    qa = qa_ref[...]
    heads_per_pair = 2 * (A_HEADS // A_KV_HEADS)
    outs = {}
    for c in range(A_KV_HEADS // 2):
        blocks, sinks = [], []
        for jj in range(heads_per_pair):
            j = c * heads_per_pair + jj
            want_low = jj < heads_per_pair // 2
            q = qa[:, (j // 2) * LANES:(j // 2 + 1) * LANES]
            if (j % 2 == 0) != want_low:
                q = pltpu.roll(q, HEAD_DIM, 1)
            blocks.append(jnp.where(low if want_low else jnp.logical_not(low), q, 0.0))
            sinks.append(jnp.full((SAMPLE_T, 1), sink_ref[j], F32))
        o, _ = _window_attention(jnp.concatenate(blocks, axis=0).astype(BF16), na_ref, ca_ref, c * LANES,
                                 A_KV_W + c * LANES, A_WINDOW, 1, 1, jnp.concatenate(sinks, axis=0))
        for jj in range(heads_per_pair):
            j = c * heads_per_pair + jj
            ob = o[jj * SAMPLE_T:(jj + 1) * SAMPLE_T]
            if (j % 2 == 0) != (jj < heads_per_pair // 2):
                ob = pltpu.roll(ob, HEAD_DIM, 1)
            outs[j] = ob
    for c in range(A_HEADS // 2):
        oa_ref[:, c * LANES:(c + 1) * LANES] = jnp.where(low, outs[2 * c], outs[2 * c + 1])

    qb = qb_ref[...]
    new_refs = (kvb1_ref, kvb2_ref, kvb3_ref)
    old_refs = (cb1_ref, cb2_ref, cb3_ref)
    buf_refs = (nb1_ref, nb2_ref, nb3_ref)
    pairs = B_GW // LANES
    o_g, l_g = [], []
    for g, (win, dil) in enumerate(B_PATTERNS):
        _shift_insert(old_refs[g], new_refs[g], buf_refs[g], win)
        o_cols, l_cols = [], []
        for c in range(pairs):
            q = qb[:, g * B_GW + c * LANES:g * B_GW + (c + 1) * LANES]
            lhs = jnp.concatenate([jnp.where(low, q, 0.0), jnp.where(low, 0.0, q)], axis=0).astype(BF16)
            o, lse = _window_attention(lhs, buf_refs[g], old_refs[g], c * LANES, B_GW + c * LANES, win, dil, 0, None)
            lse = jnp.broadcast_to(lse, (2 * SAMPLE_T, LANES))
            o_cols.append(jnp.where(low, o[:SAMPLE_T], o[SAMPLE_T:]))
            l_cols.append(jnp.where(low, lse[:SAMPLE_T], lse[SAMPLE_T:]))
        o_g.append(jnp.concatenate(o_cols, axis=1))
        l_g.append(jnp.concatenate(l_cols, axis=1))
    m = jnp.maximum(jnp.maximum(l_g[0], l_g[1]), l_g[2])
    w0, w1, w2 = jnp.exp(l_g[0] - m), jnp.exp(l_g[1] - m), jnp.exp(l_g[2] - m)
    inv = 1.0 / (w0 + w1 + w2)
    ob_ref[...] = o_g[0] * (w0 * inv) + o_g[1] * (w1 * inv) + o_g[2] * (w2 * inv)


def _sample_call(sinks, qa, kva, qb, kvbs, caches, ns):
    def rows(width):
        return pl.BlockSpec((SAMPLE_T, width), lambda i: (i, 0))

    def cache(c):
        return pl.BlockSpec((None,) + c.shape[1:], lambda i: (i, 0, 0))

    news = [qa, kva, qb, *kvbs]
    in_specs = [pl.BlockSpec(memory_space=pltpu.SMEM)] + [rows(a.shape[1]) for a in news] + [cache(c) for c in caches]
    out_shape = ([jax.ShapeDtypeStruct((ns * SAMPLE_T, A_Q_W), F32), jax.ShapeDtypeStruct((ns * SAMPLE_T, B_GW), F32)]
                 + [jax.ShapeDtypeStruct(c.shape, F32) for c in caches])
    out_specs = [rows(A_Q_W), rows(B_GW)] + [cache(c) for c in caches]
    return pl.pallas_call(
        _sample_body,
        name="sample_attn",
        grid=(ns,),
        in_specs=in_specs,
        out_specs=out_specs,
        out_shape=out_shape,
        compiler_params=_cparams(("parallel",), VMEM_LIMIT),
    )(sinks, *news, *caches)


def _tail_transpose_body(x_ref, o_ref):
    o_ref[...] = x_ref[...].T


def _tail_transpose_call(name, kv, n, s, win):
    c = kv.shape[1]
    tt = min(win, 2 * LANES)
    per_seq, first = s // tt, (s - win) // tt
    return pl.pallas_call(
        _tail_transpose_body,
        name=name,
        grid=(n, win // tt),
        in_specs=[pl.BlockSpec((tt, c), lambda i, j: (i * per_seq + first + j, 0))],
        out_specs=pl.BlockSpec((None, c, tt), lambda i, j: (i, 0, j)),
        out_shape=jax.ShapeDtypeStruct((n, c, win), F32),
        compiler_params=_cparams(("parallel", "parallel")),
    )(kv)


def _merge_body(x_ref, g_ref, oa_ref, ob_ref, wga_ref, wgb_ref, wba_ref, wbb_ref, o_ref, h_scr):
    @pl.when(pl.program_id(1) == 0)
    def _():
        h_scr[...] = _rmsnorm(x_ref[...], g_ref[...]).astype(BF16)

    h = h_scr[...]
    ga = _dot(h, wga_ref[...])
    gb = _dot(h, wgb_ref[...])
    ya = _dot(oa_ref[...].astype(BF16), wba_ref[...])
    yb = _dot(ob_ref[...].astype(BF16), wbb_ref[...])
    o_ref[...] = (jax.nn.sigmoid(ga) * ya + jax.nn.sigmoid(gb) * yb).astype(o_ref.dtype)


def _merge_call(x2d, g, oa, ob, w_gates, w_ba, w_bb, tm, tn):
    t = x2d.shape[0]
    ncol = D_MODEL // tn
    return pl.pallas_call(
        _merge_body,
        name="merge",
        grid=(t // tm, ncol),
        in_specs=[
            pl.BlockSpec((tm, D_MODEL), lambda i, j: (i, 0)),
            pl.BlockSpec((1, D_MODEL), lambda i, j: (0, 0)),
            pl.BlockSpec((tm, A_Q_W), lambda i, j: (i, 0)),
            pl.BlockSpec((tm, B_GW), lambda i, j: (i, 0)),
            pl.BlockSpec((D_MODEL, tn), lambda i, j: (0, j)),
            pl.BlockSpec((D_MODEL, tn), lambda i, j: (0, ncol + j)),
            pl.BlockSpec((A_Q_W, tn), lambda i, j: (0, j)),
            pl.BlockSpec((B_GW, tn), lambda i, j: (0, j)),
        ],
        out_specs=pl.BlockSpec((tm, tn), lambda i, j: (i, j)),
        out_shape=jax.ShapeDtypeStruct((t, D_MODEL), BF16),
        scratch_shapes=[pltpu.VMEM((tm, D_MODEL), BF16)],
        compiler_params=_cparams(("parallel", "arbitrary"), VMEM_LIMIT),
    )(x2d, g, oa, ob, w_gates, w_gates, w_ba, w_bb)


def _route(logits):
    tm = logits.shape[0]
    lane = lax.broadcasted_iota(jnp.int32, (tm, LANES), 1)
    neg = -jnp.inf
    big = LANES

    def first_where(cond):
        return jnp.min(jnp.where(cond, lane, big), axis=1, keepdims=True)

    gl = jnp.where(lane < N_EXPERT_GROUPS, logits, neg)
    gmax = jnp.max(gl, axis=1, keepdims=True)
    gidx = first_where(gl == gmax)
    g_w = 1.0 / jnp.sum(jnp.exp(gl - gmax), axis=1, keepdims=True)
    lo = N_EXPERT_GROUPS + gidx * EXPERTS_PER_GROUP
    in_grp = jnp.logical_and(lane >= lo, lane < lo + EXPERTS_PER_GROUP)
    el = jnp.where(in_grp, logits, neg)
    ep = jnp.exp(el - jnp.max(el, axis=1, keepdims=True))
    prob = ep / jnp.sum(ep, axis=1, keepdims=True)
    prob = jnp.where(in_grp, prob, -1.0)
    p1 = jnp.max(prob, axis=1, keepdims=True)
    i1 = first_where(prob == p1)
    prob2 = jnp.where(lane == i1, -1.0, prob)
    p2 = jnp.max(prob2, axis=1, keepdims=True)
    i2 = first_where(prob2 == p2)
    tot = p1 + p2
    c1 = g_w * (p1 / tot)
    c2 = g_w * (p2 / tot)
    e1 = (i1 - N_EXPERT_GROUPS).astype(F32)
    e2 = (i2 - N_EXPERT_GROUPS).astype(F32)
    return e1, e2, c1, c2


def _outproj_body(mp_ref, xp_ref, ms_ref, xs_ref, wo_ref, g_ref, wrh_ref, wrl_ref, br_ref, x1_ref, route_ref, *, n_prompt):
    def compute(m_ref, x_ref):
        x1 = x_ref[...] + _dot(m_ref[...], wo_ref[...])
        x1_ref[...] = x1
        hi, lo = _split2(_rmsnorm(x1, g_ref[...]))
        logits = _dot(hi, wrh_ref[...]) + _dot(hi, wrl_ref[...]) + _dot(lo, wrh_ref[...]) + br_ref[...]
        e1, e2, c1, c2 = _route(logits)
        lane = lax.broadcasted_iota(jnp.int32, logits.shape, 1)
        route_ref[...] = jnp.where(lane == 0, e1, jnp.where(lane == 1, e2, jnp.where(lane == 2, c1, jnp.where(lane == 3, c2, 0.0))))

    i = pl.program_id(0)
    pl.when(i < n_prompt)(lambda: compute(mp_ref, xp_ref))
    pl.when(i >= n_prompt)(lambda: compute(ms_ref, xs_ref))


def _outproj_call(merged_p, xp, merged_s, xs, w_o, g, wr_hi, wr_lo, br, tm):
    n_p, n_s = xp.shape[0] // tm, xs.shape[0] // tm
    t_all = xp.shape[0] + xs.shape[0]

    def p_map(i):
        return (jnp.minimum(i, n_p - 1), 0)

    def s_map(i):
        return (jnp.maximum(i - n_p, 0), 0)

    def const(shape):
        return pl.BlockSpec(shape, lambda i: (0, 0))

    return pl.pallas_call(
        functools.partial(_outproj_body, n_prompt=n_p),
        name="outproj",
        grid=(n_p + n_s,),
        in_specs=[
            pl.BlockSpec((tm, D_MODEL), p_map), pl.BlockSpec((tm, D_MODEL), p_map),
            pl.BlockSpec((tm, D_MODEL), s_map), pl.BlockSpec((tm, D_MODEL), s_map),
            const((D_MODEL, D_MODEL)), const((1, D_MODEL)), const((D_MODEL, LANES)), const((D_MODEL, LANES)),
            const((1, LANES)),
        ],
        out_specs=[pl.BlockSpec((tm, D_MODEL), lambda i: (i, 0)), pl.BlockSpec((tm, LANES), lambda i: (i, 0))],
        out_shape=[jax.ShapeDtypeStruct((t_all, D_MODEL), F32), jax.ShapeDtypeStruct((t_all, LANES), F32)],
        compiler_params=_cparams(("arbitrary",), VMEM_LIMIT),
    )(merged_p, xp, merged_s, xs, w_o, g, wr_hi, wr_lo, br)


def _row_copy(src, src_row, dst, dst_row, sem):
    return pltpu.make_async_copy(src.at[pl.ds(src_row, 1), :], dst.at[pl.ds(dst_row, 1), :], sem)


def _dispatch_body(pos_ref, pad_tile_ref, x1_ref, g_ref, xs_hbm, h_scr, sem, *, tile):
    tm = h_scr.shape[0]

    @pl.when(pl.program_id(0) == 0)
    def _():
        h_scr[...] = jnp.zeros_like(h_scr)

        def pad_copy(t, k):
            row = pl.multiple_of(t * tile + k * tm, tm)
            return pltpu.make_async_copy(h_scr, xs_hbm.at[pl.ds(row, tm), :], sem)

        n_used = pad_tile_ref[N_EXPERTS]
        for phase in ("start", "wait"):
            def expert_pad(e, carry, phase=phase):
                @pl.when(pad_tile_ref[e] >= 0)
                def _():
                    for k in range(tile // tm):
                        getattr(pad_copy(pad_tile_ref[e], k), phase)()
                return carry
            lax.fori_loop(0, N_EXPERTS, expert_pad, 0)

            def unused_tile(t, carry, phase=phase):
                @pl.when(t >= n_used)
                def _():
                    for k in range(tile // tm):
                        getattr(pad_copy(t, k), phase)()
                return carry
            lax.fori_loop(0, xs_hbm.shape[0] // tile, unused_tile, 0)

    h_scr[...] = _rmsnorm(x1_ref[...], g_ref[...])

    def issue(j, carry):
        _row_copy(h_scr, j, xs_hbm, pos_ref[0, 0, 2 * j], sem).start()
        _row_copy(h_scr, j, xs_hbm, pos_ref[0, 0, 2 * j + 1], sem).start()
        return carry

    lax.fori_loop(0, tm, issue, 0, unroll=8)

    def drain(j, carry):
        _row_copy(h_scr, j, xs_hbm, pos_ref[0, 0, 2 * j], sem).wait()
        _row_copy(h_scr, j, xs_hbm, pos_ref[0, 0, 2 * j + 1], sem).wait()
        return carry

    lax.fori_loop(0, tm, drain, 0, unroll=8)


def _dispatch_call(pos3, pad_tile, x1, g, rows, tm, tile):
    t = x1.shape[0]
    assert tile % tm == 0
    return pl.pallas_call(
        functools.partial(_dispatch_body, tile=tile),
        name="dispatch",
        grid=(t // tm,),
        in_specs=[
            pl.BlockSpec((1, 1, 2 * tm), lambda i: (i, 0, 0), memory_space=pltpu.SMEM),
            pl.BlockSpec(memory_space=pltpu.SMEM),
            pl.BlockSpec((tm, D_MODEL), lambda i: (i, 0)),
            pl.BlockSpec((1, D_MODEL), lambda i: (0, 0)),
        ],
        out_specs=pl.BlockSpec(memory_space=pl.ANY),
        out_shape=jax.ShapeDtypeStruct((rows, D_MODEL), F32),
        scratch_shapes=[pltpu.VMEM((tm, D_MODEL), F32), pltpu.SemaphoreType.DMA(())],
        compiler_params=_cparams(("arbitrary",)),
    )(pos3, pad_tile, x1, g)


def _moe_body(te_ref, tv_ref, nu_ref, xs_ref, wg_ref, wu_ref, wd_ref, ys_ref):
    i = pl.program_id(0)

    @pl.when(i < nu_ref[0])
    def _():
        row = lax.broadcasted_iota(jnp.int32, xs_ref.shape, 0)
        x = jnp.where(row < tv_ref[i], xs_ref[...], 0.0).astype(BF16)
        a = jax.nn.silu(_dot(x, wg_ref[0])) * _dot(x, wu_ref[0])
        ys_ref[...] = _dot(a.astype(BF16), wd_ref[0])

    @pl.when(i >= nu_ref[0])
    def _():
        ys_ref[...] = jnp.zeros_like(ys_ref)


def _moe_call(tile_expert, tile_valid, n_used, xs, w_gate, w_up, w_down, tm):
    rows = xs.shape[0]

    def row_map(i, te, tv, nu):
        return (jnp.minimum(i, nu[0] - 1), 0)

    def out_map(i, te, tv, nu):
        return (i, 0)

    def w_map(i, te, tv, nu):
        return (te[i], 0, 0)

    return pl.pallas_call(
        _moe_body,
        name="moe",
        grid_spec=pltpu.PrefetchScalarGridSpec(
            num_scalar_prefetch=3,
            grid=(rows // tm,),
            in_specs=[
                pl.BlockSpec((tm, D_MODEL), row_map),
                pl.BlockSpec((1, D_MODEL, D_EXPERT), w_map),
                pl.BlockSpec((1, D_MODEL, D_EXPERT), w_map),
                pl.BlockSpec((1, D_EXPERT, D_MODEL), w_map),
            ],
            out_specs=pl.BlockSpec((tm, D_MODEL), out_map),
        ),
        out_shape=jax.ShapeDtypeStruct((rows, D_MODEL), F32),
        compiler_params=_cparams(("arbitrary",), VMEM_LIMIT),
    )(tile_expert, tile_valid, n_used, xs, w_gate, w_up, w_down)


def _combine_body(pos_ref, x1_ref, route_ref, ys_hbm, y_ref, buf0, buf1, sem):
    tm = buf0.shape[0]

    def issue(j, carry):
        _row_copy(ys_hbm, pos_ref[0, 0, 2 * j], buf0, j, sem).start()
        _row_copy(ys_hbm, pos_ref[0, 0, 2 * j + 1], buf1, j, sem).start()
        return carry

    lax.fori_loop(0, tm, issue, 0, unroll=8)

    def drain(j, carry):
        _row_copy(ys_hbm, pos_ref[0, 0, 2 * j], buf0, j, sem).wait()
        _row_copy(ys_hbm, pos_ref[0, 0, 2 * j + 1], buf1, j, sem).wait()
        return carry

    lax.fori_loop(0, tm, drain, 0, unroll=8)
    route = route_ref[...]
    y_ref[...] = x1_ref[...] + (route[:, 2:3] * buf0[...] + route[:, 3:4] * buf1[...])


def _combine_call(name, pos3, x1_all, route, ys, t, row0, tm):
    off = row0 // tm
    return pl.pallas_call(
        _combine_body,
        name=name,
        grid=(t // tm,),
        in_specs=[
            pl.BlockSpec((1, 1, 2 * tm), lambda i: (i + off, 0, 0), memory_space=pltpu.SMEM),
            pl.BlockSpec((tm, D_MODEL), lambda i: (i + off, 0)),
            pl.BlockSpec((tm, LANES), lambda i: (i + off, 0)),
            pl.BlockSpec(memory_space=pl.ANY),
        ],
        out_specs=pl.BlockSpec((tm, D_MODEL), lambda i: (i, 0)),
        out_shape=jax.ShapeDtypeStruct((t, D_MODEL), F32),
        scratch_shapes=[pltpu.VMEM((tm, D_MODEL), F32), pltpu.VMEM((tm, D_MODEL), F32), pltpu.SemaphoreType.DMA(())],
        compiler_params=_cparams(("arbitrary",)),
    )(pos3, x1_all, route, ys)


def _routing_offsets(route, tm, n_tiles):
    e = route[:, :2].astype(jnp.int32)
    onehot = jnp.sum((e[:, :, None] == jnp.arange(N_EXPERTS, dtype=jnp.int32)[None, None, :]).astype(jnp.int32), axis=1)
    csum = jnp.cumsum(onehot, axis=0)
    rank = csum - onehot
    counts = csum[-1]
    tiles_e = (counts + tm - 1) // tm
    tile_end = jnp.cumsum(tiles_e)
    tile_start = tile_end - tiles_e
    pos = tile_start[e] * tm + jnp.take_along_axis(rank, e, axis=1)
    n_used = tile_end[-1]
    tile_id = jnp.minimum(jnp.arange(n_tiles, dtype=jnp.int32), n_used - 1)
    tile_expert = jnp.sum((tile_id[:, None] >= tile_end[None, :]).astype(jnp.int32), axis=1)
    tile_valid = jnp.clip(counts[tile_expert] - (tile_id - tile_start[tile_expert]) * tm, 0, tm)
    pad_tile = jnp.concatenate([jnp.where(tiles_e > 0, tile_end - 1, -1), n_used.reshape(1)])
    i32 = jnp.int32
    return (pos.astype(i32), tile_expert.astype(i32), tile_valid.astype(i32), n_used.reshape(1).astype(i32),
            pad_tile.astype(i32))


def _pick_tile(t, pref):
    tm = min(pref, t)
    assert t % tm == 0, (t, tm)
    return tm


def kernel(x_prompt, x_sample, cache_a_kv, cache_b1_kv, cache_b2_kv, cache_b3_kv, g_attn_norm, w_in, q_norm_a, k_norm_a, q_norm_b, k_norm_b, sinks_a, w_branch_a, w_branch_b, w_out, g_ffn_norm, w_group_router, b_group_router, w_expert_router, b_expert_router, w_expert_gate, w_expert_up, w_expert_down):
    n, s, _ = x_prompt.shape
    ns, ts, _ = x_sample.shape
    assert ts == SAMPLE_T and s == B_PATTERNS[-1][0] and x_prompt.shape[2] == D_MODEL
    assert g_attn_norm.shape[0] == 1, "single layer"
    tp, tsmp = n * s, ns * ts
    t_all = tp + tsmp
    xp = x_prompt.reshape(tp, D_MODEL)
    xs_in = x_sample.reshape(tsmp, D_MODEL)

    w = w_in[0]
    o_qa, o_ka, o_va = 0, A_Q_W, A_Q_W + A_KV_W
    o_qb = o_va + A_KV_W
    o_kb, o_vb = o_qb + B_W, o_qb + 2 * B_W
    o_g = o_vb + B_W
    cols_a = [w[:, o_qa:o_ka], w[:, o_ka:o_va], w[:, o_va:o_qb]]
    cols_b = []
    for g in range(B_GROUPS):
        cols_b += [w[:, o_kb + g * B_GW:o_kb + (g + 1) * B_GW], w[:, o_vb + g * B_GW:o_vb + (g + 1) * B_GW]]
    w_pa = jnp.concatenate(cols_a, axis=1).astype(BF16)
    w_pq = w[:, o_qb:o_kb].astype(BF16)
    w_pkv = jnp.concatenate(cols_b, axis=1).astype(BF16)
    w_gates = w[:, o_g:].astype(BF16)
    rep = LANES // HEAD_DIM
    gains = jnp.stack([jnp.tile(v[0], rep) for v in (q_norm_a, k_norm_a, q_norm_b, k_norm_b)]).astype(F32)
    bd = jnp.asarray((np.arange(LANES)[:, None] // HEAD_DIM == np.arange(LANES)[None, :] // HEAD_DIM).astype(np.float32), BF16)
    g_attn = g_attn_norm.astype(F32)
    g_ffn = g_ffn_norm.astype(F32)
    nq, nk = A_Q_W // LANES, A_KV_W // LANES
    kinds_a = (0,) * nq + (1,) * nk + (-1,) * nk
    kinds_q = (2,) * (B_W // LANES)
    kinds_kv = ((3,) * (B_GW // LANES) + (-1,) * (B_GW // LANES)) * B_GROUPS

    def project(x2d, pos, tm):
        tables = _rope_tables(pos)
        qa, kva = _proj_call("proj_a", x2d, g_attn, w_pa, gains, tables, bd, kinds_a, (A_Q_W, 2 * A_KV_W), tm)
        (qb,) = _proj_call("proj_qb", x2d, g_attn, w_pq, gains, tables, bd, kinds_q, (B_W,), tm)
        kvb = _proj_call("proj_kvb", x2d, g_attn, w_pkv, gains, tables, bd, kinds_kv, (2 * B_GW,) * B_GROUPS, tm)
        return qa, kva, qb, kvb

    tm_p = _pick_tile(tp, ROW_TILE)
    qa, kva, qb, kvb = project(xp, jnp.arange(s, dtype=jnp.int32), tm_p)
    sinks = sinks_a[0].astype(F32)
    oa_p = _attn_a_call(qa, kva, sinks, n, s)
    ob_p = _attn_b_call(qb, kvb, n, s)

    def to_feature_major(c, heads):
        return jnp.transpose(c, (0, 2, 3, 4, 1)).reshape(c.shape[0], 2 * heads * HEAD_DIM, c.shape[1])

    def from_feature_major(c, heads):
        return jnp.transpose(c.reshape(c.shape[0], 2, heads, HEAD_DIM, c.shape[2]), (0, 4, 1, 2, 3))[None]

    new_a_p = from_feature_major(_tail_transpose_call("tail_a", kva, n, s, min(A_WINDOW, s)), A_KV_HEADS)
    new_b_p = [from_feature_major(_tail_transpose_call("tail_b%d" % g, kvb[g], n, s, min(win, s)), B_HPG)
               for g, (win, _) in enumerate(B_PATTERNS)]

    tm_s = _pick_tile(tsmp, ROW_TILE)
    pos_s = PAST_LEN + (jnp.arange(tm_s, dtype=jnp.int32) % ts)
    qa_s, kva_s, qb_s, kvb_s = project(xs_in, pos_s, tm_s)
    caches = [to_feature_major(cache_a_kv[0], A_KV_HEADS)]
    caches += [to_feature_major(c[0], B_HPG) for c in (cache_b1_kv, cache_b2_kv, cache_b3_kv)]
    oa_s, ob_s, na, nb1, nb2, nb3 = _sample_call(sinks, qa_s, kva_s, qb_s, kvb_s, caches, ns)
    new_a_s = from_feature_major(na, A_KV_HEADS)
    new_b_s = [from_feature_major(c, B_HPG) for c in (nb1, nb2, nb3)]

    w_ba = w_branch_a[0].astype(BF16)
    w_bb = w_branch_b[0].astype(BF16)
    w_o = w_out[0].astype(BF16)
    wr = jnp.zeros((D_MODEL, LANES), F32)
    wr = wr.at[:, :N_EXPERT_GROUPS].set(w_group_router[0]).at[:, N_EXPERT_GROUPS:N_EXPERT_GROUPS + N_EXPERTS].set(w_expert_router[0])
    wr_hi = wr.astype(BF16)
    wr_lo = (wr - wr_hi.astype(F32)).astype(BF16)
    br = jnp.zeros((1, LANES), F32)
    br = br.at[0, :N_EXPERT_GROUPS].set(b_group_router[0]).at[0, N_EXPERT_GROUPS:N_EXPERT_GROUPS + N_EXPERTS].set(b_expert_router[0])

    tn = 512
    tm_mp, tm_ms = _pick_tile(tp, 512), _pick_tile(tsmp, 512)
    merged_p = _merge_call(xp, g_attn, oa_p, ob_p, w_gates, w_ba, w_bb, tm_mp, tn)
    merged_s = _merge_call(xs_in, g_attn, oa_s, ob_s, w_gates, w_ba, w_bb, tm_ms, tn)
    tm_o = _pick_tile(tsmp, ROW_TILE)
    assert tp % tm_o == 0
    x1_all, route = _outproj_call(merged_p, xp, merged_s, xs_in, w_o, g_ffn, wr_hi, wr_lo, br, tm_o)

    tm_e = MOE_TILE
    n_tiles = (2 * t_all + N_EXPERTS * (tm_e - 1)) // tm_e + 1
    pos, tile_expert, tile_valid, n_used, pad_tile = _routing_offsets(route, tm_e, n_tiles)
    tm_d = _pick_tile(tsmp, ROW_TILE)
    pos3 = pos.reshape(t_all // tm_d, 1, 2 * tm_d)
    xs_sorted = _dispatch_call(pos3, pad_tile, x1_all, g_ffn, n_tiles * tm_e, tm_d, tm_e)
    ys = _moe_call(tile_expert, tile_valid, n_used, xs_sorted, w_expert_gate[0].astype(BF16), w_expert_up[0].astype(BF16),
                   w_expert_down[0].astype(BF16), tm_e)
    y_p = _combine_call("combine_p", pos3, x1_all, route, ys, tp, 0, tm_d)
    y_s = _combine_call("combine_s", pos3, x1_all, route, ys, tsmp, tp, tm_d)

    return (y_p.reshape(n, s, D_MODEL), y_s.reshape(ns, ts, D_MODEL),
            new_a_p, new_a_s, new_b_p[0], new_b_s[0], new_b_p[1], new_b_s[1], new_b_p[2], new_b_s[2])
```

```python
import functools

import jax
import jax.numpy as jnp
import numpy as np
from jax import lax
from jax.experimental import pallas as pl
from jax.experimental.pallas import tpu as pltpu

F32 = jnp.float32
BF16 = jnp.bfloat16

D_MODEL = 2048
HEAD_DIM = 64
ROT_DIM = HEAD_DIM // 4
ROPE_THETA = 500000.0
NORM_EPS = 1e-6
BLOCK = 128
PAST_LEN = 16384

A_HEADS = 16
A_KV_HEADS = 4
A_WINDOW = 128
B_PATTERNS = ((128, 1), (512, 4), (2048, 16))
B_GROUPS = 3
B_HPG = 8
A_Q_W = A_HEADS * HEAD_DIM
A_KV_W = A_KV_HEADS * HEAD_DIM
B_GW = B_HPG * HEAD_DIM
B_W = B_GROUPS * B_GW

N_EXPERT_GROUPS = 4
EXPERTS_PER_GROUP = 8
N_EXPERTS = N_EXPERT_GROUPS * EXPERTS_PER_GROUP
D_EXPERT = 512

LANES = 128
SUBLANES = 8
VMEM_LIMIT = 56 * 1024 * 1024
SAMPLE_T = 8

MOE_TILE = 512
ROW_TILE = 256


def _cparams(sem, vmem=None):
    return pltpu.CompilerParams(dimension_semantics=sem, vmem_limit_bytes=vmem)


def _rmsnorm(x, g):
    return x * lax.rsqrt(jnp.mean(x * x, axis=-1, keepdims=True) + NORM_EPS) * g


def _split2(v):
    hi = v.astype(BF16)
    lo = (v - hi.astype(F32)).astype(BF16)
    return hi, lo


def _split3(v):
    hi = v.astype(BF16)
    r = v - hi.astype(F32)
    mid = r.astype(BF16)
    lo = (r - mid.astype(F32)).astype(BF16)
    return hi, mid, lo


def _dot(a, b):
    return jnp.dot(a, b, preferred_element_type=F32)


def _multiple_of(x, m):
    return x if isinstance(x, int) else pl.multiple_of(x, m)


def _dot_nt(a, b):
    return lax.dot_general(a, b, (((1,), (1,)), ((), ())), preferred_element_type=F32)


def _rope_tables(pos):
    half = ROT_DIM // 2
    inv_freq = ROPE_THETA ** (-jnp.arange(half, dtype=F32) / half)
    ang = pos.astype(F32)[:, None] * inv_freq[None, :]
    cos, sin = jnp.cos(ang), jnp.sin(ang)
    p = pos.shape[0]
    ones = jnp.ones((p, HEAD_DIM - ROT_DIM), F32)
    z8 = jnp.zeros((p, half), F32)
    z48 = jnp.zeros((p, HEAD_DIM - ROT_DIM), F32)
    c = jnp.concatenate([cos, cos, ones], axis=1)
    s_up = jnp.concatenate([-sin, z8, z48], axis=1)
    s_dn = jnp.concatenate([z8, sin, z48], axis=1)
    rep = LANES // HEAD_DIM
    return jnp.tile(c, (1, rep)), jnp.tile(s_up, (1, rep)), jnp.tile(s_dn, (1, rep))


def _proj_body(x_ref, g_ref, w_ref, gain_ref, c_ref, su_ref, sd_ref, *out_refs, kinds, widths):
    h = _rmsnorm(x_ref[...], g_ref[...]).astype(BF16)
    low = lax.broadcasted_iota(jnp.int32, (x_ref.shape[0], LANES), 1) < HEAD_DIM
    col = 0
    for o_ref, width in zip(out_refs, widths):
        z_all = _dot(h, w_ref[:, col:col + width])
        for c in range(width // LANES):
            z = z_all[:, c * LANES:(c + 1) * LANES]
            kind = kinds[(col // LANES) + c]
            if kind >= 0:
                zz = z * z
                ss = jnp.where(low, jnp.sum(jnp.where(low, zz, 0.0), axis=1, keepdims=True),
                               jnp.sum(jnp.where(low, 0.0, zz), axis=1, keepdims=True))
                y = z * lax.rsqrt(ss * (1.0 / HEAD_DIM) + NORM_EPS) * gain_ref[kind:kind + 1, :]
                y = (y * c_ref[...] + pltpu.roll(y, LANES - ROT_DIM // 2, 1) * su_ref[...]
                     + pltpu.roll(y, ROT_DIM // 2, 1) * sd_ref[...])
                if kind % 2 == 0:
                    y = y * (HEAD_DIM ** -0.5)
                z = y
            o_ref[:, c * LANES:(c + 1) * LANES] = z
        col += width


def _proj_call(name, x2d, g, w, gains, tables, kinds, widths, tm):
    t = x2d.shape[0]
    wtot = sum(widths)
    pos_blocks = tables[0].shape[0] // tm
    tab_spec = pl.BlockSpec((tm, LANES), lambda i: (i % pos_blocks, 0))
    return pl.pallas_call(
        functools.partial(_proj_body, kinds=kinds, widths=widths),
        name=name,
        grid=(t // tm,),
        in_specs=[
            pl.BlockSpec((tm, D_MODEL), lambda i: (i, 0)),
            pl.BlockSpec((1, D_MODEL), lambda i: (0, 0)),
            pl.BlockSpec((D_MODEL, wtot), lambda i: (0, 0)),
            pl.BlockSpec((4, LANES), lambda i: (0, 0)),
            tab_spec, tab_spec, tab_spec,
        ],
        out_specs=[pl.BlockSpec((tm, w_), lambda i: (i, 0)) for w_ in widths],
        out_shape=[jax.ShapeDtypeStruct((t, w_), F32) for w_ in widths],
        compiler_params=_cparams(("parallel",), VMEM_LIMIT),
    )(x2d, g, w, gains, *tables)


def _band_mask(rows, nkeys, max_dist, has_prev):
    qi = lax.broadcasted_iota(jnp.int32, (rows, nkeys), 0) & (BLOCK - 1)
    kk = lax.broadcasted_iota(jnp.int32, (rows, nkeys), 1)
    if nkeys == BLOCK:
        dist = qi - kk
        return jnp.logical_and(dist >= 0, dist <= max_dist)
    dist = qi + BLOCK - kk
    band = jnp.logical_and(dist >= 0, dist <= max_dist)
    return jnp.logical_and(band, jnp.logical_or(kk >= BLOCK, has_prev))


UNIT_ROWS = 2 * BLOCK


def _band_bias(nkeys, max_dist, has_prev):
    return jnp.where(_band_mask(UNIT_ROWS, nkeys, max_dist, has_prev), 0.0, -jnp.inf).astype(F32)


def _softmax_stage(s_ref, bias, sink, p_ref, t_ref):
    s = s_ref[...] + bias
    m = jnp.max(s, axis=1, keepdims=True)
    if sink is not None:
        m = jnp.maximum(m, sink)
    p = jnp.exp(s - m)
    denom = jnp.sum(p, axis=1, keepdims=True)
    if sink is not None:
        denom = denom + jnp.exp(sink - m)
    p_ref[...] = p.astype(BF16)
    t_ref[:, :LANES] = jnp.broadcast_to(1.0 / denom, (UNIT_ROWS, LANES))
    t_ref[:, LANES:] = jnp.broadcast_to(m + jnp.log(denom), (UNIT_ROWS, LANES))


def _pipeline(n_units, qk, sm, pv):
    qk(0, 0)
    qk(1, 1)
    sm(0, 0)

    def body(j, carry):
        i = 2 * j
        pv(i - 2, 0)
        sm(i - 1, 1)
        qk(i, 0)
        pv(i - 1, 1)
        sm(i, 0)
        qk(i + 1, 1)
        return carry

    lax.fori_loop(1, n_units // 2, body, 0)
    pv(n_units - 2, 0)
    sm(n_units - 1, 1)
    pv(n_units - 1, 1)


def _attn_a_body(sink_ref, q_ref, kv_ref, o_ref, s_scr, p_scr, t_scr, bias_scr, *, s):
    nb = s // BLOCK
    low = lax.broadcasted_iota(jnp.int32, (BLOCK, LANES), 1) < HEAD_DIM
    high = jnp.logical_not(low)
    top = lax.broadcasted_iota(jnp.int32, (UNIT_ROWS, 1), 0) < BLOCK
    bias_scr[0] = _band_bias(2 * BLOCK, A_WINDOW - 1, False)
    bias_scr[1] = _band_bias(2 * BLOCK, A_WINDOW - 1, True)
    pairs = A_KV_HEADS // 2
    per_pair = A_HEADS // A_KV_HEADS
    units = [(c, hp) for c in range(pairs) for hp in range(per_pair)]

    def rows2(b, lanes):
        start_c = _multiple_of(b * BLOCK, BLOCK)
        start_p = _multiple_of(jnp.maximum(b - 1, 0) * BLOCK, BLOCK)
        return jnp.concatenate([kv_ref[pl.ds(start_p, BLOCK), lanes], kv_ref[pl.ds(start_c, BLOCK), lanes]], axis=0)

    def qk(b, slot):
        start_c = _multiple_of(b * BLOCK, BLOCK)
        for c in range(pairs):
            k2 = rows2(b, slice(c * LANES, (c + 1) * LANES)).astype(BF16)
            for hp in range(per_pair):
                col = c * per_pair + hp
                q = q_ref[pl.ds(start_c, BLOCK), col * LANES:(col + 1) * LANES]
                qr = pltpu.roll(q, HEAD_DIM, 1)
                if hp < per_pair // 2:
                    lhs = jnp.concatenate([jnp.where(low, q, 0.0), jnp.where(low, qr, 0.0)], axis=0)
                else:
                    lhs = jnp.concatenate([jnp.where(high, qr, 0.0), jnp.where(high, q, 0.0)], axis=0)
                s_scr[slot, units.index((c, hp))] = _dot_nt(lhs.astype(BF16), k2)

    def sm(b, slot):
        bias = bias_scr[jnp.minimum(b, 1)]
        for ui, (c, hp) in enumerate(units):
            col = c * per_pair + hp
            sink = jnp.where(top, sink_ref[2 * col], sink_ref[2 * col + 1])
            _softmax_stage(s_scr.at[slot, ui], bias, sink, p_scr.at[slot, ui], t_scr.at[slot, ui])

    def pv(b, slot):
        start_c = _multiple_of(b * BLOCK, BLOCK)
        for c in range(pairs):
            v2 = rows2(b, slice(A_KV_W + c * LANES, A_KV_W + (c + 1) * LANES)).astype(BF16)
            for hp in range(per_pair):
                ui = units.index((c, hp))
                col = c * per_pair + hp
                o2 = _dot(p_scr[slot, ui], v2) * t_scr[slot, ui, :, :LANES]
                if hp < per_pair // 2:
                    o = jnp.where(low, o2[:BLOCK], pltpu.roll(o2[BLOCK:], HEAD_DIM, 1))
                else:
                    o = jnp.where(low, pltpu.roll(o2[:BLOCK], HEAD_DIM, 1), o2[BLOCK:])
                o_ref[pl.ds(start_c, BLOCK), col * LANES:(col + 1) * LANES] = o.astype(o_ref.dtype)

    _pipeline(nb, qk, sm, pv)


def _attn_a_call(qa, kva, sinks, n, s):
    n_units = (A_KV_HEADS // 2) * (A_HEADS // A_KV_HEADS)
    return pl.pallas_call(
        functools.partial(_attn_a_body, s=s),
        name="attn_a",
        grid=(n,),
        in_specs=[
            pl.BlockSpec(memory_space=pltpu.SMEM),
            pl.BlockSpec((s, A_Q_W), lambda i: (i, 0)),
            pl.BlockSpec((s, 2 * A_KV_W), lambda i: (i, 0)),
        ],
        out_specs=pl.BlockSpec((s, A_Q_W), lambda i: (i, 0)),
        out_shape=jax.ShapeDtypeStruct((n * s, A_Q_W), BF16),
        scratch_shapes=[pltpu.VMEM((2, n_units, UNIT_ROWS, 2 * BLOCK), F32),
                        pltpu.VMEM((2, n_units, UNIT_ROWS, 2 * BLOCK), BF16),
                        pltpu.VMEM((2, n_units, UNIT_ROWS, 2 * LANES), F32),
                        pltpu.VMEM((2, UNIT_ROWS, 2 * BLOCK), F32)],
        compiler_params=_cparams(("parallel",), VMEM_LIMIT),
    )(sinks, qa, kva)


def _attn_b_body(q1, q2, q3, k1, v1, k2, v2, k3, v3, o_ref, o_scr, l_scr, s_scr, p_scr, t_scr, bias_scr, *, s):
    low = lax.broadcasted_iota(jnp.int32, (BLOCK, LANES), 1) < HEAD_DIM
    bias_scr[0] = _band_bias(2 * BLOCK, BLOCK, False)
    bias_scr[1] = _band_bias(2 * BLOCK, BLOCK, True)
    bias_cur = _band_bias(BLOCK, BLOCK, False)
    for g, ((_, dil), q_ref, k_ref, v_ref) in enumerate(zip(B_PATTERNS, (q1, q2, q3), (k1, k2, k3), (v1, v2, v3))):
        span = BLOCK * dil
        nblk = max(s // span, 1)
        use_prev = nblk > 1
        nk = 2 * BLOCK if use_prev else BLOCK

        def rows(ref, start, dil=dil):
            if dil == 1:
                return ref[pl.ds(_multiple_of(start, BLOCK), BLOCK), :]
            return ref[pl.ds(start, BLOCK, stride=dil), :]

        def starts(u, dil=dil, span=span):
            b = u // dil
            return b * span + u % dil, jnp.maximum(b - 1, 0) * span + u % dil, b

        def rows2(ref, u, rows=rows, starts=starts, use_prev=use_prev):
            start_c, start_p, _ = starts(u)
            if use_prev:
                return jnp.concatenate([rows(ref, start_p), rows(ref, start_c)], axis=0)
            return rows(ref, start_c)

        def qk(u, slot, q_ref=q_ref, k_ref=k_ref, rows=rows, starts=starts, rows2=rows2, nk=nk):
            q = rows(q_ref, starts(u)[0])
            lhs = jnp.concatenate([jnp.where(low, q, 0.0), jnp.where(low, 0.0, q)], axis=0).astype(BF16)
            s_scr[slot, :, :nk] = _dot_nt(lhs, rows2(k_ref, u).astype(BF16))

        def sm(u, slot, starts=starts, use_prev=use_prev, nk=nk):
            bias = bias_scr[jnp.minimum(starts(u)[2], 1)] if use_prev else bias_cur
            _softmax_stage(s_scr.at[slot, :, :nk], bias, None, p_scr.at[slot, :, :nk], t_scr.at[slot])

        def pv(u, slot, g=g, dil=dil, v_ref=v_ref, starts=starts, rows2=rows2, nk=nk):
            o2 = _dot(p_scr[slot, :, :nk], rows2(v_ref, u).astype(BF16)) * t_scr[slot, :, :LANES]
            l2 = t_scr[slot, :, LANES:]
            o = jnp.where(low, o2[:BLOCK], o2[BLOCK:])
            lse = jnp.where(low, l2[:BLOCK], l2[BLOCK:])
            start_c = starts(u)[0]
            if dil == 1:
                idx = pl.ds(_multiple_of(start_c, BLOCK), BLOCK)
            else:
                idx = pl.ds(start_c, BLOCK, stride=dil)
            o_scr[g, idx, :] = o
            l_scr[g, idx, :] = lse

        _pipeline(nblk * dil, qk, sm, pv)

    chunk = 256

    def comb(c, carry):
        r0 = pl.multiple_of(c * chunk, chunk)
        l0, l1, l2 = (l_scr[g, pl.ds(r0, chunk), :] for g in range(B_GROUPS))
        m = jnp.maximum(jnp.maximum(l0, l1), l2)
        w0, w1, w2 = jnp.exp(l0 - m), jnp.exp(l1 - m), jnp.exp(l2 - m)
        inv = 1.0 / (w0 + w1 + w2)
        acc = (o_scr[0, pl.ds(r0, chunk), :] * (w0 * inv) + o_scr[1, pl.ds(r0, chunk), :] * (w1 * inv)
               + o_scr[2, pl.ds(r0, chunk), :] * (w2 * inv))
        o_ref[pl.ds(r0, chunk), :] = acc.astype(o_ref.dtype)
        return carry

    lax.fori_loop(0, s // chunk, comb, 0)


def _attn_b_call(qb, kvs, n, s):
    pairs = B_GW // LANES
    in_specs = [pl.BlockSpec((s, LANES), lambda i, h, g=g: (i, g * pairs + h)) for g in range(B_GROUPS)]
    args = [qb, qb, qb]
    for kv in kvs:
        in_specs.append(pl.BlockSpec((s, LANES), lambda i, h: (i, h)))
        in_specs.append(pl.BlockSpec((s, LANES), lambda i, h: (i, pairs + h)))
        args += [kv, kv]
    return pl.pallas_call(
        functools.partial(_attn_b_body, s=s),
        name="attn_b",
        grid=(n, pairs),
        in_specs=in_specs,
        out_specs=pl.BlockSpec((s, LANES), lambda i, h: (i, h)),
        out_shape=jax.ShapeDtypeStruct((n * s, B_GW), BF16),
        scratch_shapes=[pltpu.VMEM((B_GROUPS, s, LANES), F32), pltpu.VMEM((B_GROUPS, s, LANES), F32),
                        pltpu.VMEM((2, UNIT_ROWS, 2 * BLOCK), F32), pltpu.VMEM((2, UNIT_ROWS, 2 * BLOCK), BF16),
                        pltpu.VMEM((2, UNIT_ROWS, 2 * LANES), F32), pltpu.VMEM((2, UNIT_ROWS, 2 * BLOCK), F32)],
        compiler_params=_cparams(("parallel", "arbitrary"), VMEM_LIMIT),
    )(*args)


def _shift_insert(old_ref, new_ref, out_ref, w):
    ncol = w // LANES
    keep = lax.broadcasted_iota(jnp.int32, (BLOCK, LANES), 1) < LANES - SAMPLE_T

    def strip(k, carry):
        r0 = pl.multiple_of(k * BLOCK, BLOCK)
        x = old_ref[pl.ds(r0, BLOCK), :]
        pad = jnp.concatenate([jnp.zeros((BLOCK - SAMPLE_T, BLOCK), F32), new_ref[:, pl.ds(r0, BLOCK)]], axis=0)
        nxt = pad.T
        for j in reversed(range(ncol)):
            cur = pltpu.roll(x[:, j * LANES:(j + 1) * LANES], LANES - SAMPLE_T, 1)
            out_ref[pl.ds(r0, BLOCK), j * LANES:(j + 1) * LANES] = jnp.where(keep, cur, nxt)
            nxt = cur
        return carry

    lax.fori_loop(0, old_ref.shape[0] // BLOCK, strip, 0)


def _window_scores(lhs, buf_ref, old_ref, krow, w, dil, min_old):
    r = lhs.shape[0]
    t_main = lax.broadcasted_iota(jnp.int32, (r, w), 0) & (SAMPLE_T - 1)
    delta = (w - SAMPLE_T) + t_main - lax.broadcasted_iota(jnp.int32, (r, w), 1)
    valid_main = jnp.logical_and(delta >= 0, (delta & (dil - 1)) == 0)
    t_old = lax.broadcasted_iota(jnp.int32, (r, LANES), 0) & (SAMPLE_T - 1)
    c_old = lax.broadcasted_iota(jnp.int32, (r, LANES), 1)
    valid_old = jnp.logical_and(jnp.logical_and(c_old < SAMPLE_T, c_old >= t_old + min_old),
                                ((w + t_old - c_old) & (dil - 1)) == 0)
    kt = buf_ref[krow:krow + LANES, :].astype(BF16)
    kx = old_ref[krow:krow + LANES, 0:LANES].astype(BF16)
    return jnp.where(valid_main, _dot(lhs, kt), -jnp.inf), jnp.where(valid_old, _dot(lhs, kx), -jnp.inf)


def _window_softmax(s, sx, sink):
    m = jnp.maximum(jnp.max(s, axis=1, keepdims=True), jnp.max(sx, axis=1, keepdims=True))
    if sink is not None:
        m = jnp.maximum(m, sink)
    p = jnp.exp(s - m)
    px = jnp.exp(sx - m)
    denom = jnp.sum(p, axis=1, keepdims=True) + jnp.sum(px, axis=1, keepdims=True)
    if sink is not None:
        denom = denom + jnp.exp(sink - m)
    inv = 1.0 / denom
    return (p * inv).astype(BF16), (px * inv).astype(BF16), m + jnp.log(denom)


def _window_values(p, px, buf_ref, old_ref, vrow):
    vt = buf_ref[vrow:vrow + LANES, :].astype(BF16)
    vx = old_ref[vrow:vrow + LANES, 0:LANES].astype(BF16)
    return _dot_nt(p, vt) + _dot_nt(px, vx)


def _sample_body(sink_ref, qa_ref, kva_ref, qb_ref, kvb1_ref, kvb2_ref, kvb3_ref, ca_ref, cb1_ref, cb2_ref, cb3_ref,
                 oa_ref, ob_ref, na_ref, nb1_ref, nb2_ref, nb3_ref):
    low = lax.broadcasted_iota(jnp.int32, (SAMPLE_T, LANES), 1) < HEAD_DIM
    new_refs = (kvb1_ref, kvb2_ref, kvb3_ref)
    old_refs = (cb1_ref, cb2_ref, cb3_ref)
    buf_refs = (nb1_ref, nb2_ref, nb3_ref)
    _shift_insert(ca_ref, kva_ref, na_ref, A_WINDOW)
    for g, (win, _) in enumerate(B_PATTERNS):
        _shift_insert(old_refs[g], new_refs[g], buf_refs[g], win)

    units = []
    qa = qa_ref[...]
    heads_per_pair = 2 * (A_HEADS // A_KV_HEADS)
    for c in range(A_KV_HEADS // 2):
        blocks, sinks = [], []
        for jj in range(heads_per_pair):
            j = c * heads_per_pair + jj
            want_low = jj < heads_per_pair // 2
            q = qa[:, (j // 2) * LANES:(j // 2 + 1) * LANES]
            if (j % 2 == 0) != want_low:
                q = pltpu.roll(q, HEAD_DIM, 1)
            blocks.append(jnp.where(low if want_low else jnp.logical_not(low), q, 0.0))
            sinks.append(jnp.full((SAMPLE_T, 1), sink_ref[j], F32))
        units.append((jnp.concatenate(blocks, axis=0).astype(BF16), na_ref, ca_ref, c * LANES, A_KV_W + c * LANES,
                      A_WINDOW, 1, 1, jnp.concatenate(sinks, axis=0)))
    qb = qb_ref[...]
    pairs = B_GW // LANES
    for g, (win, dil) in enumerate(B_PATTERNS):
        for c in range(pairs):
            q = qb[:, g * B_GW + c * LANES:g * B_GW + (c + 1) * LANES]
            lhs = jnp.concatenate([jnp.where(low, q, 0.0), jnp.where(low, 0.0, q)], axis=0).astype(BF16)
            units.append((lhs, buf_refs[g], old_refs[g], c * LANES, B_GW + c * LANES, win, dil, 0, None))

    scores = [_window_scores(lhs, buf, old, krow, w, dil, min_old) for lhs, buf, old, krow, _, w, dil, min_old, _ in units]
    probs = [_window_softmax(s, sx, u[8]) for (s, sx), u in zip(scores, units)]
    results = [(_window_values(p, px, u[1], u[2], u[4]), lse) for (p, px, lse), u in zip(probs, units)]

    outs = {}
    for c in range(A_KV_HEADS // 2):
        o = results[c][0]
        for jj in range(heads_per_pair):
            j = c * heads_per_pair + jj
            ob = o[jj * SAMPLE_T:(jj + 1) * SAMPLE_T]
            if (j % 2 == 0) != (jj < heads_per_pair // 2):
                ob = pltpu.roll(ob, HEAD_DIM, 1)
            outs[j] = ob
    for c in range(A_HEADS // 2):
        oa_ref[:, c * LANES:(c + 1) * LANES] = jnp.where(low, outs[2 * c], outs[2 * c + 1])

    o_g, l_g = [], []
    for g in range(B_GROUPS):
        o_cols, l_cols = [], []
        for c in range(pairs):
            o, lse = results[A_KV_HEADS // 2 + g * pairs + c]
            lse = jnp.broadcast_to(lse, (2 * SAMPLE_T, LANES))
            o_cols.append(jnp.where(low, o[:SAMPLE_T], o[SAMPLE_T:]))
            l_cols.append(jnp.where(low, lse[:SAMPLE_T], lse[SAMPLE_T:]))
        o_g.append(jnp.concatenate(o_cols, axis=1))
        l_g.append(jnp.concatenate(l_cols, axis=1))
    m = jnp.maximum(jnp.maximum(l_g[0], l_g[1]), l_g[2])
    w0, w1, w2 = jnp.exp(l_g[0] - m), jnp.exp(l_g[1] - m), jnp.exp(l_g[2] - m)
    inv = 1.0 / (w0 + w1 + w2)
    ob_ref[...] = o_g[0] * (w0 * inv) + o_g[1] * (w1 * inv) + o_g[2] * (w2 * inv)


def _sample_call(sinks, qa, kva, qb, kvbs, caches, ns):
    def rows(width):
        return pl.BlockSpec((SAMPLE_T, width), lambda i: (i, 0))

    def cache(c):
        return pl.BlockSpec((None,) + c.shape[1:], lambda i: (i, 0, 0))

    news = [qa, kva, qb, *kvbs]
    in_specs = [pl.BlockSpec(memory_space=pltpu.SMEM)] + [rows(a.shape[1]) for a in news] + [cache(c) for c in caches]
    out_shape = ([jax.ShapeDtypeStruct((ns * SAMPLE_T, A_Q_W), F32), jax.ShapeDtypeStruct((ns * SAMPLE_T, B_GW), F32)]
                 + [jax.ShapeDtypeStruct(c.shape, F32) for c in caches])
    out_specs = [rows(A_Q_W), rows(B_GW)] + [cache(c) for c in caches]
    return pl.pallas_call(
        _sample_body,
        name="sample_attn",
        grid=(ns,),
        in_specs=in_specs,
        out_specs=out_specs,
        out_shape=out_shape,
        compiler_params=_cparams(("parallel",), VMEM_LIMIT),
    )(sinks, *news, *caches)


def _tail_transpose_body(x_ref, o_ref):
    o_ref[...] = x_ref[...].T


def _tail_transpose_call(name, kv, n, s, win):
    c = kv.shape[1]
    tt = min(win, 2 * LANES)
    per_seq, first = s // tt, (s - win) // tt
    return pl.pallas_call(
        _tail_transpose_body,
        name=name,
        grid=(n, win // tt),
        in_specs=[pl.BlockSpec((tt, c), lambda i, j: (i * per_seq + first + j, 0))],
        out_specs=pl.BlockSpec((None, c, tt), lambda i, j: (i, 0, j)),
        out_shape=jax.ShapeDtypeStruct((n, c, win), F32),
        compiler_params=_cparams(("parallel", "parallel")),
    )(kv)


def _merge_body(x_ref, g_ref, oa_ref, ob_ref, wga_ref, wgb_ref, wba_ref, wbb_ref, o_ref, h_scr):
    @pl.when(pl.program_id(1) == 0)
    def _():
        h_scr[...] = _rmsnorm(x_ref[...], g_ref[...]).astype(BF16)

    h = h_scr[...]
    ga = _dot(h, wga_ref[...])
    gb = _dot(h, wgb_ref[...])
    ya = _dot(oa_ref[...].astype(BF16), wba_ref[...])
    yb = _dot(ob_ref[...].astype(BF16), wbb_ref[...])
    o_ref[...] = (jax.nn.sigmoid(ga) * ya + jax.nn.sigmoid(gb) * yb).astype(o_ref.dtype)


def _merge_call(x2d, g, oa, ob, w_gates, w_ba, w_bb, tm, tn):
    t = x2d.shape[0]
    ncol = D_MODEL // tn
    return pl.pallas_call(
        _merge_body,
        name="merge",
        grid=(t // tm, ncol),
        in_specs=[
            pl.BlockSpec((tm, D_MODEL), lambda i, j: (i, 0)),
            pl.BlockSpec((1, D_MODEL), lambda i, j: (0, 0)),
            pl.BlockSpec((tm, A_Q_W), lambda i, j: (i, 0)),
            pl.BlockSpec((tm, B_GW), lambda i, j: (i, 0)),
            pl.BlockSpec((D_MODEL, tn), lambda i, j: (0, j)),
            pl.BlockSpec((D_MODEL, tn), lambda i, j: (0, ncol + j)),
            pl.BlockSpec((A_Q_W, tn), lambda i, j: (0, j)),
            pl.BlockSpec((B_GW, tn), lambda i, j: (0, j)),
        ],
        out_specs=pl.BlockSpec((tm, tn), lambda i, j: (i, j)),
        out_shape=jax.ShapeDtypeStruct((t, D_MODEL), BF16),
        scratch_shapes=[pltpu.VMEM((tm, D_MODEL), BF16)],
        compiler_params=_cparams(("parallel", "arbitrary"), VMEM_LIMIT),
    )(x2d, g, oa, ob, w_gates, w_gates, w_ba, w_bb)


def _route(logits):
    tm = logits.shape[0]
    lane = lax.broadcasted_iota(jnp.int32, (tm, LANES), 1)
    neg = -jnp.inf
    big = LANES

    def first_where(cond):
        return jnp.min(jnp.where(cond, lane, big), axis=1, keepdims=True)

    gl = jnp.where(lane < N_EXPERT_GROUPS, logits, neg)
    gmax = jnp.max(gl, axis=1, keepdims=True)
    gidx = first_where(gl == gmax)
    g_w = 1.0 / jnp.sum(jnp.exp(gl - gmax), axis=1, keepdims=True)
    lo = N_EXPERT_GROUPS + gidx * EXPERTS_PER_GROUP
    in_grp = jnp.logical_and(lane >= lo, lane < lo + EXPERTS_PER_GROUP)
    el = jnp.where(in_grp, logits, neg)
    ep = jnp.exp(el - jnp.max(el, axis=1, keepdims=True))
    prob = ep / jnp.sum(ep, axis=1, keepdims=True)
    prob = jnp.where(in_grp, prob, -1.0)
    p1 = jnp.max(prob, axis=1, keepdims=True)
    i1 = first_where(prob == p1)
    prob2 = jnp.where(lane == i1, -1.0, prob)
    p2 = jnp.max(prob2, axis=1, keepdims=True)
    i2 = first_where(prob2 == p2)
    tot = p1 + p2
    c1 = g_w * (p1 / tot)
    c2 = g_w * (p2 / tot)
    e1 = (i1 - N_EXPERT_GROUPS).astype(F32)
    e2 = (i2 - N_EXPERT_GROUPS).astype(F32)
    return e1, e2, c1, c2


def _outproj_body(mp_ref, xp_ref, ms_ref, xs_ref, wo_ref, g_ref, wrh_ref, wrl_ref, br_ref, x1_ref, route_ref, *, n_prompt):
    def compute(m_ref, x_ref):
        x1 = x_ref[...] + _dot(m_ref[...], wo_ref[...])
        x1_ref[...] = x1
        hi, lo = _split2(_rmsnorm(x1, g_ref[...]))
        logits = _dot(hi, wrh_ref[...]) + _dot(hi, wrl_ref[...]) + _dot(lo, wrh_ref[...]) + br_ref[...]
        e1, e2, c1, c2 = _route(logits)
        lane = lax.broadcasted_iota(jnp.int32, logits.shape, 1)
        route_ref[...] = jnp.where(lane == 0, e1, jnp.where(lane == 1, e2, jnp.where(lane == 2, c1, jnp.where(lane == 3, c2, 0.0))))

    i = pl.program_id(0)
    pl.when(i < n_prompt)(lambda: compute(mp_ref, xp_ref))
    pl.when(i >= n_prompt)(lambda: compute(ms_ref, xs_ref))


def _outproj_call(merged_p, xp, merged_s, xs, w_o, g, wr_hi, wr_lo, br, tm):
    n_p, n_s = xp.shape[0] // tm, xs.shape[0] // tm
    t_all = xp.shape[0] + xs.shape[0]

    def p_map(i):
        return (jnp.minimum(i, n_p - 1), 0)

    def s_map(i):
        return (jnp.maximum(i - n_p, 0), 0)

    def const(shape):
        return pl.BlockSpec(shape, lambda i: (0, 0))

    return pl.pallas_call(
        functools.partial(_outproj_body, n_prompt=n_p),
        name="outproj",
        grid=(n_p + n_s,),
        in_specs=[
            pl.BlockSpec((tm, D_MODEL), p_map), pl.BlockSpec((tm, D_MODEL), p_map),
            pl.BlockSpec((tm, D_MODEL), s_map), pl.BlockSpec((tm, D_MODEL), s_map),
            const((D_MODEL, D_MODEL)), const((1, D_MODEL)), const((D_MODEL, LANES)), const((D_MODEL, LANES)),
            const((1, LANES)),
        ],
        out_specs=[pl.BlockSpec((tm, D_MODEL), lambda i: (i, 0)), pl.BlockSpec((tm, LANES), lambda i: (i, 0))],
        out_shape=[jax.ShapeDtypeStruct((t_all, D_MODEL), F32), jax.ShapeDtypeStruct((t_all, LANES), F32)],
        compiler_params=_cparams(("arbitrary",), VMEM_LIMIT),
    )(merged_p, xp, merged_s, xs, w_o, g, wr_hi, wr_lo, br)


def _pack_bf16_pairs(x):
    n = x.shape[1] // 2
    bits = pltpu.bitcast(x.astype(BF16).astype(F32), jnp.uint32)
    return (bits[:, n:] & jnp.uint32(0xFFFF0000)) | (bits[:, :n] >> 16)


def _unpack_bf16_pairs(packed):
    lo = pltpu.bitcast(packed << 16, F32)
    hi = pltpu.bitcast(packed & jnp.uint32(0xFFFF0000), F32)
    return jnp.concatenate([lo, hi], axis=1).astype(BF16)


def _row_copy(src, src_row, dst, dst_row, sem):
    return pltpu.make_async_copy(src.at[pl.ds(src_row, 1), :], dst.at[pl.ds(dst_row, 1), :], sem)


def _dispatch_body(pos_ref, pad_tile_ref, x1_ref, g_ref, xs_hbm, h_scr, sem, *, tile):
    tm = h_scr.shape[0]

    @pl.when(pl.program_id(0) == 0)
    def _():
        h_scr[...] = jnp.zeros_like(h_scr)

        def pad_copy(t, k):
            row = pl.multiple_of(t * tile + k * tm, tm)
            return pltpu.make_async_copy(h_scr, xs_hbm.at[pl.ds(row, tm), :], sem)

        n_used = pad_tile_ref[N_EXPERTS]
        for phase in ("start", "wait"):
            def expert_pad(e, carry, phase=phase):
                @pl.when(pad_tile_ref[e] >= 0)
                def _():
                    for k in range(tile // tm):
                        getattr(pad_copy(pad_tile_ref[e], k), phase)()
                return carry
            lax.fori_loop(0, N_EXPERTS, expert_pad, 0)

            def unused_tile(t, carry, phase=phase):
                @pl.when(t >= n_used)
                def _():
                    for k in range(tile // tm):
                        getattr(pad_copy(t, k), phase)()
                return carry
            lax.fori_loop(0, xs_hbm.shape[0] // tile, unused_tile, 0)

    h_scr[...] = _pack_bf16_pairs(_rmsnorm(x1_ref[...], g_ref[...]))

    def issue(j, carry):
        _row_copy(h_scr, j, xs_hbm, pos_ref[0, 0, 2 * j], sem).start()
        _row_copy(h_scr, j, xs_hbm, pos_ref[0, 0, 2 * j + 1], sem).start()
        return carry

    lax.fori_loop(0, tm, issue, 0, unroll=8)

    def drain(j, carry):
        _row_copy(h_scr, j, xs_hbm, pos_ref[0, 0, 2 * j], sem).wait()
        _row_copy(h_scr, j, xs_hbm, pos_ref[0, 0, 2 * j + 1], sem).wait()
        return carry

    lax.fori_loop(0, tm, drain, 0, unroll=8)


def _dispatch_call(pos3, pad_tile, x1, g, rows, tm, tile):
    t = x1.shape[0]
    assert tile % tm == 0
    return pl.pallas_call(
        functools.partial(_dispatch_body, tile=tile),
        name="dispatch",
        grid=(t // tm,),
        in_specs=[
            pl.BlockSpec((1, 1, 2 * tm), lambda i: (i, 0, 0), memory_space=pltpu.SMEM),
            pl.BlockSpec(memory_space=pltpu.SMEM),
            pl.BlockSpec((tm, D_MODEL), lambda i: (i, 0)),
            pl.BlockSpec((1, D_MODEL), lambda i: (0, 0)),
        ],
        out_specs=pl.BlockSpec(memory_space=pl.ANY),
        out_shape=jax.ShapeDtypeStruct((rows, D_MODEL // 2), jnp.uint32),
        scratch_shapes=[pltpu.VMEM((tm, D_MODEL // 2), jnp.uint32), pltpu.SemaphoreType.DMA(())],
        compiler_params=_cparams(("arbitrary",)),
    )(pos3, pad_tile, x1, g)


def _moe_body(te_ref, tv_ref, nu_ref, xs_ref, wg_ref, wu_ref, wd_ref, ys_ref):
    i = pl.program_id(0)

    @pl.when(i < nu_ref[0])
    def _():
        row = lax.broadcasted_iota(jnp.int32, xs_ref.shape, 0)
        x = _unpack_bf16_pairs(jnp.where(row < tv_ref[i], xs_ref[...], jnp.uint32(0)))
        a = jax.nn.silu(_dot(x, wg_ref[0])) * _dot(x, wu_ref[0])
        ys_ref[...] = _dot(a.astype(BF16), wd_ref[0])

    @pl.when(i >= nu_ref[0])
    def _():
        ys_ref[...] = jnp.zeros_like(ys_ref)


def _moe_call(tile_expert, tile_valid, n_used, xs, w_gate, w_up, w_down, tm):
    rows = xs.shape[0]

    def row_map(i, te, tv, nu):
        return (jnp.minimum(i, nu[0] - 1), 0)

    def out_map(i, te, tv, nu):
        return (i, 0)

    def w_map(i, te, tv, nu):
        return (te[i], 0, 0)

    return pl.pallas_call(
        _moe_body,
        name="moe",
        grid_spec=pltpu.PrefetchScalarGridSpec(
            num_scalar_prefetch=3,
            grid=(rows // tm,),
            in_specs=[
                pl.BlockSpec((tm, D_MODEL // 2), row_map),
                pl.BlockSpec((1, D_MODEL, D_EXPERT), w_map),
                pl.BlockSpec((1, D_MODEL, D_EXPERT), w_map),
                pl.BlockSpec((1, D_EXPERT, D_MODEL), w_map),
            ],
            out_specs=pl.BlockSpec((tm, D_MODEL), out_map),
        ),
        out_shape=jax.ShapeDtypeStruct((rows, D_MODEL), F32),
        compiler_params=_cparams(("arbitrary",), VMEM_LIMIT),
    )(tile_expert, tile_valid, n_used, xs, w_gate, w_up, w_down)


def _combine_body(pos_ref, x1_ref, route_ref, ys_hbm, y_ref, buf0, buf1, sem):
    tm = buf0.shape[0]

    def issue(j, carry):
        _row_copy(ys_hbm, pos_ref[0, 0, 2 * j], buf0, j, sem).start()
        _row_copy(ys_hbm, pos_ref[0, 0, 2 * j + 1], buf1, j, sem).start()
        return carry

    lax.fori_loop(0, tm, issue, 0, unroll=8)

    def drain(j, carry):
        _row_copy(ys_hbm, pos_ref[0, 0, 2 * j], buf0, j, sem).wait()
        _row_copy(ys_hbm, pos_ref[0, 0, 2 * j + 1], buf1, j, sem).wait()
        return carry

    lax.fori_loop(0, tm, drain, 0, unroll=8)
    route = route_ref[...]
    y_ref[...] = x1_ref[...] + (route[:, 2:3] * buf0[...] + route[:, 3:4] * buf1[...])


def _combine_call(name, pos3, x1_all, route, ys, t, row0, tm):
    off = row0 // tm
    return pl.pallas_call(
        _combine_body,
        name=name,
        grid=(t // tm,),
        in_specs=[
            pl.BlockSpec((1, 1, 2 * tm), lambda i: (i + off, 0, 0), memory_space=pltpu.SMEM),
            pl.BlockSpec((tm, D_MODEL), lambda i: (i + off, 0)),
            pl.BlockSpec((tm, LANES), lambda i: (i + off, 0)),
            pl.BlockSpec(memory_space=pl.ANY),
        ],
        out_specs=pl.BlockSpec((tm, D_MODEL), lambda i: (i, 0)),
        out_shape=jax.ShapeDtypeStruct((t, D_MODEL), F32),
        scratch_shapes=[pltpu.VMEM((tm, D_MODEL), F32), pltpu.VMEM((tm, D_MODEL), F32), pltpu.SemaphoreType.DMA(())],
        compiler_params=_cparams(("arbitrary",)),
    )(pos3, x1_all, route, ys)


def _routing_offsets(route, tm, n_tiles):
    e = route[:, :2].astype(jnp.int32)
    onehot = jnp.sum((e[:, :, None] == jnp.arange(N_EXPERTS, dtype=jnp.int32)[None, None, :]).astype(jnp.int32), axis=1)
    csum = jnp.cumsum(onehot, axis=0)
    rank = csum - onehot
    counts = csum[-1]
    tiles_e = (counts + tm - 1) // tm
    tile_end = jnp.cumsum(tiles_e)
    tile_start = tile_end - tiles_e
    pos = tile_start[e] * tm + jnp.take_along_axis(rank, e, axis=1)
    n_used = tile_end[-1]
    tile_id = jnp.minimum(jnp.arange(n_tiles, dtype=jnp.int32), n_used - 1)
    tile_expert = jnp.sum((tile_id[:, None] >= tile_end[None, :]).astype(jnp.int32), axis=1)
    tile_valid = jnp.clip(counts[tile_expert] - (tile_id - tile_start[tile_expert]) * tm, 0, tm)
    pad_tile = jnp.concatenate([jnp.where(tiles_e > 0, tile_end - 1, -1), n_used.reshape(1)])
    i32 = jnp.int32
    return (pos.astype(i32), tile_expert.astype(i32), tile_valid.astype(i32), n_used.reshape(1).astype(i32),
            pad_tile.astype(i32))


def _pick_tile(t, pref):
    tm = min(pref, t)
    assert t % tm == 0, (t, tm)
    return tm


def kernel(x_prompt, x_sample, cache_a_kv, cache_b1_kv, cache_b2_kv, cache_b3_kv, g_attn_norm, w_in, q_norm_a, k_norm_a, q_norm_b, k_norm_b, sinks_a, w_branch_a, w_branch_b, w_out, g_ffn_norm, w_group_router, b_group_router, w_expert_router, b_expert_router, w_expert_gate, w_expert_up, w_expert_down):
    n, s, _ = x_prompt.shape
    ns, ts, _ = x_sample.shape
    assert ts == SAMPLE_T and s == B_PATTERNS[-1][0] and x_prompt.shape[2] == D_MODEL
    assert g_attn_norm.shape[0] == 1, "single layer"
    tp, tsmp = n * s, ns * ts
    t_all = tp + tsmp
    xp = x_prompt.reshape(tp, D_MODEL)
    xs_in = x_sample.reshape(tsmp, D_MODEL)

    w = w_in[0]
    o_qa, o_ka, o_va = 0, A_Q_W, A_Q_W + A_KV_W
    o_qb = o_va + A_KV_W
    o_kb, o_vb = o_qb + B_W, o_qb + 2 * B_W
    o_g = o_vb + B_W
    cols_a = [w[:, o_qa:o_ka], w[:, o_ka:o_va], w[:, o_va:o_qb]]
    cols_b = []
    for g in range(B_GROUPS):
        cols_b += [w[:, o_kb + g * B_GW:o_kb + (g + 1) * B_GW], w[:, o_vb + g * B_GW:o_vb + (g + 1) * B_GW]]
    w_pa = jnp.concatenate(cols_a, axis=1).astype(BF16)
    w_pq = w[:, o_qb:o_kb].astype(BF16)
    w_pkv = jnp.concatenate(cols_b, axis=1).astype(BF16)
    w_gates = w[:, o_g:].astype(BF16)
    rep = LANES // HEAD_DIM
    gains = jnp.stack([jnp.tile(v[0], rep) for v in (q_norm_a, k_norm_a, q_norm_b, k_norm_b)]).astype(F32)
    g_attn = g_attn_norm.astype(F32)
    g_ffn = g_ffn_norm.astype(F32)
    nq, nk = A_Q_W // LANES, A_KV_W // LANES
    kinds_a = (0,) * nq + (1,) * nk + (-1,) * nk
    kinds_q = (2,) * (B_W // LANES)
    kinds_kv = ((3,) * (B_GW // LANES) + (-1,) * (B_GW // LANES)) * B_GROUPS

    def project(x2d, pos, tm):
        tables = _rope_tables(pos)
        qa, kva = _proj_call("proj_a", x2d, g_attn, w_pa, gains, tables, kinds_a, (A_Q_W, 2 * A_KV_W), tm)
        (qb,) = _proj_call("proj_qb", x2d, g_attn, w_pq, gains, tables, kinds_q, (B_W,), tm)
        kvb = _proj_call("proj_kvb", x2d, g_attn, w_pkv, gains, tables, kinds_kv, (2 * B_GW,) * B_GROUPS, tm)
        return qa, kva, qb, kvb

    tm_p = _pick_tile(tp, ROW_TILE)
    qa, kva, qb, kvb = project(xp, jnp.arange(s, dtype=jnp.int32), tm_p)
    sinks = sinks_a[0].astype(F32)
    oa_p = _attn_a_call(qa, kva, sinks, n, s)
    ob_p = _attn_b_call(qb, kvb, n, s)

    def to_feature_major(c, heads):
        return jnp.transpose(c, (0, 2, 3, 4, 1)).reshape(c.shape[0], 2 * heads * HEAD_DIM, c.shape[1])

    def from_feature_major(c, heads):
        return jnp.transpose(c.reshape(c.shape[0], 2, heads, HEAD_DIM, c.shape[2]), (0, 4, 1, 2, 3))[None]

    new_a_p = from_feature_major(_tail_transpose_call("tail_a", kva, n, s, min(A_WINDOW, s)), A_KV_HEADS)
    new_b_p = [from_feature_major(_tail_transpose_call("tail_b%d" % g, kvb[g], n, s, min(win, s)), B_HPG)
               for g, (win, _) in enumerate(B_PATTERNS)]

    tm_s = _pick_tile(tsmp, ROW_TILE)
    pos_s = PAST_LEN + (jnp.arange(tm_s, dtype=jnp.int32) % ts)
    qa_s, kva_s, qb_s, kvb_s = project(xs_in, pos_s, tm_s)
    caches = [to_feature_major(cache_a_kv[0], A_KV_HEADS)]
    caches += [to_feature_major(c[0], B_HPG) for c in (cache_b1_kv, cache_b2_kv, cache_b3_kv)]
    oa_s, ob_s, na, nb1, nb2, nb3 = _sample_call(sinks, qa_s, kva_s, qb_s, kvb_s, caches, ns)
    new_a_s = from_feature_major(na, A_KV_HEADS)
    new_b_s = [from_feature_major(c, B_HPG) for c in (nb1, nb2, nb3)]

    w_ba = w_branch_a[0].astype(BF16)
    w_bb = w_branch_b[0].astype(BF16)
    w_o = w_out[0].astype(BF16)
    wr = jnp.zeros((D_MODEL, LANES), F32)
    wr = wr.at[:, :N_EXPERT_GROUPS].set(w_group_router[0]).at[:, N_EXPERT_GROUPS:N_EXPERT_GROUPS + N_EXPERTS].set(w_expert_router[0])
    wr_hi = wr.astype(BF16)
    wr_lo = (wr - wr_hi.astype(F32)).astype(BF16)
    br = jnp.zeros((1, LANES), F32)
    br = br.at[0, :N_EXPERT_GROUPS].set(b_group_router[0]).at[0, N_EXPERT_GROUPS:N_EXPERT_GROUPS + N_EXPERTS].set(b_expert_router[0])

    tn = 512
    tm_mp, tm_ms = _pick_tile(tp, 512), _pick_tile(tsmp, 512)
    merged_p = _merge_call(xp, g_attn, oa_p, ob_p, w_gates, w_ba, w_bb, tm_mp, tn)
    merged_s = _merge_call(xs_in, g_attn, oa_s, ob_s, w_gates, w_ba, w_bb, tm_ms, tn)
    tm_o = _pick_tile(tsmp, ROW_TILE)
    assert tp % tm_o == 0
    x1_all, route = _outproj_call(merged_p, xp, merged_s, xs_in, w_o, g_ffn, wr_hi, wr_lo, br, tm_o)

    tm_e = MOE_TILE
    n_tiles = (2 * t_all + N_EXPERTS * (tm_e - 1)) // tm_e + 1
    pos, tile_expert, tile_valid, n_used, pad_tile = _routing_offsets(route, tm_e, n_tiles)
    tm_d = _pick_tile(tsmp, ROW_TILE)
    pos3 = pos.reshape(t_all // tm_d, 1, 2 * tm_d)
    xs_sorted = _dispatch_call(pos3, pad_tile, x1_all, g_ffn, n_tiles * tm_e, tm_d, tm_e)
    ys = _moe_call(tile_expert, tile_valid, n_used, xs_sorted, w_expert_gate[0].astype(BF16), w_expert_up[0].astype(BF16),
                   w_expert_down[0].astype(BF16), tm_e)
    y_p = _combine_call("combine_p", pos3, x1_all, route, ys, tp, 0, tm_d)
    y_s = _combine_call("combine_s", pos3, x1_all, route, ys, tsmp, tp, tm_d)

    return (y_p.reshape(n, s, D_MODEL), y_s.reshape(ns, ts, D_MODEL),
            new_a_p, new_a_s, new_b_p[0], new_b_s[0], new_b_p[1], new_b_s[1], new_b_p[2], new_b_s[2])
```

```python
import functools

import jax
import jax.numpy as jnp
import numpy as np
from jax import lax
from jax.experimental import pallas as pl
from jax.experimental.pallas import tpu as pltpu

F32 = jnp.float32
BF16 = jnp.bfloat16

D_MODEL = 2048
HEAD_DIM = 64
ROT_DIM = HEAD_DIM // 4
ROPE_THETA = 500000.0
NORM_EPS = 1e-6
BLOCK = 128
PAST_LEN = 16384

A_HEADS = 16
A_KV_HEADS = 4
A_WINDOW = 128
B_PATTERNS = ((128, 1), (512, 4), (2048, 16))
B_GROUPS = 3
B_HPG = 8
A_Q_W = A_HEADS * HEAD_DIM
A_KV_W = A_KV_HEADS * HEAD_DIM
B_GW = B_HPG * HEAD_DIM
B_W = B_GROUPS * B_GW

N_EXPERT_GROUPS = 4
EXPERTS_PER_GROUP = 8
N_EXPERTS = N_EXPERT_GROUPS * EXPERTS_PER_GROUP
D_EXPERT = 512

LANES = 128
SUBLANES = 8
VMEM_LIMIT = 56 * 1024 * 1024
SAMPLE_T = 8

MOE_TILE = 512
ROW_TILE = 256


def _cparams(sem, vmem=None):
    return pltpu.CompilerParams(dimension_semantics=sem, vmem_limit_bytes=vmem)


def _rmsnorm(x, g):
    return x * lax.rsqrt(jnp.mean(x * x, axis=-1, keepdims=True) + NORM_EPS) * g


def _split2(v):
    hi = v.astype(BF16)
    lo = (v - hi.astype(F32)).astype(BF16)
    return hi, lo


def _split3(v):
    hi = v.astype(BF16)
    r = v - hi.astype(F32)
    mid = r.astype(BF16)
    lo = (r - mid.astype(F32)).astype(BF16)
    return hi, mid, lo


def _dot(a, b):
    return jnp.dot(a, b, preferred_element_type=F32)


def _multiple_of(x, m):
    return x if isinstance(x, int) else pl.multiple_of(x, m)


def _dot_nt(a, b):
    return lax.dot_general(a, b, (((1,), (1,)), ((), ())), preferred_element_type=F32)


def _rope_tables(pos):
    half = ROT_DIM // 2
    inv_freq = ROPE_THETA ** (-jnp.arange(half, dtype=F32) / half)
    ang = pos.astype(F32)[:, None] * inv_freq[None, :]
    cos, sin = jnp.cos(ang), jnp.sin(ang)
    p = pos.shape[0]
    ones = jnp.ones((p, HEAD_DIM - ROT_DIM), F32)
    z8 = jnp.zeros((p, half), F32)
    z48 = jnp.zeros((p, HEAD_DIM - ROT_DIM), F32)
    c = jnp.concatenate([cos, cos, ones], axis=1)
    s_up = jnp.concatenate([-sin, z8, z48], axis=1)
    s_dn = jnp.concatenate([z8, sin, z48], axis=1)
    rep = LANES // HEAD_DIM
    return jnp.tile(c, (1, rep)), jnp.tile(s_up, (1, rep)), jnp.tile(s_dn, (1, rep))


PROJ_SEG = 512


def _proj_body(*refs, plan, n_w, transposed):
    x_ref, g_ref = refs[:2]
    w_refs = refs[2:2 + n_w]
    gain_ref, c_ref, su_ref, sd_ref = refs[2 + n_w:6 + n_w]
    out_refs = refs[6 + n_w:]
    h = _rmsnorm(x_ref[...], g_ref[...]).astype(BF16)
    low = lax.broadcasted_iota(jnp.int32, (x_ref.shape[0], LANES), 1) < HEAD_DIM
    for oi, segments in enumerate(plan):
        o_ref = out_refs[oi]
        ocol = 0
        for widx, col0, width, kind in segments:
            z_all = _dot(h, w_refs[widx][:, col0:col0 + width])
            for c in range(width // LANES):
                z = z_all[:, c * LANES:(c + 1) * LANES]
                if kind >= 0:
                    zz = z * z
                    ss = jnp.where(low, jnp.sum(jnp.where(low, zz, 0.0), axis=1, keepdims=True),
                                   jnp.sum(jnp.where(low, 0.0, zz), axis=1, keepdims=True))
                    z = z * lax.rsqrt(ss * (1.0 / HEAD_DIM) + NORM_EPS) * gain_ref[kind:kind + 1, :]
                    z = (z * c_ref[...] + pltpu.roll(z, LANES - ROT_DIM // 2, 1) * su_ref[...]
                         + pltpu.roll(z, ROT_DIM // 2, 1) * sd_ref[...])
                    if kind % 2 == 0:
                        z = z * (HEAD_DIM ** -0.5)
                o_ref[:, ocol:ocol + LANES] = z
                if transposed and oi == len(plan) - 1:
                    out_refs[-1][ocol:ocol + LANES, :] = z.T
                ocol += LANES


def _proj_call(name, x2d, g, w_full, w_blocks, gains, tables, plan, tm, transposed_seq=None):
    t = x2d.shape[0]
    pos_blocks = tables[0].shape[0] // tm
    tab_spec = pl.BlockSpec((tm, LANES), lambda i: (i % pos_blocks, 0))
    widths = [sum(seg[2] for seg in segments) for segments in plan]
    out_specs = [pl.BlockSpec((tm, w_), lambda i: (i, 0)) for w_ in widths]
    out_shape = [jax.ShapeDtypeStruct((t, w_), F32) for w_ in widths]
    if transposed_seq is not None:
        per_seq = transposed_seq // tm
        out_specs.append(pl.BlockSpec((None, widths[-1], tm), lambda i: (i // per_seq, 0, i % per_seq)))
        out_shape.append(jax.ShapeDtypeStruct((t // transposed_seq, widths[-1], transposed_seq), F32))
    return pl.pallas_call(
        functools.partial(_proj_body, plan=plan, n_w=len(w_blocks), transposed=transposed_seq is not None),
        name=name,
        grid=(t // tm,),
        in_specs=[
            pl.BlockSpec((tm, D_MODEL), lambda i: (i, 0)),
            pl.BlockSpec((1, D_MODEL), lambda i: (0, 0)),
            *[pl.BlockSpec((D_MODEL, bw), lambda i, bj=bj: (0, bj)) for bw, bj in w_blocks],
            pl.BlockSpec((4, LANES), lambda i: (0, 0)),
            tab_spec, tab_spec, tab_spec,
        ],
        out_specs=out_specs,
        out_shape=out_shape,
        compiler_params=_cparams(("parallel",), VMEM_LIMIT),
    )(x2d, g, *([w_full] * len(w_blocks)), gains, *tables)


def _band_mask(rows, nkeys, max_dist, has_prev):
    qi = lax.broadcasted_iota(jnp.int32, (rows, nkeys), 0) & (BLOCK - 1)
    kk = lax.broadcasted_iota(jnp.int32, (rows, nkeys), 1)
    if nkeys == BLOCK:
        dist = qi - kk
        return jnp.logical_and(dist >= 0, dist <= max_dist)
    dist = qi + BLOCK - kk
    band = jnp.logical_and(dist >= 0, dist <= max_dist)
    return jnp.logical_and(band, jnp.logical_or(kk >= BLOCK, has_prev))


UNIT_ROWS = 2 * BLOCK


def _band_bias(nkeys, max_dist, has_prev):
    return jnp.where(_band_mask(UNIT_ROWS, nkeys, max_dist, has_prev), 0.0, -jnp.inf).astype(F32)


def _softmax_stage(s_ref, bias, sink, p_ref, t_ref):
    s = s_ref[...] + bias
    m = jnp.max(s, axis=1, keepdims=True)
    if sink is not None:
        m = jnp.maximum(m, sink)
    p = jnp.exp(s - m)
    denom = jnp.sum(p, axis=1, keepdims=True)
    if sink is not None:
        denom = denom + jnp.exp(sink - m)
    p_ref[...] = p.astype(BF16)
    t_ref[:, :LANES] = jnp.broadcast_to(1.0 / denom, (UNIT_ROWS, LANES))
    t_ref[:, LANES:] = jnp.broadcast_to(m + jnp.log(denom), (UNIT_ROWS, LANES))


def _pipeline(n_units, qk, sm, pv):
    qk(0, 0)
    qk(1, 1)
    sm(0, 0)

    def body(j, carry):
        i = 2 * j
        pv(i - 2, 0)
        sm(i - 1, 1)
        qk(i, 0)
        pv(i - 1, 1)
        sm(i, 0)
        qk(i + 1, 1)
        return carry

    lax.fori_loop(1, n_units // 2, body, 0)
    pv(n_units - 2, 0)
    sm(n_units - 1, 1)
    pv(n_units - 1, 1)


def _attn_a_body(sink_ref, q_ref, kv_ref, o_ref, s_scr, p_scr, t_scr, bias_scr, *, s):
    nb = s // BLOCK
    low = lax.broadcasted_iota(jnp.int32, (BLOCK, LANES), 1) < HEAD_DIM
    high = jnp.logical_not(low)
    top = lax.broadcasted_iota(jnp.int32, (UNIT_ROWS, 1), 0) < BLOCK
    bias_scr[0] = _band_bias(2 * BLOCK, A_WINDOW - 1, False)
    bias_scr[1] = _band_bias(2 * BLOCK, A_WINDOW - 1, True)
    pairs = A_KV_HEADS // 2
    per_pair = A_HEADS // A_KV_HEADS
    units = [(c, hp) for c in range(pairs) for hp in range(per_pair)]

    def rows2(b, lanes):
        start_c = _multiple_of(b * BLOCK, BLOCK)
        start_p = _multiple_of(jnp.maximum(b - 1, 0) * BLOCK, BLOCK)
        return jnp.concatenate([kv_ref[pl.ds(start_p, BLOCK), lanes], kv_ref[pl.ds(start_c, BLOCK), lanes]], axis=0)

    def qk(b, slot):
        start_c = _multiple_of(b * BLOCK, BLOCK)
        for c in range(pairs):
            k2 = rows2(b, slice(c * LANES, (c + 1) * LANES)).astype(BF16)
            for hp in range(per_pair):
                col = c * per_pair + hp
                q = q_ref[pl.ds(start_c, BLOCK), col * LANES:(col + 1) * LANES]
                qr = pltpu.roll(q, HEAD_DIM, 1)
                if hp < per_pair // 2:
                    lhs = jnp.concatenate([jnp.where(low, q, 0.0), jnp.where(low, qr, 0.0)], axis=0)
                else:
                    lhs = jnp.concatenate([jnp.where(high, qr, 0.0), jnp.where(high, q, 0.0)], axis=0)
                s_scr[slot, units.index((c, hp))] = _dot_nt(lhs.astype(BF16), k2)

    def sm(b, slot):
        bias = bias_scr[jnp.minimum(b, 1)]
        for ui, (c, hp) in enumerate(units):
            col = c * per_pair + hp
            sink = jnp.where(top, sink_ref[2 * col], sink_ref[2 * col + 1])
            _softmax_stage(s_scr.at[slot, ui], bias, sink, p_scr.at[slot, ui], t_scr.at[slot, ui])

    def pv(b, slot):
        start_c = _multiple_of(b * BLOCK, BLOCK)
        for c in range(pairs):
            v2 = rows2(b, slice(A_KV_W + c * LANES, A_KV_W + (c + 1) * LANES)).astype(BF16)
            for hp in range(per_pair):
                ui = units.index((c, hp))
                col = c * per_pair + hp
                o2 = _dot(p_scr[slot, ui], v2) * t_scr[slot, ui, :, :LANES]
                if hp < per_pair // 2:
                    o = jnp.where(low, o2[:BLOCK], pltpu.roll(o2[BLOCK:], HEAD_DIM, 1))
                else:
                    o = jnp.where(low, pltpu.roll(o2[:BLOCK], HEAD_DIM, 1), o2[BLOCK:])
                o_ref[pl.ds(start_c, BLOCK), col * LANES:(col + 1) * LANES] = o.astype(o_ref.dtype)

    _pipeline(nb, qk, sm, pv)


def _attn_a_call(qa, kva, sinks, n, s):
    n_units = (A_KV_HEADS // 2) * (A_HEADS // A_KV_HEADS)
    return pl.pallas_call(
        functools.partial(_attn_a_body, s=s),
        name="attn_a",
        grid=(n,),
        in_specs=[
            pl.BlockSpec(memory_space=pltpu.SMEM),
            pl.BlockSpec((s, A_Q_W), lambda i: (i, 0)),
            pl.BlockSpec((s, 2 * A_KV_W), lambda i: (i, 0)),
        ],
        out_specs=pl.BlockSpec((s, A_Q_W), lambda i: (i, 0)),
        out_shape=jax.ShapeDtypeStruct((n * s, A_Q_W), BF16),
        scratch_shapes=[pltpu.VMEM((2, n_units, UNIT_ROWS, 2 * BLOCK), F32),
                        pltpu.VMEM((2, n_units, UNIT_ROWS, 2 * BLOCK), BF16),
                        pltpu.VMEM((2, n_units, UNIT_ROWS, 2 * LANES), F32),
                        pltpu.VMEM((2, UNIT_ROWS, 2 * BLOCK), F32)],
        compiler_params=_cparams(("parallel",), VMEM_LIMIT),
    )(sinks, qa, kva)


def _attn_b_body(q1, q2, q3, k1, v1, k2, v2, k3, v3, o_ref, o_scr, l_scr, s_scr, p_scr, t_scr, bias_scr, *, s):
    low = lax.broadcasted_iota(jnp.int32, (BLOCK, LANES), 1) < HEAD_DIM
    bias_scr[0] = _band_bias(2 * BLOCK, BLOCK, False)
    bias_scr[1] = _band_bias(2 * BLOCK, BLOCK, True)
    bias_cur = _band_bias(BLOCK, BLOCK, False)
    for g, ((_, dil), q_ref, k_ref, v_ref) in enumerate(zip(B_PATTERNS, (q1, q2, q3), (k1, k2, k3), (v1, v2, v3))):
        span = BLOCK * dil
        nblk = max(s // span, 1)
        use_prev = nblk > 1
        nk = 2 * BLOCK if use_prev else BLOCK

        def rows(ref, start, dil=dil):
            if dil == 1:
                return ref[pl.ds(_multiple_of(start, BLOCK), BLOCK), :]
            return ref[pl.ds(start, BLOCK, stride=dil), :]

        def starts(u, dil=dil, span=span):
            b = u // dil
            return b * span + u % dil, jnp.maximum(b - 1, 0) * span + u % dil, b

        def rows2(ref, u, rows=rows, starts=starts, use_prev=use_prev):
            start_c, start_p, _ = starts(u)
            if use_prev:
                return jnp.concatenate([rows(ref, start_p), rows(ref, start_c)], axis=0)
            return rows(ref, start_c)

        def qk(u, slot, q_ref=q_ref, k_ref=k_ref, rows=rows, starts=starts, rows2=rows2, nk=nk):
            q = rows(q_ref, starts(u)[0])
            lhs = jnp.concatenate([jnp.where(low, q, 0.0), jnp.where(low, 0.0, q)], axis=0).astype(BF16)
            s_scr[slot, :, :nk] = _dot_nt(lhs, rows2(k_ref, u).astype(BF16))

        def sm(u, slot, starts=starts, use_prev=use_prev, nk=nk):
            bias = bias_scr[jnp.minimum(starts(u)[2], 1)] if use_prev else bias_cur
            _softmax_stage(s_scr.at[slot, :, :nk], bias, None, p_scr.at[slot, :, :nk], t_scr.at[slot])

        def pv(u, slot, g=g, dil=dil, v_ref=v_ref, starts=starts, rows2=rows2, nk=nk):
            o2 = _dot(p_scr[slot, :, :nk], rows2(v_ref, u).astype(BF16)) * t_scr[slot, :, :LANES]
            l2 = t_scr[slot, :, LANES:]
            o = jnp.where(low, o2[:BLOCK], o2[BLOCK:])
            lse = jnp.where(low, l2[:BLOCK], l2[BLOCK:])
            start_c = starts(u)[0]
            if dil == 1:
                idx = pl.ds(_multiple_of(start_c, BLOCK), BLOCK)
            else:
                idx = pl.ds(start_c, BLOCK, stride=dil)
            o_scr[g, idx, :] = o
            l_scr[g, idx, :] = lse

        _pipeline(nblk * dil, qk, sm, pv)

    chunk = 256

    def comb(c, carry):
        r0 = pl.multiple_of(c * chunk, chunk)
        l0, l1, l2 = (l_scr[g, pl.ds(r0, chunk), :] for g in range(B_GROUPS))
        m = jnp.maximum(jnp.maximum(l0, l1), l2)
        w0, w1, w2 = jnp.exp(l0 - m), jnp.exp(l1 - m), jnp.exp(l2 - m)
        inv = 1.0 / (w0 + w1 + w2)
        acc = (o_scr[0, pl.ds(r0, chunk), :] * (w0 * inv) + o_scr[1, pl.ds(r0, chunk), :] * (w1 * inv)
               + o_scr[2, pl.ds(r0, chunk), :] * (w2 * inv))
        o_ref[pl.ds(r0, chunk), :] = acc.astype(o_ref.dtype)
        return carry

    lax.fori_loop(0, s // chunk, comb, 0)


def _attn_b_call(qb, kvs, n, s):
    pairs = B_GW // LANES
    in_specs = [pl.BlockSpec((s, LANES), lambda i, h, g=g: (i, g * pairs + h)) for g in range(B_GROUPS)]
    args = [qb, qb, qb]
    for kv in kvs:
        in_specs.append(pl.BlockSpec((s, LANES), lambda i, h: (i, h)))
        in_specs.append(pl.BlockSpec((s, LANES), lambda i, h: (i, pairs + h)))
        args += [kv, kv]
    return pl.pallas_call(
        functools.partial(_attn_b_body, s=s),
        name="attn_b",
        grid=(n, pairs),
        in_specs=in_specs,
        out_specs=pl.BlockSpec((s, LANES), lambda i, h: (i, h)),
        out_shape=jax.ShapeDtypeStruct((n * s, B_GW), BF16),
        scratch_shapes=[pltpu.VMEM((B_GROUPS, s, LANES), F32), pltpu.VMEM((B_GROUPS, s, LANES), F32),
                        pltpu.VMEM((2, UNIT_ROWS, 2 * BLOCK), F32), pltpu.VMEM((2, UNIT_ROWS, 2 * BLOCK), BF16),
                        pltpu.VMEM((2, UNIT_ROWS, 2 * LANES), F32), pltpu.VMEM((2, UNIT_ROWS, 2 * BLOCK), F32)],
        compiler_params=_cparams(("parallel", "arbitrary"), VMEM_LIMIT),
    )(*args)


def _shift_insert(old_ref, new_ref, out_ref, w):
    ncol = w // LANES
    keep = lax.broadcasted_iota(jnp.int32, (BLOCK, LANES), 1) < LANES - SAMPLE_T

    def strip(k, carry):
        r0 = pl.multiple_of(k * BLOCK, BLOCK)
        x = old_ref[pl.ds(r0, BLOCK), :]
        pad = jnp.concatenate([jnp.zeros((BLOCK - SAMPLE_T, BLOCK), F32), new_ref[:, pl.ds(r0, BLOCK)]], axis=0)
        nxt = pad.T
        for j in reversed(range(ncol)):
            cur = pltpu.roll(x[:, j * LANES:(j + 1) * LANES], LANES - SAMPLE_T, 1)
            out_ref[pl.ds(r0, BLOCK), j * LANES:(j + 1) * LANES] = jnp.where(keep, cur, nxt)
            nxt = cur
        return carry

    lax.fori_loop(0, old_ref.shape[0] // BLOCK, strip, 0)


def _window_scores(lhs, buf_ref, old_ref, krow, w, dil, min_old):
    r = lhs.shape[0]
    t_main = lax.broadcasted_iota(jnp.int32, (r, w), 0) & (SAMPLE_T - 1)
    delta = (w - SAMPLE_T) + t_main - lax.broadcasted_iota(jnp.int32, (r, w), 1)
    valid_main = jnp.logical_and(delta >= 0, (delta & (dil - 1)) == 0)
    t_old = lax.broadcasted_iota(jnp.int32, (r, LANES), 0) & (SAMPLE_T - 1)
    c_old = lax.broadcasted_iota(jnp.int32, (r, LANES), 1)
    valid_old = jnp.logical_and(jnp.logical_and(c_old < SAMPLE_T, c_old >= t_old + min_old),
                                ((w + t_old - c_old) & (dil - 1)) == 0)
    kt = buf_ref[krow:krow + LANES, :].astype(BF16)
    kx = old_ref[krow:krow + LANES, 0:LANES].astype(BF16)
    return jnp.where(valid_main, _dot(lhs, kt), -jnp.inf), jnp.where(valid_old, _dot(lhs, kx), -jnp.inf)


def _window_softmax(s, sx, sink):
    m = jnp.maximum(jnp.max(s, axis=1, keepdims=True), jnp.max(sx, axis=1, keepdims=True))
    if sink is not None:
        m = jnp.maximum(m, sink)
    p = jnp.exp(s - m)
    px = jnp.exp(sx - m)
    denom = jnp.sum(p, axis=1, keepdims=True) + jnp.sum(px, axis=1, keepdims=True)
    if sink is not None:
        denom = denom + jnp.exp(sink - m)
    inv = 1.0 / denom
    return (p * inv).astype(BF16), (px * inv).astype(BF16), m + jnp.log(denom)


def _window_values(p, px, buf_ref, old_ref, vrow):
    vt = buf_ref[vrow:vrow + LANES, :].astype(BF16)
    vx = old_ref[vrow:vrow + LANES, 0:LANES].astype(BF16)
    return _dot_nt(p, vt) + _dot_nt(px, vx)


def _sample_body(sink_ref, qa_ref, kva_ref, qb_ref, kvb1_ref, kvb2_ref, kvb3_ref, ca_ref, cb1_ref, cb2_ref, cb3_ref,
                 oa_ref, ob_ref, na_ref, nb1_ref, nb2_ref, nb3_ref):
    low = lax.broadcasted_iota(jnp.int32, (SAMPLE_T, LANES), 1) < HEAD_DIM
    new_refs = (kvb1_ref, kvb2_ref, kvb3_ref)
    old_refs = (cb1_ref, cb2_ref, cb3_ref)
    buf_refs = (nb1_ref, nb2_ref, nb3_ref)
    _shift_insert(ca_ref, kva_ref, na_ref, A_WINDOW)
    for g, (win, _) in enumerate(B_PATTERNS):
        _shift_insert(old_refs[g], new_refs[g], buf_refs[g], win)

    units = []
    qa = qa_ref[...]
    heads_per_pair = 2 * (A_HEADS // A_KV_HEADS)
    for c in range(A_KV_HEADS // 2):
        blocks, sinks = [], []
        for jj in range(heads_per_pair):
            j = c * heads_per_pair + jj
            want_low = jj < heads_per_pair // 2
            q = qa[:, (j // 2) * LANES:(j // 2 + 1) * LANES]
            if (j % 2 == 0) != want_low:
                q = pltpu.roll(q, HEAD_DIM, 1)
            blocks.append(jnp.where(low if want_low else jnp.logical_not(low), q, 0.0))
            sinks.append(jnp.full((SAMPLE_T, 1), sink_ref[j], F32))
        units.append((jnp.concatenate(blocks, axis=0).astype(BF16), na_ref, ca_ref, c * LANES, A_KV_W + c * LANES,
                      A_WINDOW, 1, 1, jnp.concatenate(sinks, axis=0)))
    qb = qb_ref[...]
    pairs = B_GW // LANES
    for g, (win, dil) in enumerate(B_PATTERNS):
        for c in range(pairs):
            q = qb[:, g * B_GW + c * LANES:g * B_GW + (c + 1) * LANES]
            lhs = jnp.concatenate([jnp.where(low, q, 0.0), jnp.where(low, 0.0, q)], axis=0).astype(BF16)
            units.append((lhs, buf_refs[g], old_refs[g], c * LANES, B_GW + c * LANES, win, dil, 0, None))

    scores = [_window_scores(lhs, buf, old, krow, w, dil, min_old) for lhs, buf, old, krow, _, w, dil, min_old, _ in units]
    probs = [_window_softmax(s, sx, u[8]) for (s, sx), u in zip(scores, units)]
    results = [(_window_values(p, px, u[1], u[2], u[4]), lse) for (p, px, lse), u in zip(probs, units)]

    outs = {}
    for c in range(A_KV_HEADS // 2):
        o = results[c][0]
        for jj in range(heads_per_pair):
            j = c * heads_per_pair + jj
            ob = o[jj * SAMPLE_T:(jj + 1) * SAMPLE_T]
            if (j % 2 == 0) != (jj < heads_per_pair // 2):
                ob = pltpu.roll(ob, HEAD_DIM, 1)
            outs[j] = ob
    for c in range(A_HEADS // 2):
        oa_ref[:, c * LANES:(c + 1) * LANES] = jnp.where(low, outs[2 * c], outs[2 * c + 1])

    o_g, l_g = [], []
    for g in range(B_GROUPS):
        o_cols, l_cols = [], []
        for c in range(pairs):
            o, lse = results[A_KV_HEADS // 2 + g * pairs + c]
            lse = jnp.broadcast_to(lse, (2 * SAMPLE_T, LANES))
            o_cols.append(jnp.where(low, o[:SAMPLE_T], o[SAMPLE_T:]))
            l_cols.append(jnp.where(low, lse[:SAMPLE_T], lse[SAMPLE_T:]))
        o_g.append(jnp.concatenate(o_cols, axis=1))
        l_g.append(jnp.concatenate(l_cols, axis=1))
    m = jnp.maximum(jnp.maximum(l_g[0], l_g[1]), l_g[2])
    w0, w1, w2 = jnp.exp(l_g[0] - m), jnp.exp(l_g[1] - m), jnp.exp(l_g[2] - m)
    inv = 1.0 / (w0 + w1 + w2)
    ob_ref[...] = o_g[0] * (w0 * inv) + o_g[1] * (w1 * inv) + o_g[2] * (w2 * inv)


def _sample_call(sinks, qa, kva, qb, kvbs, caches, ns):
    def rows(width):
        return pl.BlockSpec((SAMPLE_T, width), lambda i: (i, 0))

    def cache(c):
        return pl.BlockSpec((None,) + c.shape[1:], lambda i: (i, 0, 0))

    news = [qa, kva, qb, *kvbs]
    in_specs = [pl.BlockSpec(memory_space=pltpu.SMEM)] + [rows(a.shape[1]) for a in news] + [cache(c) for c in caches]
    out_shape = ([jax.ShapeDtypeStruct((ns * SAMPLE_T, A_Q_W), F32), jax.ShapeDtypeStruct((ns * SAMPLE_T, B_GW), F32)]
                 + [jax.ShapeDtypeStruct(c.shape, F32) for c in caches])
    out_specs = [rows(A_Q_W), rows(B_GW)] + [cache(c) for c in caches]
    return pl.pallas_call(
        _sample_body,
        name="sample_attn",
        grid=(ns,),
        in_specs=in_specs,
        out_specs=out_specs,
        out_shape=out_shape,
        compiler_params=_cparams(("parallel",), VMEM_LIMIT),
    )(sinks, *news, *caches)


def _tail_transpose_body(x_ref, o_ref):
    o_ref[...] = x_ref[...].T


def _tail_transpose_call(name, kv, n, s, win):
    c = kv.shape[1]
    tt = min(win, 2 * LANES)
    per_seq, first = s // tt, (s - win) // tt
    return pl.pallas_call(
        _tail_transpose_body,
        name=name,
        grid=(n, win // tt),
        in_specs=[pl.BlockSpec((tt, c), lambda i, j: (i * per_seq + first + j, 0))],
        out_specs=pl.BlockSpec((None, c, tt), lambda i, j: (i, 0, j)),
        out_shape=jax.ShapeDtypeStruct((n, c, win), F32),
        compiler_params=_cparams(("parallel", "parallel")),
    )(kv)


def _merge_body(x_ref, g_ref, oa_ref, ob_ref, wga_ref, wgb_ref, wba_ref, wbb_ref, o_ref, h_scr):
    @pl.when(pl.program_id(1) == 0)
    def _():
        h_scr[...] = _rmsnorm(x_ref[...], g_ref[...]).astype(BF16)

    h = h_scr[...]
    oa = oa_ref[...].astype(BF16)
    ob = ob_ref[...].astype(BF16)
    half = o_ref.shape[1] // 2
    for c in range(2):
        cols = slice(c * half, (c + 1) * half)
        ga = _dot(h, wga_ref[:, cols])
        gb = _dot(h, wgb_ref[:, cols])
        ya = _dot(oa, wba_ref[:, cols])
        yb = _dot(ob, wbb_ref[:, cols])
        o_ref[:, cols] = (jax.nn.sigmoid(ga) * ya + jax.nn.sigmoid(gb) * yb).astype(o_ref.dtype)


def _merge_call(x2d, g, oa, ob, w_full, gate_col0, w_ba, w_bb, tm, tn):
    t = x2d.shape[0]
    ncol = D_MODEL // tn
    gate0 = gate_col0 // tn
    return pl.pallas_call(
        _merge_body,
        name="merge",
        grid=(t // tm, ncol),
        in_specs=[
            pl.BlockSpec((tm, D_MODEL), lambda i, j: (i, 0)),
            pl.BlockSpec((1, D_MODEL), lambda i, j: (0, 0)),
            pl.BlockSpec((tm, A_Q_W), lambda i, j: (i, 0)),
            pl.BlockSpec((tm, B_GW), lambda i, j: (i, 0)),
            pl.BlockSpec((D_MODEL, tn), lambda i, j: (0, gate0 + j)),
            pl.BlockSpec((D_MODEL, tn), lambda i, j: (0, gate0 + ncol + j)),
            pl.BlockSpec((A_Q_W, tn), lambda i, j: (0, j)),
            pl.BlockSpec((B_GW, tn), lambda i, j: (0, j)),
        ],
        out_specs=pl.BlockSpec((tm, tn), lambda i, j: (i, j)),
        out_shape=jax.ShapeDtypeStruct((t, D_MODEL), BF16),
        scratch_shapes=[pltpu.VMEM((tm, D_MODEL), BF16)],
        compiler_params=_cparams(("parallel", "arbitrary"), VMEM_LIMIT),
    )(x2d, g, oa, ob, w_full, w_full, w_ba, w_bb)


def _route(logits):
    tm = logits.shape[0]
    lane = lax.broadcasted_iota(jnp.int32, (tm, LANES), 1)
    neg = -jnp.inf
    big = LANES

    def first_where(cond):
        return jnp.min(jnp.where(cond, lane, big), axis=1, keepdims=True)

    gl = jnp.where(lane < N_EXPERT_GROUPS, logits, neg)
    gmax = jnp.max(gl, axis=1, keepdims=True)
    gidx = first_where(gl == gmax)
    g_w = 1.0 / jnp.sum(jnp.exp(gl - gmax), axis=1, keepdims=True)
    lo = N_EXPERT_GROUPS + gidx * EXPERTS_PER_GROUP
    in_grp = jnp.logical_and(lane >= lo, lane < lo + EXPERTS_PER_GROUP)
    el = jnp.where(in_grp, logits, neg)
    ep = jnp.exp(el - jnp.max(el, axis=1, keepdims=True))
    prob = ep / jnp.sum(ep, axis=1, keepdims=True)
    prob = jnp.where(in_grp, prob, -1.0)
    p1 = jnp.max(prob, axis=1, keepdims=True)
    i1 = first_where(prob == p1)
    prob2 = jnp.where(lane == i1, -1.0, prob)
    p2 = jnp.max(prob2, axis=1, keepdims=True)
    i2 = first_where(prob2 == p2)
    tot = p1 + p2
    c1 = g_w * (p1 / tot)
    c2 = g_w * (p2 / tot)
    e1 = (i1 - N_EXPERT_GROUPS).astype(F32)
    e2 = (i2 - N_EXPERT_GROUPS).astype(F32)
    return e1, e2, c1, c2


def _outproj_body(mp_ref, xp_ref, ms_ref, xs_ref, wo_ref, g_ref, wrh_ref, wrl_ref, br_ref, x1_ref, route_ref, *, n_prompt):
    def compute(m_ref, x_ref):
        x1 = x_ref[...] + _dot(m_ref[...], wo_ref[...])
        x1_ref[...] = x1
        hi, lo = _split2(_rmsnorm(x1, g_ref[...]))
        logits = _dot(hi, wrh_ref[...]) + _dot(hi, wrl_ref[...]) + _dot(lo, wrh_ref[...]) + br_ref[...]
        e1, e2, c1, c2 = _route(logits)
        lane = lax.broadcasted_iota(jnp.int32, logits.shape, 1)
        route_ref[...] = jnp.where(lane == 0, e1, jnp.where(lane == 1, e2, jnp.where(lane == 2, c1, jnp.where(lane == 3, c2, 0.0))))

    i = pl.program_id(0)
    pl.when(i < n_prompt)(lambda: compute(mp_ref, xp_ref))
    pl.when(i >= n_prompt)(lambda: compute(ms_ref, xs_ref))


def _outproj_call(merged_p, xp, merged_s, xs, w_o, g, wr_hi, wr_lo, br, tm):
    n_p, n_s = xp.shape[0] // tm, xs.shape[0] // tm
    t_all = xp.shape[0] + xs.shape[0]

    def p_map(i):
        return (jnp.minimum(i, n_p - 1), 0)

    def s_map(i):
        return (jnp.maximum(i - n_p, 0), 0)

    def const(shape):
        return pl.BlockSpec(shape, lambda i: (0, 0))

    return pl.pallas_call(
        functools.partial(_outproj_body, n_prompt=n_p),
        name="outproj",
        grid=(n_p + n_s,),
        in_specs=[
            pl.BlockSpec((tm, D_MODEL), p_map), pl.BlockSpec((tm, D_MODEL), p_map),
            pl.BlockSpec((tm, D_MODEL), s_map), pl.BlockSpec((tm, D_MODEL), s_map),
            const((D_MODEL, D_MODEL)), const((1, D_MODEL)), const((D_MODEL, LANES)), const((D_MODEL, LANES)),
            const((1, LANES)),
        ],
        out_specs=[pl.BlockSpec((tm, D_MODEL), lambda i: (i, 0)), pl.BlockSpec((tm, LANES), lambda i: (i, 0))],
        out_shape=[jax.ShapeDtypeStruct((t_all, D_MODEL), F32), jax.ShapeDtypeStruct((t_all, LANES), F32)],
        compiler_params=_cparams(("arbitrary",), VMEM_LIMIT),
    )(merged_p, xp, merged_s, xs, w_o, g, wr_hi, wr_lo, br)


def _pack_bf16_pairs(x):
    n = x.shape[1] // 2
    bits = pltpu.bitcast(x.astype(BF16).astype(F32), jnp.uint32)
    return (bits[:, n:] & jnp.uint32(0xFFFF0000)) | (bits[:, :n] >> 16)


def _unpack_bf16_pairs(packed):
    lo = pltpu.bitcast(packed << 16, F32)
    hi = pltpu.bitcast(packed & jnp.uint32(0xFFFF0000), F32)
    return jnp.concatenate([lo, hi], axis=1).astype(BF16)


def _row_copy(src, src_row, dst, dst_row, sem):
    return pltpu.make_async_copy(src.at[pl.ds(src_row, 1), :], dst.at[pl.ds(dst_row, 1), :], sem)


def _dispatch_body(pos_ref, pad_tile_ref, x1_ref, g_ref, xs_hbm, h_scr, sem, *, tile):
    tm = h_scr.shape[0]

    @pl.when(pl.program_id(0) == 0)
    def _():
        h_scr[...] = jnp.zeros_like(h_scr)

        def pad_copy(t, k):
            row = pl.multiple_of(t * tile + k * tm, tm)
            return pltpu.make_async_copy(h_scr, xs_hbm.at[pl.ds(row, tm), :], sem)

        n_used = pad_tile_ref[N_EXPERTS]
        for phase in ("start", "wait"):
            def expert_pad(e, carry, phase=phase):
                @pl.when(pad_tile_ref[e] >= 0)
                def _():
                    for k in range(tile // tm):
                        getattr(pad_copy(pad_tile_ref[e], k), phase)()
                return carry
            lax.fori_loop(0, N_EXPERTS, expert_pad, 0)

            def unused_tile(t, carry, phase=phase):
                @pl.when(t >= n_used)
                def _():
                    for k in range(tile // tm):
                        getattr(pad_copy(t, k), phase)()
                return carry
            lax.fori_loop(0, xs_hbm.shape[0] // tile, unused_tile, 0)

    h_scr[...] = _pack_bf16_pairs(_rmsnorm(x1_ref[...], g_ref[...]))

    def issue(j, carry):
        _row_copy(h_scr, j, xs_hbm, pos_ref[0, 0, 2 * j], sem).start()
        _row_copy(h_scr, j, xs_hbm, pos_ref[0, 0, 2 * j + 1], sem).start()
        return carry

    lax.fori_loop(0, tm, issue, 0, unroll=8)

    def drain(j, carry):
        _row_copy(h_scr, j, xs_hbm, pos_ref[0, 0, 2 * j], sem).wait()
        _row_copy(h_scr, j, xs_hbm, pos_ref[0, 0, 2 * j + 1], sem).wait()
        return carry

    lax.fori_loop(0, tm, drain, 0, unroll=8)


def _dispatch_call(pos3, pad_tile, x1, g, rows, tm, tile):
    t = x1.shape[0]
    assert tile % tm == 0
    return pl.pallas_call(
        functools.partial(_dispatch_body, tile=tile),
        name="dispatch",
        grid=(t // tm,),
        in_specs=[
            pl.BlockSpec((1, 1, 2 * tm), lambda i: (i, 0, 0), memory_space=pltpu.SMEM),
            pl.BlockSpec(memory_space=pltpu.SMEM),
            pl.BlockSpec((tm, D_MODEL), lambda i: (i, 0)),
            pl.BlockSpec((1, D_MODEL), lambda i: (0, 0)),
        ],
        out_specs=pl.BlockSpec(memory_space=pl.ANY),
        out_shape=jax.ShapeDtypeStruct((rows, D_MODEL // 2), jnp.uint32),
        scratch_shapes=[pltpu.VMEM((tm, D_MODEL // 2), jnp.uint32), pltpu.SemaphoreType.DMA(())],
        compiler_params=_cparams(("arbitrary",)),
    )(pos3, pad_tile, x1, g)


def _moe_body(te_ref, tv_ref, nu_ref, xs_ref, wg_ref, wu_ref, wd_ref, ys_ref):
    i = pl.program_id(0)

    @pl.when(i < nu_ref[0])
    def _():
        row = lax.broadcasted_iota(jnp.int32, xs_ref.shape, 0)
        x = _unpack_bf16_pairs(jnp.where(row < tv_ref[i], xs_ref[...], jnp.uint32(0)))
        a = jax.nn.silu(_dot(x, wg_ref[0])) * _dot(x, wu_ref[0])
        ys_ref[...] = _dot(a.astype(BF16), wd_ref[0])

    @pl.when(i >= nu_ref[0])
    def _():
        ys_ref[...] = jnp.zeros_like(ys_ref)


def _moe_call(tile_expert, tile_valid, n_used, xs, w_gate, w_up, w_down, tm):
    rows = xs.shape[0]

    def row_map(i, te, tv, nu):
        return (jnp.minimum(i, nu[0] - 1), 0)

    def out_map(i, te, tv, nu):
        return (i, 0)

    def w_map(i, te, tv, nu):
        return (te[i], 0, 0)

    return pl.pallas_call(
        _moe_body,
        name="moe",
        grid_spec=pltpu.PrefetchScalarGridSpec(
            num_scalar_prefetch=3,
            grid=(rows // tm,),
            in_specs=[
                pl.BlockSpec((tm, D_MODEL // 2), row_map),
                pl.BlockSpec((1, D_MODEL, D_EXPERT), w_map),
                pl.BlockSpec((1, D_MODEL, D_EXPERT), w_map),
                pl.BlockSpec((1, D_EXPERT, D_MODEL), w_map),
            ],
            out_specs=pl.BlockSpec((tm, D_MODEL), out_map),
        ),
        out_shape=jax.ShapeDtypeStruct((rows, D_MODEL), F32),
        compiler_params=_cparams(("arbitrary",), VMEM_LIMIT),
    )(tile_expert, tile_valid, n_used, xs, w_gate, w_up, w_down)


def _combine_body(pos_ref, x1_ref, route_ref, ys_hbm, y_ref, buf0, buf1, sem):
    tm = buf0.shape[0]

    def issue(j, carry):
        _row_copy(ys_hbm, pos_ref[0, 0, 2 * j], buf0, j, sem).start()
        _row_copy(ys_hbm, pos_ref[0, 0, 2 * j + 1], buf1, j, sem).start()
        return carry

    lax.fori_loop(0, tm, issue, 0, unroll=8)

    def drain(j, carry):
        _row_copy(ys_hbm, pos_ref[0, 0, 2 * j], buf0, j, sem).wait()
        _row_copy(ys_hbm, pos_ref[0, 0, 2 * j + 1], buf1, j, sem).wait()
        return carry

    lax.fori_loop(0, tm, drain, 0, unroll=8)
    route = route_ref[...]
    y_ref[...] = x1_ref[...] + (route[:, 2:3] * buf0[...] + route[:, 3:4] * buf1[...])


def _combine_call(name, pos3, x1_all, route, ys, t, row0, tm):
    off = row0 // tm
    return pl.pallas_call(
        _combine_body,
        name=name,
        grid=(t // tm,),
        in_specs=[
            pl.BlockSpec((1, 1, 2 * tm), lambda i: (i + off, 0, 0), memory_space=pltpu.SMEM),
            pl.BlockSpec((tm, D_MODEL), lambda i: (i + off, 0)),
            pl.BlockSpec((tm, LANES), lambda i: (i + off, 0)),
            pl.BlockSpec(memory_space=pl.ANY),
        ],
        out_specs=pl.BlockSpec((tm, D_MODEL), lambda i: (i, 0)),
        out_shape=jax.ShapeDtypeStruct((t, D_MODEL), F32),
        scratch_shapes=[pltpu.VMEM((tm, D_MODEL), F32), pltpu.VMEM((tm, D_MODEL), F32), pltpu.SemaphoreType.DMA(())],
        compiler_params=_cparams(("arbitrary",)),
    )(pos3, x1_all, route, ys)


def _routing_offsets(route, tm, n_tiles):
    e = route[:, :2].astype(jnp.int32)
    onehot = jnp.sum((e[:, :, None] == jnp.arange(N_EXPERTS, dtype=jnp.int32)[None, None, :]).astype(jnp.int32), axis=1)
    csum = jnp.cumsum(onehot, axis=0)
    rank = csum - onehot
    counts = csum[-1]
    tiles_e = (counts + tm - 1) // tm
    tile_end = jnp.cumsum(tiles_e)
    tile_start = tile_end - tiles_e
    pos = tile_start[e] * tm + jnp.take_along_axis(rank, e, axis=1)
    n_used = tile_end[-1]
    tile_id = jnp.minimum(jnp.arange(n_tiles, dtype=jnp.int32), n_used - 1)
    tile_expert = jnp.sum((tile_id[:, None] >= tile_end[None, :]).astype(jnp.int32), axis=1)
    tile_valid = jnp.clip(counts[tile_expert] - (tile_id - tile_start[tile_expert]) * tm, 0, tm)
    pad_tile = jnp.concatenate([jnp.where(tiles_e > 0, tile_end - 1, -1), n_used.reshape(1)])
    i32 = jnp.int32
    return (pos.astype(i32), tile_expert.astype(i32), tile_valid.astype(i32), n_used.reshape(1).astype(i32),
            pad_tile.astype(i32))


def _pick_tile(t, pref):
    tm = min(pref, t)
    assert t % tm == 0, (t, tm)
    return tm


def kernel(x_prompt, x_sample, cache_a_kv, cache_b1_kv, cache_b2_kv, cache_b3_kv, g_attn_norm, w_in, q_norm_a, k_norm_a, q_norm_b, k_norm_b, sinks_a, w_branch_a, w_branch_b, w_out, g_ffn_norm, w_group_router, b_group_router, w_expert_router, b_expert_router, w_expert_gate, w_expert_up, w_expert_down):
    n, s, _ = x_prompt.shape
    ns, ts, _ = x_sample.shape
    assert ts == SAMPLE_T and s == B_PATTERNS[-1][0] and x_prompt.shape[2] == D_MODEL
    assert g_attn_norm.shape[0] == 1, "single layer"
    tp, tsmp = n * s, ns * ts
    t_all = tp + tsmp
    xp = x_prompt.reshape(tp, D_MODEL)
    xs_in = x_sample.reshape(tsmp, D_MODEL)

    w = w_in[0].astype(BF16)
    blk = A_Q_W + 2 * A_KV_W
    assert blk == B_W and blk % PROJ_SEG == 0 and B_GW == PROJ_SEG
    rep = LANES // HEAD_DIM
    gains = jnp.stack([jnp.tile(v[0], rep) for v in (q_norm_a, k_norm_a, q_norm_b, k_norm_b)]).astype(F32)
    g_attn = g_attn_norm.astype(F32)
    g_ffn = g_ffn_norm.astype(F32)
    plan_a = ([(0, c, PROJ_SEG, 0) for c in range(0, A_Q_W, PROJ_SEG)],
              [(0, A_Q_W, A_KV_W, 1), (0, A_Q_W + A_KV_W, A_KV_W, -1)])
    plan_q = ([(0, c, PROJ_SEG, 2) for c in range(0, B_W, PROJ_SEG)],)
    plan_kv = tuple([(0, g * B_GW, B_GW, 3), (1, g * B_GW, B_GW, -1)] for g in range(B_GROUPS))

    def project(x2d, pos, tm, transposed_seq=None):
        tables = _rope_tables(pos)
        qa, kva = _proj_call("proj_a", x2d, g_attn, w, [(blk, 0)], gains, tables, plan_a, tm)
        (qb,) = _proj_call("proj_qb", x2d, g_attn, w, [(blk, 1)], gains, tables, plan_q, tm)
        kvb = _proj_call("proj_kvb", x2d, g_attn, w, [(blk, 2), (blk, 3)], gains, tables, plan_kv, tm, transposed_seq)
        return qa, kva, qb, kvb

    tm_p = _pick_tile(s, ROW_TILE)
    assert B_PATTERNS[-1][0] >= s, "the last B group's new buffer is the whole sequence, written by proj_kvb"
    qa, kva, qb, kvb = project(xp, jnp.arange(s, dtype=jnp.int32), tm_p, s)
    kvb, kvb_last_t = kvb[:B_GROUPS], kvb[B_GROUPS]
    sinks = sinks_a[0].astype(F32)
    oa_p = _attn_a_call(qa, kva, sinks, n, s)
    ob_p = _attn_b_call(qb, kvb, n, s)

    def to_feature_major(c, heads):
        return jnp.transpose(c, (0, 2, 3, 4, 1)).reshape(c.shape[0], 2 * heads * HEAD_DIM, c.shape[1])

    def from_feature_major(c, heads):
        return jnp.transpose(c.reshape(c.shape[0], 2, heads, HEAD_DIM, c.shape[2]), (0, 4, 1, 2, 3))[None]

    new_a_p = from_feature_major(_tail_transpose_call("tail_a", kva, n, s, min(A_WINDOW, s)), A_KV_HEADS)
    new_b_p = [from_feature_major(_tail_transpose_call("tail_b%d" % g, kvb[g], n, s, min(win, s)), B_HPG)
               for g, (win, _) in enumerate(B_PATTERNS[:-1])]
    new_b_p.append(from_feature_major(kvb_last_t, B_HPG))

    tm_s = _pick_tile(tsmp, ROW_TILE)
    pos_s = PAST_LEN + (jnp.arange(tm_s, dtype=jnp.int32) % ts)
    qa_s, kva_s, qb_s, kvb_s = project(xs_in, pos_s, tm_s)
    caches = [to_feature_major(cache_a_kv[0], A_KV_HEADS)]
    caches += [to_feature_major(c[0], B_HPG) for c in (cache_b1_kv, cache_b2_kv, cache_b3_kv)]
    oa_s, ob_s, na, nb1, nb2, nb3 = _sample_call(sinks, qa_s, kva_s, qb_s, kvb_s, caches, ns)
    new_a_s = from_feature_major(na, A_KV_HEADS)
    new_b_s = [from_feature_major(c, B_HPG) for c in (nb1, nb2, nb3)]

    w_ba = w_branch_a[0].astype(BF16)
    w_bb = w_branch_b[0].astype(BF16)
    w_o = w_out[0].astype(BF16)
    wr = jnp.zeros((D_MODEL, LANES), F32)
    wr = wr.at[:, :N_EXPERT_GROUPS].set(w_group_router[0]).at[:, N_EXPERT_GROUPS:N_EXPERT_GROUPS + N_EXPERTS].set(w_expert_router[0])
    wr_hi = wr.astype(BF16)
    wr_lo = (wr - wr_hi.astype(F32)).astype(BF16)
    br = jnp.zeros((1, LANES), F32)
    br = br.at[0, :N_EXPERT_GROUPS].set(b_group_router[0]).at[0, N_EXPERT_GROUPS:N_EXPERT_GROUPS + N_EXPERTS].set(b_expert_router[0])

    tn = 512
    tm_mp, tm_ms = _pick_tile(tp, 1024), _pick_tile(tsmp, 1024)
    merged_p = _merge_call(xp, g_attn, oa_p, ob_p, w, 4 * blk, w_ba, w_bb, tm_mp, tn)
    merged_s = _merge_call(xs_in, g_attn, oa_s, ob_s, w, 4 * blk, w_ba, w_bb, tm_ms, tn)
    tm_o = _pick_tile(tsmp, ROW_TILE)
    assert tp % tm_o == 0
    x1_all, route = _outproj_call(merged_p, xp, merged_s, xs_in, w_o, g_ffn, wr_hi, wr_lo, br, tm_o)

    tm_e = MOE_TILE
    n_tiles = (2 * t_all + N_EXPERTS * (tm_e - 1)) // tm_e + 1
    pos, tile_expert, tile_valid, n_used, pad_tile = _routing_offsets(route, tm_e, n_tiles)
    tm_d = _pick_tile(tsmp, ROW_TILE)
    pos3 = pos.reshape(t_all // tm_d, 1, 2 * tm_d)
    xs_sorted = _dispatch_call(pos3, pad_tile, x1_all, g_ffn, n_tiles * tm_e, tm_d, tm_e)
    ys = _moe_call(tile_expert, tile_valid, n_used, xs_sorted, w_expert_gate[0].astype(BF16), w_expert_up[0].astype(BF16),
                   w_expert_down[0].astype(BF16), tm_e)
    y_p = _combine_call("combine_p", pos3, x1_all, route, ys, tp, 0, tm_d)
    y_s = _combine_call("combine_s", pos3, x1_all, route, ys, tsmp, tp, tm_d)

    return (y_p.reshape(n, s, D_MODEL), y_s.reshape(ns, ts, D_MODEL),
            new_a_p, new_a_s, new_b_p[0], new_b_s[0], new_b_p[1], new_b_s[1], new_b_p[2], new_b_s[2])
```

```python
import functools

import jax
import jax.numpy as jnp
import numpy as np
from jax import lax
from jax.experimental import pallas as pl
from jax.experimental.pallas import tpu as pltpu

F32 = jnp.float32
BF16 = jnp.bfloat16

D_MODEL = 2048
HEAD_DIM = 64
ROT_DIM = HEAD_DIM // 4
ROPE_THETA = 500000.0
NORM_EPS = 1e-6
BLOCK = 128
PAST_LEN = 16384

A_HEADS = 16
A_KV_HEADS = 4
A_WINDOW = 128
B_PATTERNS = ((128, 1), (512, 4), (2048, 16))
B_GROUPS = 3
B_HPG = 8
A_Q_W = A_HEADS * HEAD_DIM
A_KV_W = A_KV_HEADS * HEAD_DIM
B_GW = B_HPG * HEAD_DIM
B_W = B_GROUPS * B_GW

N_EXPERT_GROUPS = 4
EXPERTS_PER_GROUP = 8
N_EXPERTS = N_EXPERT_GROUPS * EXPERTS_PER_GROUP
D_EXPERT = 512

LANES = 128
SUBLANES = 8
VMEM_LIMIT = 56 * 1024 * 1024
SAMPLE_T = 8

MOE_TILE = 512
ROW_TILE = 256


def _cparams(sem, vmem=None):
    return pltpu.CompilerParams(dimension_semantics=sem, vmem_limit_bytes=vmem)


def _rmsnorm(x, g):
    return x * lax.rsqrt(jnp.mean(x * x, axis=-1, keepdims=True) + NORM_EPS) * g


def _split2(v):
    hi = v.astype(BF16)
    lo = (v - hi.astype(F32)).astype(BF16)
    return hi, lo


def _split3(v):
    hi = v.astype(BF16)
    r = v - hi.astype(F32)
    mid = r.astype(BF16)
    lo = (r - mid.astype(F32)).astype(BF16)
    return hi, mid, lo


def _dot(a, b):
    return jnp.dot(a, b, preferred_element_type=F32)


def _multiple_of(x, m):
    return x if isinstance(x, int) else pl.multiple_of(x, m)


def _dot_nt(a, b):
    return lax.dot_general(a, b, (((1,), (1,)), ((), ())), preferred_element_type=F32)


def _rope_tables(pos):
    half = ROT_DIM // 2
    inv_freq = ROPE_THETA ** (-jnp.arange(half, dtype=F32) / half)
    ang = pos.astype(F32)[:, None] * inv_freq[None, :]
    cos, sin = jnp.cos(ang), jnp.sin(ang)
    p = pos.shape[0]
    ones = jnp.ones((p, HEAD_DIM - ROT_DIM), F32)
    z8 = jnp.zeros((p, half), F32)
    z48 = jnp.zeros((p, HEAD_DIM - ROT_DIM), F32)
    c = jnp.concatenate([cos, cos, ones], axis=1)
    s_up = jnp.concatenate([-sin, z8, z48], axis=1)
    s_dn = jnp.concatenate([z8, sin, z48], axis=1)
    rep = LANES // HEAD_DIM
    return jnp.tile(c, (1, rep)), jnp.tile(s_up, (1, rep)), jnp.tile(s_dn, (1, rep))


PROJ_SEG = 512


def _proj_body(*refs, plan, n_w, transposed):
    x_ref, g_ref = refs[:2]
    w_refs = refs[2:2 + n_w]
    gain_ref, c_ref, su_ref, sd_ref = refs[2 + n_w:6 + n_w]
    out_refs = refs[6 + n_w:]
    h = _rmsnorm(x_ref[...], g_ref[...]).astype(BF16)
    low = lax.broadcasted_iota(jnp.int32, (x_ref.shape[0], LANES), 1) < HEAD_DIM
    for oi, segments in enumerate(plan):
        o_ref = out_refs[oi]
        ocol = 0
        for widx, col0, width, kind in segments:
            z_all = _dot(h, w_refs[widx][:, col0:col0 + width])
            for c in range(width // LANES):
                z = z_all[:, c * LANES:(c + 1) * LANES]
                if kind >= 0:
                    zz = z * z
                    ss = jnp.where(low, jnp.sum(jnp.where(low, zz, 0.0), axis=1, keepdims=True),
                                   jnp.sum(jnp.where(low, 0.0, zz), axis=1, keepdims=True))
                    z = z * lax.rsqrt(ss * (1.0 / HEAD_DIM) + NORM_EPS) * gain_ref[kind:kind + 1, :]
                    z = (z * c_ref[...] + pltpu.roll(z, LANES - ROT_DIM // 2, 1) * su_ref[...]
                         + pltpu.roll(z, ROT_DIM // 2, 1) * sd_ref[...])
                    if kind % 2 == 0:
                        z = z * (HEAD_DIM ** -0.5)
                o_ref[:, ocol:ocol + LANES] = z
                if transposed and oi == len(plan) - 1:
                    out_refs[-1][ocol:ocol + LANES, :] = z.T
                ocol += LANES


def _proj_call(name, x2d, g, w_full, w_blocks, gains, tables, plan, tm, transposed_seq=None):
    t = x2d.shape[0]
    pos_blocks = tables[0].shape[0] // tm
    tab_spec = pl.BlockSpec((tm, LANES), lambda i: (i % pos_blocks, 0))
    widths = [sum(seg[2] for seg in segments) for segments in plan]
    out_specs = [pl.BlockSpec((tm, w_), lambda i: (i, 0)) for w_ in widths]
    out_shape = [jax.ShapeDtypeStruct((t, w_), F32) for w_ in widths]
    if transposed_seq is not None:
        per_seq = transposed_seq // tm
        out_specs.append(pl.BlockSpec((None, widths[-1], tm), lambda i: (i // per_seq, 0, i % per_seq)))
        out_shape.append(jax.ShapeDtypeStruct((t // transposed_seq, widths[-1], transposed_seq), F32))
    return pl.pallas_call(
        functools.partial(_proj_body, plan=plan, n_w=len(w_blocks), transposed=transposed_seq is not None),
        name=name,
        grid=(t // tm,),
        in_specs=[
            pl.BlockSpec((tm, D_MODEL), lambda i: (i, 0)),
            pl.BlockSpec((1, D_MODEL), lambda i: (0, 0)),
            *[pl.BlockSpec((D_MODEL, bw), lambda i, bj=bj: (0, bj)) for bw, bj in w_blocks],
            pl.BlockSpec((4, LANES), lambda i: (0, 0)),
            tab_spec, tab_spec, tab_spec,
        ],
        out_specs=out_specs,
        out_shape=out_shape,
        compiler_params=_cparams(("parallel",), VMEM_LIMIT),
    )(x2d, g, *([w_full] * len(w_blocks)), gains, *tables)


def _band_mask(rows, nkeys, max_dist, has_prev):
    qi = lax.broadcasted_iota(jnp.int32, (rows, nkeys), 0) & (BLOCK - 1)
    kk = lax.broadcasted_iota(jnp.int32, (rows, nkeys), 1)
    if nkeys == BLOCK:
        dist = qi - kk
        return jnp.logical_and(dist >= 0, dist <= max_dist)
    dist = qi + BLOCK - kk
    band = jnp.logical_and(dist >= 0, dist <= max_dist)
    return jnp.logical_and(band, jnp.logical_or(kk >= BLOCK, has_prev))


UNIT_ROWS = 2 * BLOCK


def _band_bias(nkeys, max_dist, has_prev):
    return jnp.where(_band_mask(UNIT_ROWS, nkeys, max_dist, has_prev), 0.0, -jnp.inf).astype(F32)


def _softmax_stage(s_ref, bias, sink, p_ref, t_ref):
    s = s_ref[...] + bias
    m = jnp.max(s, axis=1, keepdims=True)
    if sink is not None:
        m = jnp.maximum(m, sink)
    p = jnp.exp(s - m)
    denom = jnp.sum(p, axis=1, keepdims=True)
    if sink is not None:
        denom = denom + jnp.exp(sink - m)
    p_ref[...] = p.astype(BF16)
    t_ref[:, :LANES] = jnp.broadcast_to(1.0 / denom, (UNIT_ROWS, LANES))
    t_ref[:, LANES:] = jnp.broadcast_to(m + jnp.log(denom), (UNIT_ROWS, LANES))


def _pipeline(n_units, qk, sm, pv):
    qk(0, 0)
    qk(1, 1)
    sm(0, 0)

    def body(j, carry):
        i = 2 * j
        pv(i - 2, 0)
        sm(i - 1, 1)
        qk(i, 0)
        pv(i - 1, 1)
        sm(i, 0)
        qk(i + 1, 1)
        return carry

    lax.fori_loop(1, n_units // 2, body, 0)
    pv(n_units - 2, 0)
    sm(n_units - 1, 1)
    pv(n_units - 1, 1)


def _attn_a_body(sink_ref, q_ref, kv_ref, o_ref, s_scr, p_scr, t_scr, bias_scr, *, s):
    nb = s // BLOCK
    low = lax.broadcasted_iota(jnp.int32, (BLOCK, LANES), 1) < HEAD_DIM
    high = jnp.logical_not(low)
    top = lax.broadcasted_iota(jnp.int32, (UNIT_ROWS, 1), 0) < BLOCK
    bias_scr[0] = _band_bias(2 * BLOCK, A_WINDOW - 1, False)
    bias_scr[1] = _band_bias(2 * BLOCK, A_WINDOW - 1, True)
    pairs = A_KV_HEADS // 2
    per_pair = A_HEADS // A_KV_HEADS
    units = [(c, hp) for c in range(pairs) for hp in range(per_pair)]

    def rows2(b, lanes):
        start_c = _multiple_of(b * BLOCK, BLOCK)
        start_p = _multiple_of(jnp.maximum(b - 1, 0) * BLOCK, BLOCK)
        return jnp.concatenate([kv_ref[pl.ds(start_p, BLOCK), lanes], kv_ref[pl.ds(start_c, BLOCK), lanes]], axis=0)

    def qk(b, slot):
        start_c = _multiple_of(b * BLOCK, BLOCK)
        for c in range(pairs):
            k2 = rows2(b, slice(c * LANES, (c + 1) * LANES)).astype(BF16)
            for hp in range(per_pair):
                col = c * per_pair + hp
                q = q_ref[pl.ds(start_c, BLOCK), col * LANES:(col + 1) * LANES]
                qr = pltpu.roll(q, HEAD_DIM, 1)
                if hp < per_pair // 2:
                    lhs = jnp.concatenate([jnp.where(low, q, 0.0), jnp.where(low, qr, 0.0)], axis=0)
                else:
                    lhs = jnp.concatenate([jnp.where(high, qr, 0.0), jnp.where(high, q, 0.0)], axis=0)
                s_scr[slot, units.index((c, hp))] = _dot_nt(lhs.astype(BF16), k2)

    def sm(b, slot):
        bias = bias_scr[jnp.minimum(b, 1)]
        for ui, (c, hp) in enumerate(units):
            col = c * per_pair + hp
            sink = jnp.where(top, sink_ref[2 * col], sink_ref[2 * col + 1])
            _softmax_stage(s_scr.at[slot, ui], bias, sink, p_scr.at[slot, ui], t_scr.at[slot, ui])

    def pv(b, slot):
        start_c = _multiple_of(b * BLOCK, BLOCK)
        for c in range(pairs):
            v2 = rows2(b, slice(A_KV_W + c * LANES, A_KV_W + (c + 1) * LANES)).astype(BF16)
            for hp in range(per_pair):
                ui = units.index((c, hp))
                col = c * per_pair + hp
                o2 = _dot(p_scr[slot, ui], v2) * t_scr[slot, ui, :, :LANES]
                if hp < per_pair // 2:
                    o = jnp.where(low, o2[:BLOCK], pltpu.roll(o2[BLOCK:], HEAD_DIM, 1))
                else:
                    o = jnp.where(low, pltpu.roll(o2[:BLOCK], HEAD_DIM, 1), o2[BLOCK:])
                o_ref[pl.ds(start_c, BLOCK), col * LANES:(col + 1) * LANES] = o.astype(o_ref.dtype)

    _pipeline(nb, qk, sm, pv)


def _attn_a_call(qa, kva, sinks, n, s):
    n_units = (A_KV_HEADS // 2) * (A_HEADS // A_KV_HEADS)
    return pl.pallas_call(
        functools.partial(_attn_a_body, s=s),
        name="attn_a",
        grid=(n,),
        in_specs=[
            pl.BlockSpec(memory_space=pltpu.SMEM),
            pl.BlockSpec((s, A_Q_W), lambda i: (i, 0)),
            pl.BlockSpec((s, 2 * A_KV_W), lambda i: (i, 0)),
        ],
        out_specs=pl.BlockSpec((s, A_Q_W), lambda i: (i, 0)),
        out_shape=jax.ShapeDtypeStruct((n * s, A_Q_W), BF16),
        scratch_shapes=[pltpu.VMEM((2, n_units, UNIT_ROWS, 2 * BLOCK), F32),
                        pltpu.VMEM((2, n_units, UNIT_ROWS, 2 * BLOCK), BF16),
                        pltpu.VMEM((2, n_units, UNIT_ROWS, 2 * LANES), F32),
                        pltpu.VMEM((2, UNIT_ROWS, 2 * BLOCK), F32)],
        compiler_params=_cparams(("parallel",), VMEM_LIMIT),
    )(sinks, qa, kva)


def _attn_b_body(q1, q2, q3, k1, v1, k2, v2, k3, v3, o_ref, o_scr, l_scr, s_scr, p_scr, t_scr, bias_scr, *, s):
    low = lax.broadcasted_iota(jnp.int32, (BLOCK, LANES), 1) < HEAD_DIM
    bias_scr[0] = _band_bias(2 * BLOCK, BLOCK, False)
    bias_scr[1] = _band_bias(2 * BLOCK, BLOCK, True)
    bias_cur = _band_bias(BLOCK, BLOCK, False)
    for g, ((_, dil), q_ref, k_ref, v_ref) in enumerate(zip(B_PATTERNS, (q1, q2, q3), (k1, k2, k3), (v1, v2, v3))):
        span = BLOCK * dil
        nblk = max(s // span, 1)
        use_prev = nblk > 1
        nk = 2 * BLOCK if use_prev else BLOCK

        def rows(ref, start, dil=dil):
            if dil == 1:
                return ref[pl.ds(_multiple_of(start, BLOCK), BLOCK), :]
            return ref[pl.ds(start, BLOCK, stride=dil), :]

        def starts(u, dil=dil, span=span):
            b = u // dil
            return b * span + u % dil, jnp.maximum(b - 1, 0) * span + u % dil, b

        def rows2(ref, u, rows=rows, starts=starts, use_prev=use_prev):
            start_c, start_p, _ = starts(u)
            if use_prev:
                return jnp.concatenate([rows(ref, start_p), rows(ref, start_c)], axis=0)
            return rows(ref, start_c)

        def qk(u, slot, q_ref=q_ref, k_ref=k_ref, rows=rows, starts=starts, rows2=rows2, nk=nk):
            q = rows(q_ref, starts(u)[0])
            lhs = jnp.concatenate([jnp.where(low, q, 0.0), jnp.where(low, 0.0, q)], axis=0).astype(BF16)
            s_scr[slot, :, :nk] = _dot_nt(lhs, rows2(k_ref, u).astype(BF16))

        def sm(u, slot, starts=starts, use_prev=use_prev, nk=nk):
            bias = bias_scr[jnp.minimum(starts(u)[2], 1)] if use_prev else bias_cur
            _softmax_stage(s_scr.at[slot, :, :nk], bias, None, p_scr.at[slot, :, :nk], t_scr.at[slot])

        def pv(u, slot, g=g, dil=dil, v_ref=v_ref, starts=starts, rows2=rows2, nk=nk):
            o2 = _dot(p_scr[slot, :, :nk], rows2(v_ref, u).astype(BF16)) * t_scr[slot, :, :LANES]
            l2 = t_scr[slot, :, LANES:]
            o = jnp.where(low, o2[:BLOCK], o2[BLOCK:])
            lse = jnp.where(low, l2[:BLOCK], l2[BLOCK:])
            start_c = starts(u)[0]
            if dil == 1:
                idx = pl.ds(_multiple_of(start_c, BLOCK), BLOCK)
            else:
                idx = pl.ds(start_c, BLOCK, stride=dil)
            o_scr[g, idx, :] = o
            l_scr[g, idx, :] = lse

        _pipeline(nblk * dil, qk, sm, pv)

    chunk = 256

    def comb(c, carry):
        r0 = pl.multiple_of(c * chunk, chunk)
        l0, l1, l2 = (l_scr[g, pl.ds(r0, chunk), :] for g in range(B_GROUPS))
        m = jnp.maximum(jnp.maximum(l0, l1), l2)
        w0, w1, w2 = jnp.exp(l0 - m), jnp.exp(l1 - m), jnp.exp(l2 - m)
        inv = 1.0 / (w0 + w1 + w2)
        acc = (o_scr[0, pl.ds(r0, chunk), :] * (w0 * inv) + o_scr[1, pl.ds(r0, chunk), :] * (w1 * inv)
               + o_scr[2, pl.ds(r0, chunk), :] * (w2 * inv))
        o_ref[pl.ds(r0, chunk), :] = acc.astype(o_ref.dtype)
        return carry

    lax.fori_loop(0, s // chunk, comb, 0)


def _attn_b_call(qb, kvs, n, s):
    pairs = B_GW // LANES
    in_specs = [pl.BlockSpec((s, LANES), lambda i, h, g=g: (i, g * pairs + h)) for g in range(B_GROUPS)]
    args = [qb, qb, qb]
    for kv in kvs:
        in_specs.append(pl.BlockSpec((s, LANES), lambda i, h: (i, h)))
        in_specs.append(pl.BlockSpec((s, LANES), lambda i, h: (i, pairs + h)))
        args += [kv, kv]
    return pl.pallas_call(
        functools.partial(_attn_b_body, s=s),
        name="attn_b",
        grid=(n, pairs),
        in_specs=in_specs,
        out_specs=pl.BlockSpec((s, LANES), lambda i, h: (i, h)),
        out_shape=jax.ShapeDtypeStruct((n * s, B_GW), BF16),
        scratch_shapes=[pltpu.VMEM((B_GROUPS, s, LANES), F32), pltpu.VMEM((B_GROUPS, s, LANES), F32),
                        pltpu.VMEM((2, UNIT_ROWS, 2 * BLOCK), F32), pltpu.VMEM((2, UNIT_ROWS, 2 * BLOCK), BF16),
                        pltpu.VMEM((2, UNIT_ROWS, 2 * LANES), F32), pltpu.VMEM((2, UNIT_ROWS, 2 * BLOCK), F32)],
        compiler_params=_cparams(("parallel", "arbitrary"), VMEM_LIMIT),
    )(*args)


def _shift_insert(old_ref, new_ref, out_ref, w):
    ncol = w // LANES
    keep = lax.broadcasted_iota(jnp.int32, (BLOCK, LANES), 1) < LANES - SAMPLE_T

    def strip(k, carry):
        r0 = pl.multiple_of(k * BLOCK, BLOCK)
        x = old_ref[pl.ds(r0, BLOCK), :]
        pad = jnp.concatenate([jnp.zeros((BLOCK - SAMPLE_T, BLOCK), F32), new_ref[:, pl.ds(r0, BLOCK)]], axis=0)
        nxt = pad.T
        for j in reversed(range(ncol)):
            cur = pltpu.roll(x[:, j * LANES:(j + 1) * LANES], LANES - SAMPLE_T, 1)
            out_ref[pl.ds(r0, BLOCK), j * LANES:(j + 1) * LANES] = jnp.where(keep, cur, nxt)
            nxt = cur
        return carry

    lax.fori_loop(0, old_ref.shape[0] // BLOCK, strip, 0)


def _window_scores(lhs, buf_ref, old_ref, krow, w, dil, min_old):
    r = lhs.shape[0]
    t_main = lax.broadcasted_iota(jnp.int32, (r, w), 0) & (SAMPLE_T - 1)
    delta = (w - SAMPLE_T) + t_main - lax.broadcasted_iota(jnp.int32, (r, w), 1)
    valid_main = jnp.logical_and(delta >= 0, (delta & (dil - 1)) == 0)
    t_old = lax.broadcasted_iota(jnp.int32, (r, LANES), 0) & (SAMPLE_T - 1)
    c_old = lax.broadcasted_iota(jnp.int32, (r, LANES), 1)
    valid_old = jnp.logical_and(jnp.logical_and(c_old < SAMPLE_T, c_old >= t_old + min_old),
                                ((w + t_old - c_old) & (dil - 1)) == 0)
    kt = buf_ref[krow:krow + LANES, :].astype(BF16)
    kx = old_ref[krow:krow + LANES, 0:LANES].astype(BF16)
    return jnp.where(valid_main, _dot(lhs, kt), -jnp.inf), jnp.where(valid_old, _dot(lhs, kx), -jnp.inf)


def _window_softmax(s, sx, sink):
    m = jnp.maximum(jnp.max(s, axis=1, keepdims=True), jnp.max(sx, axis=1, keepdims=True))
    if sink is not None:
        m = jnp.maximum(m, sink)
    p = jnp.exp(s - m)
    px = jnp.exp(sx - m)
    denom = jnp.sum(p, axis=1, keepdims=True) + jnp.sum(px, axis=1, keepdims=True)
    if sink is not None:
        denom = denom + jnp.exp(sink - m)
    inv = 1.0 / denom
    return (p * inv).astype(BF16), (px * inv).astype(BF16), m + jnp.log(denom)


def _window_values(p, px, buf_ref, old_ref, vrow):
    vt = buf_ref[vrow:vrow + LANES, :].astype(BF16)
    vx = old_ref[vrow:vrow + LANES, 0:LANES].astype(BF16)
    return _dot_nt(p, vt) + _dot_nt(px, vx)


def _sample_body(sink_ref, qa_ref, kva_ref, qb_ref, kvb1_ref, kvb2_ref, kvb3_ref, ca_ref, cb1_ref, cb2_ref, cb3_ref,
                 oa_ref, ob_ref, na_ref, nb1_ref, nb2_ref, nb3_ref):
    low = lax.broadcasted_iota(jnp.int32, (SAMPLE_T, LANES), 1) < HEAD_DIM
    new_refs = (kvb1_ref, kvb2_ref, kvb3_ref)
    old_refs = (cb1_ref, cb2_ref, cb3_ref)
    buf_refs = (nb1_ref, nb2_ref, nb3_ref)
    _shift_insert(ca_ref, kva_ref, na_ref, A_WINDOW)
    for g, (win, _) in enumerate(B_PATTERNS):
        _shift_insert(old_refs[g], new_refs[g], buf_refs[g], win)

    units = []
    qa = qa_ref[...]
    heads_per_pair = 2 * (A_HEADS // A_KV_HEADS)
    for c in range(A_KV_HEADS // 2):
        blocks, sinks = [], []
        for jj in range(heads_per_pair):
            j = c * heads_per_pair + jj
            want_low = jj < heads_per_pair // 2
            q = qa[:, (j // 2) * LANES:(j // 2 + 1) * LANES]
            if (j % 2 == 0) != want_low:
                q = pltpu.roll(q, HEAD_DIM, 1)
            blocks.append(jnp.where(low if want_low else jnp.logical_not(low), q, 0.0))
            sinks.append(jnp.full((SAMPLE_T, 1), sink_ref[j], F32))
        units.append((jnp.concatenate(blocks, axis=0).astype(BF16), na_ref, ca_ref, c * LANES, A_KV_W + c * LANES,
                      A_WINDOW, 1, 1, jnp.concatenate(sinks, axis=0)))
    qb = qb_ref[...]
    pairs = B_GW // LANES
    for g, (win, dil) in enumerate(B_PATTERNS):
        for c in range(pairs):
            q = qb[:, g * B_GW + c * LANES:g * B_GW + (c + 1) * LANES]
            lhs = jnp.concatenate([jnp.where(low, q, 0.0), jnp.where(low, 0.0, q)], axis=0).astype(BF16)
            units.append((lhs, buf_refs[g], old_refs[g], c * LANES, B_GW + c * LANES, win, dil, 0, None))

    scores = [_window_scores(lhs, buf, old, krow, w, dil, min_old) for lhs, buf, old, krow, _, w, dil, min_old, _ in units]
    probs = [_window_softmax(s, sx, u[8]) for (s, sx), u in zip(scores, units)]
    results = [(_window_values(p, px, u[1], u[2], u[4]), lse) for (p, px, lse), u in zip(probs, units)]

    outs = {}
    for c in range(A_KV_HEADS // 2):
        o = results[c][0]
        for jj in range(heads_per_pair):
            j = c * heads_per_pair + jj
            ob = o[jj * SAMPLE_T:(jj + 1) * SAMPLE_T]
            if (j % 2 == 0) != (jj < heads_per_pair // 2):
                ob = pltpu.roll(ob, HEAD_DIM, 1)
            outs[j] = ob
    for c in range(A_HEADS // 2):
        oa_ref[:, c * LANES:(c + 1) * LANES] = jnp.where(low, outs[2 * c], outs[2 * c + 1])

    o_g, l_g = [], []
    for g in range(B_GROUPS):
        o_cols, l_cols = [], []
        for c in range(pairs):
            o, lse = results[A_KV_HEADS // 2 + g * pairs + c]
            lse = jnp.broadcast_to(lse, (2 * SAMPLE_T, LANES))
            o_cols.append(jnp.where(low, o[:SAMPLE_T], o[SAMPLE_T:]))
            l_cols.append(jnp.where(low, lse[:SAMPLE_T], lse[SAMPLE_T:]))
        o_g.append(jnp.concatenate(o_cols, axis=1))
        l_g.append(jnp.concatenate(l_cols, axis=1))
    m = jnp.maximum(jnp.maximum(l_g[0], l_g[1]), l_g[2])
    w0, w1, w2 = jnp.exp(l_g[0] - m), jnp.exp(l_g[1] - m), jnp.exp(l_g[2] - m)
    inv = 1.0 / (w0 + w1 + w2)
    ob_ref[...] = o_g[0] * (w0 * inv) + o_g[1] * (w1 * inv) + o_g[2] * (w2 * inv)


def _sample_call(sinks, qa, kva, qb, kvbs, caches, ns):
    def rows(width):
        return pl.BlockSpec((SAMPLE_T, width), lambda i: (i, 0))

    def cache(c):
        return pl.BlockSpec((None,) + c.shape[1:], lambda i: (i, 0, 0))

    news = [qa, kva, qb, *kvbs]
    in_specs = [pl.BlockSpec(memory_space=pltpu.SMEM)] + [rows(a.shape[1]) for a in news] + [cache(c) for c in caches]
    out_shape = ([jax.ShapeDtypeStruct((ns * SAMPLE_T, A_Q_W), F32), jax.ShapeDtypeStruct((ns * SAMPLE_T, B_GW), F32)]
                 + [jax.ShapeDtypeStruct(c.shape, F32) for c in caches])
    out_specs = [rows(A_Q_W), rows(B_GW)] + [cache(c) for c in caches]
    return pl.pallas_call(
        _sample_body,
        name="sample_attn",
        grid=(ns,),
        in_specs=in_specs,
        out_specs=out_specs,
        out_shape=out_shape,
        compiler_params=_cparams(("parallel",), VMEM_LIMIT),
    )(sinks, *news, *caches)


def _tail_transpose_body(x_ref, o_ref):
    o_ref[...] = x_ref[...].T


def _tail_transpose_call(name, kv, n, s, win):
    c = kv.shape[1]
    tt = min(win, 2 * LANES)
    per_seq, first = s // tt, (s - win) // tt
    return pl.pallas_call(
        _tail_transpose_body,
        name=name,
        grid=(n, win // tt),
        in_specs=[pl.BlockSpec((tt, c), lambda i, j: (i * per_seq + first + j, 0))],
        out_specs=pl.BlockSpec((None, c, tt), lambda i, j: (i, 0, j)),
        out_shape=jax.ShapeDtypeStruct((n, c, win), F32),
        compiler_params=_cparams(("parallel", "parallel")),
    )(kv)


def _merge_body(x_ref, g_ref, oa_ref, ob_ref, wga_ref, wgb_ref, wba_ref, wbb_ref, o_ref, h_scr):
    @pl.when(pl.program_id(1) == 0)
    def _():
        h_scr[...] = _rmsnorm(x_ref[...], g_ref[...]).astype(BF16)

    h = h_scr[...]
    oa = oa_ref[...].astype(BF16)
    ob = ob_ref[...].astype(BF16)
    half = o_ref.shape[1] // 2
    for c in range(2):
        cols = slice(c * half, (c + 1) * half)
        ga = _dot(h, wga_ref[:, cols])
        gb = _dot(h, wgb_ref[:, cols])
        ya = _dot(oa, wba_ref[:, cols])
        yb = _dot(ob, wbb_ref[:, cols])
        o_ref[:, cols] = (jax.nn.sigmoid(ga) * ya + jax.nn.sigmoid(gb) * yb).astype(o_ref.dtype)


def _merge_call(x2d, g, oa, ob, w_full, gate_col0, w_ba, w_bb, tm, tn):
    t = x2d.shape[0]
    ncol = D_MODEL // tn
    gate0 = gate_col0 // tn
    return pl.pallas_call(
        _merge_body,
        name="merge",
        grid=(t // tm, ncol),
        in_specs=[
            pl.BlockSpec((tm, D_MODEL), lambda i, j: (i, 0)),
            pl.BlockSpec((1, D_MODEL), lambda i, j: (0, 0)),
            pl.BlockSpec((tm, A_Q_W), lambda i, j: (i, 0)),
            pl.BlockSpec((tm, B_GW), lambda i, j: (i, 0)),
            pl.BlockSpec((D_MODEL, tn), lambda i, j: (0, gate0 + j)),
            pl.BlockSpec((D_MODEL, tn), lambda i, j: (0, gate0 + ncol + j)),
            pl.BlockSpec((A_Q_W, tn), lambda i, j: (0, j)),
            pl.BlockSpec((B_GW, tn), lambda i, j: (0, j)),
        ],
        out_specs=pl.BlockSpec((tm, tn), lambda i, j: (i, j)),
        out_shape=jax.ShapeDtypeStruct((t, D_MODEL), BF16),
        scratch_shapes=[pltpu.VMEM((tm, D_MODEL), BF16)],
        compiler_params=_cparams(("parallel", "arbitrary"), VMEM_LIMIT),
    )(x2d, g, oa, ob, w_full, w_full, w_ba, w_bb)


def _route(logits):
    tm = logits.shape[0]
    lane = lax.broadcasted_iota(jnp.int32, (tm, LANES), 1)
    neg = -jnp.inf
    big = LANES

    def first_where(cond):
        return jnp.min(jnp.where(cond, lane, big), axis=1, keepdims=True)

    gl = jnp.where(lane < N_EXPERT_GROUPS, logits, neg)
    gmax = jnp.max(gl, axis=1, keepdims=True)
    gidx = first_where(gl == gmax)
    g_w = 1.0 / jnp.sum(jnp.exp(gl - gmax), axis=1, keepdims=True)
    lo = N_EXPERT_GROUPS + gidx * EXPERTS_PER_GROUP
    in_grp = jnp.logical_and(lane >= lo, lane < lo + EXPERTS_PER_GROUP)
    el = jnp.where(in_grp, logits, neg)
    ep = jnp.exp(el - jnp.max(el, axis=1, keepdims=True))
    prob = ep / jnp.sum(ep, axis=1, keepdims=True)
    prob = jnp.where(in_grp, prob, -1.0)
    p1 = jnp.max(prob, axis=1, keepdims=True)
    i1 = first_where(prob == p1)
    prob2 = jnp.where(lane == i1, -1.0, prob)
    p2 = jnp.max(prob2, axis=1, keepdims=True)
    i2 = first_where(prob2 == p2)
    tot = p1 + p2
    c1 = g_w * (p1 / tot)
    c2 = g_w * (p2 / tot)
    e1 = (i1 - N_EXPERT_GROUPS).astype(F32)
    e2 = (i2 - N_EXPERT_GROUPS).astype(F32)
    return e1, e2, c1, c2


def _outproj_body(mp_ref, xp_ref, ms_ref, xs_ref, wo_ref, g_ref, wrh_ref, wrl_ref, br_ref, x1_ref, route_ref, *, n_prompt):
    def compute(m_ref, x_ref):
        x1 = x_ref[...] + _dot(m_ref[...], wo_ref[...])
        x1_ref[...] = x1
        hi, lo = _split2(_rmsnorm(x1, g_ref[...]))
        logits = _dot(hi, wrh_ref[...]) + _dot(hi, wrl_ref[...]) + _dot(lo, wrh_ref[...]) + br_ref[...]
        e1, e2, c1, c2 = _route(logits)
        lane = lax.broadcasted_iota(jnp.int32, logits.shape, 1)
        route_ref[...] = jnp.where(lane == 0, e1, jnp.where(lane == 1, e2, jnp.where(lane == 2, c1, jnp.where(lane == 3, c2, 0.0))))

    i = pl.program_id(0)
    pl.when(i < n_prompt)(lambda: compute(mp_ref, xp_ref))
    pl.when(i >= n_prompt)(lambda: compute(ms_ref, xs_ref))


def _outproj_call(merged_p, xp, merged_s, xs, w_o, g, wr_hi, wr_lo, br, tm):
    n_p, n_s = xp.shape[0] // tm, xs.shape[0] // tm
    t_all = xp.shape[0] + xs.shape[0]

    def p_map(i):
        return (jnp.minimum(i, n_p - 1), 0)

    def s_map(i):
        return (jnp.maximum(i - n_p, 0), 0)

    def const(shape):
        return pl.BlockSpec(shape, lambda i: (0, 0))

    return pl.pallas_call(
        functools.partial(_outproj_body, n_prompt=n_p),
        name="outproj",
        grid=(n_p + n_s,),
        in_specs=[
            pl.BlockSpec((tm, D_MODEL), p_map), pl.BlockSpec((tm, D_MODEL), p_map),
            pl.BlockSpec((tm, D_MODEL), s_map), pl.BlockSpec((tm, D_MODEL), s_map),
            const((D_MODEL, D_MODEL)), const((1, D_MODEL)), const((D_MODEL, LANES)), const((D_MODEL, LANES)),
            const((1, LANES)),
        ],
        out_specs=[pl.BlockSpec((tm, D_MODEL), lambda i: (i, 0)), pl.BlockSpec((tm, LANES), lambda i: (i, 0))],
        out_shape=[jax.ShapeDtypeStruct((t_all, D_MODEL), F32), jax.ShapeDtypeStruct((t_all, LANES), F32)],
        compiler_params=_cparams(("arbitrary",), VMEM_LIMIT),
    )(merged_p, xp, merged_s, xs, w_o, g, wr_hi, wr_lo, br)


def _pack_bf16_pairs(x):
    n = x.shape[1] // 2
    bits = pltpu.bitcast(x.astype(BF16).astype(F32), jnp.uint32)
    return (bits[:, n:] & jnp.uint32(0xFFFF0000)) | (bits[:, :n] >> 16)


def _unpack_bf16_pairs(packed):
    lo = pltpu.bitcast(packed << 16, F32)
    hi = pltpu.bitcast(packed & jnp.uint32(0xFFFF0000), F32)
    return jnp.concatenate([lo, hi], axis=1).astype(BF16)


def _row_copy(src, src_row, dst, dst_row, sem):
    return pltpu.make_async_copy(src.at[pl.ds(src_row, 1), :], dst.at[pl.ds(dst_row, 1), :], sem)


def _dispatch_body(pos_ref, pad_tile_ref, x1_ref, g_ref, xs_hbm, h_scr, sem, *, tile):
    tm = h_scr.shape[0]

    @pl.when(pl.program_id(0) == 0)
    def _():
        h_scr[...] = jnp.zeros_like(h_scr)

        def pad_copy(t, k):
            row = pl.multiple_of(t * tile + k * tm, tm)
            return pltpu.make_async_copy(h_scr, xs_hbm.at[pl.ds(row, tm), :], sem)

        n_used = pad_tile_ref[N_EXPERTS]
        for phase in ("start", "wait"):
            def expert_pad(e, carry, phase=phase):
                @pl.when(pad_tile_ref[e] >= 0)
                def _():
                    for k in range(tile // tm):
                        getattr(pad_copy(pad_tile_ref[e], k), phase)()
                return carry
            lax.fori_loop(0, N_EXPERTS, expert_pad, 0)

            def unused_tile(t, carry, phase=phase):
                @pl.when(t >= n_used)
                def _():
                    for k in range(tile // tm):
                        getattr(pad_copy(t, k), phase)()
                return carry
            lax.fori_loop(0, xs_hbm.shape[0] // tile, unused_tile, 0)

    h_scr[...] = _pack_bf16_pairs(_rmsnorm(x1_ref[...], g_ref[...]))

    def issue(j, carry):
        _row_copy(h_scr, j, xs_hbm, pos_ref[0, 0, 2 * j], sem).start()
        _row_copy(h_scr, j, xs_hbm, pos_ref[0, 0, 2 * j + 1], sem).start()
        return carry

    lax.fori_loop(0, tm, issue, 0, unroll=8)

    def drain(j, carry):
        _row_copy(h_scr, j, xs_hbm, pos_ref[0, 0, 2 * j], sem).wait()
        _row_copy(h_scr, j, xs_hbm, pos_ref[0, 0, 2 * j + 1], sem).wait()
        return carry

    lax.fori_loop(0, tm, drain, 0, unroll=8)


def _dispatch_call(pos3, pad_tile, x1, g, rows, tm, tile):
    t = x1.shape[0]
    assert tile % tm == 0
    return pl.pallas_call(
        functools.partial(_dispatch_body, tile=tile),
        name="dispatch",
        grid=(t // tm,),
        in_specs=[
            pl.BlockSpec((1, 1, 2 * tm), lambda i: (i, 0, 0), memory_space=pltpu.SMEM),
            pl.BlockSpec(memory_space=pltpu.SMEM),
            pl.BlockSpec((tm, D_MODEL), lambda i: (i, 0)),
            pl.BlockSpec((1, D_MODEL), lambda i: (0, 0)),
        ],
        out_specs=pl.BlockSpec(memory_space=pl.ANY),
        out_shape=jax.ShapeDtypeStruct((rows, D_MODEL // 2), jnp.uint32),
        scratch_shapes=[pltpu.VMEM((tm, D_MODEL // 2), jnp.uint32), pltpu.SemaphoreType.DMA(())],
        compiler_params=_cparams(("arbitrary",), VMEM_LIMIT),
    )(pos3, pad_tile, x1, g)


def _moe_body(te_ref, tv_ref, nu_ref, xs_ref, wg_ref, wu_ref, wd_ref, ys_ref, wg_scr, wu_scr, wd_scr):
    i = pl.program_id(0)

    @pl.when(jnp.logical_and(i < nu_ref[0], jnp.logical_or(i == 0, te_ref[i] != te_ref[jnp.maximum(i - 1, 0)])))
    def _():
        wg_scr[...] = wg_ref[0].astype(BF16)
        wu_scr[...] = wu_ref[0].astype(BF16)
        wd_scr[...] = wd_ref[0].astype(BF16)

    @pl.when(i < nu_ref[0])
    def _():
        row = lax.broadcasted_iota(jnp.int32, xs_ref.shape, 0)
        x = _unpack_bf16_pairs(jnp.where(row < tv_ref[i], xs_ref[...], jnp.uint32(0)))
        a = jax.nn.silu(_dot(x, wg_scr[...])) * _dot(x, wu_scr[...])
        ys_ref[...] = _dot(a.astype(BF16), wd_scr[...])

    @pl.when(i >= nu_ref[0])
    def _():
        ys_ref[...] = jnp.zeros_like(ys_ref)


def _moe_call(tile_expert, tile_valid, n_used, xs, w_gate, w_up, w_down, tm):
    rows = xs.shape[0]

    def row_map(i, te, tv, nu):
        return (jnp.minimum(i, nu[0] - 1), 0)

    def out_map(i, te, tv, nu):
        return (i, 0)

    def w_map(i, te, tv, nu):
        return (te[i], 0, 0)

    return pl.pallas_call(
        _moe_body,
        name="moe",
        grid_spec=pltpu.PrefetchScalarGridSpec(
            num_scalar_prefetch=3,
            grid=(rows // tm,),
            in_specs=[
                pl.BlockSpec((tm, D_MODEL // 2), row_map),
                pl.BlockSpec((1, D_MODEL, D_EXPERT), w_map),
                pl.BlockSpec((1, D_MODEL, D_EXPERT), w_map),
                pl.BlockSpec((1, D_EXPERT, D_MODEL), w_map),
            ],
            out_specs=pl.BlockSpec((tm, D_MODEL), out_map),
            scratch_shapes=[pltpu.VMEM((D_MODEL, D_EXPERT), BF16), pltpu.VMEM((D_MODEL, D_EXPERT), BF16),
                            pltpu.VMEM((D_EXPERT, D_MODEL), BF16)],
        ),
        out_shape=jax.ShapeDtypeStruct((rows, D_MODEL), F32),
        compiler_params=_cparams(("arbitrary",), VMEM_LIMIT),
    )(tile_expert, tile_valid, n_used, xs, w_gate, w_up, w_down)


def _combine_body(pos_ref, x1_ref, route_ref, ys_hbm, y_ref, buf0, buf1, sem):
    tm = buf0.shape[0]

    def issue(j, carry):
        _row_copy(ys_hbm, pos_ref[0, 0, 2 * j], buf0, j, sem).start()
        _row_copy(ys_hbm, pos_ref[0, 0, 2 * j + 1], buf1, j, sem).start()
        return carry

    lax.fori_loop(0, tm, issue, 0, unroll=8)

    def drain(j, carry):
        _row_copy(ys_hbm, pos_ref[0, 0, 2 * j], buf0, j, sem).wait()
        _row_copy(ys_hbm, pos_ref[0, 0, 2 * j + 1], buf1, j, sem).wait()
        return carry

    lax.fori_loop(0, tm, drain, 0, unroll=8)
    route = route_ref[...]
    y_ref[...] = x1_ref[...] + (route[:, 2:3] * buf0[...] + route[:, 3:4] * buf1[...])


def _combine_call(name, pos3, x1_all, route, ys, t, row0, tm):
    off = row0 // tm
    return pl.pallas_call(
        _combine_body,
        name=name,
        grid=(t // tm,),
        in_specs=[
            pl.BlockSpec((1, 1, 2 * tm), lambda i: (i + off, 0, 0), memory_space=pltpu.SMEM),
            pl.BlockSpec((tm, D_MODEL), lambda i: (i + off, 0)),
            pl.BlockSpec((tm, LANES), lambda i: (i + off, 0)),
            pl.BlockSpec(memory_space=pl.ANY),
        ],
        out_specs=pl.BlockSpec((tm, D_MODEL), lambda i: (i, 0)),
        out_shape=jax.ShapeDtypeStruct((t, D_MODEL), F32),
        scratch_shapes=[pltpu.VMEM((tm, D_MODEL), F32), pltpu.VMEM((tm, D_MODEL), F32), pltpu.SemaphoreType.DMA(())],
        compiler_params=_cparams(("arbitrary",), VMEM_LIMIT),
    )(pos3, x1_all, route, ys)


def _routing_offsets(route, tm, n_tiles):
    experts = jnp.arange(N_EXPERTS, dtype=jnp.int32)[None, :]
    e = route[:, :2].astype(jnp.int32)
    hot = [e[:, k:k + 1] == experts for k in range(2)]
    onehot = hot[0].astype(jnp.int32) + hot[1].astype(jnp.int32)
    csum = jnp.cumsum(onehot, axis=0)
    counts = csum[-1]
    tiles_e = (counts + tm - 1) // tm
    tile_end = jnp.cumsum(tiles_e)
    tile_start = tile_end - tiles_e
    row0 = (csum - onehot) + (tile_start * tm)[None, :]
    pos = jnp.stack([jnp.sum(jnp.where(h, row0, 0), axis=1) for h in hot], axis=1)
    n_used = tile_end[-1]
    tile_id = jnp.minimum(jnp.arange(n_tiles, dtype=jnp.int32), n_used - 1)
    tile_expert = jnp.sum((tile_id[:, None] >= tile_end[None, :]).astype(jnp.int32), axis=1)
    tile_hot = tile_expert[:, None] == experts
    seg_end = jnp.sum(jnp.where(tile_hot, (tile_start * tm + counts)[None, :], 0), axis=1)
    tile_valid = jnp.clip(seg_end - tile_id * tm, 0, tm)
    pad_tile = jnp.concatenate([jnp.where(tiles_e > 0, tile_end - 1, -1), n_used.reshape(1)])
    i32 = jnp.int32
    return (pos.astype(i32), tile_expert.astype(i32), tile_valid.astype(i32), n_used.reshape(1).astype(i32),
            pad_tile.astype(i32))


def _pick_tile(t, pref):
    tm = min(pref, t)
    assert t % tm == 0, (t, tm)
    return tm


def kernel(x_prompt, x_sample, cache_a_kv, cache_b1_kv, cache_b2_kv, cache_b3_kv, g_attn_norm, w_in, q_norm_a, k_norm_a, q_norm_b, k_norm_b, sinks_a, w_branch_a, w_branch_b, w_out, g_ffn_norm, w_group_router, b_group_router, w_expert_router, b_expert_router, w_expert_gate, w_expert_up, w_expert_down):
    n, s, _ = x_prompt.shape
    ns, ts, _ = x_sample.shape
    assert ts == SAMPLE_T and s == B_PATTERNS[-1][0] and x_prompt.shape[2] == D_MODEL
    assert g_attn_norm.shape[0] == 1, "single layer"
    tp, tsmp = n * s, ns * ts
    t_all = tp + tsmp
    xp = x_prompt.reshape(tp, D_MODEL)
    xs_in = x_sample.reshape(tsmp, D_MODEL)

    w = w_in[0].astype(BF16)
    blk = A_Q_W + 2 * A_KV_W
    assert blk == B_W and blk % PROJ_SEG == 0 and B_GW == PROJ_SEG
    rep = LANES // HEAD_DIM
    gains = jnp.stack([jnp.tile(v[0], rep) for v in (q_norm_a, k_norm_a, q_norm_b, k_norm_b)]).astype(F32)
    g_attn = g_attn_norm.astype(F32)
    g_ffn = g_ffn_norm.astype(F32)
    plan_a = ([(0, c, PROJ_SEG, 0) for c in range(0, A_Q_W, PROJ_SEG)],
              [(0, A_Q_W, A_KV_W, 1), (0, A_Q_W + A_KV_W, A_KV_W, -1)])
    plan_q = ([(0, c, PROJ_SEG, 2) for c in range(0, B_W, PROJ_SEG)],)
    plan_kv = tuple([(0, g * B_GW, B_GW, 3), (1, g * B_GW, B_GW, -1)] for g in range(B_GROUPS))

    def project(x2d, pos, tm, transposed_seq=None):
        tables = _rope_tables(pos)
        qa, kva = _proj_call("proj_a", x2d, g_attn, w, [(blk, 0)], gains, tables, plan_a, tm)
        (qb,) = _proj_call("proj_qb", x2d, g_attn, w, [(blk, 1)], gains, tables, plan_q, tm)
        kvb = _proj_call("proj_kvb", x2d, g_attn, w, [(blk, 2), (blk, 3)], gains, tables, plan_kv, tm, transposed_seq)
        return qa, kva, qb, kvb

    tm_p = _pick_tile(s, ROW_TILE)
    assert B_PATTERNS[-1][0] >= s, "the last B group's new buffer is the whole sequence, written by proj_kvb"
    qa, kva, qb, kvb = project(xp, jnp.arange(s, dtype=jnp.int32), tm_p, s)
    kvb, kvb_last_t = kvb[:B_GROUPS], kvb[B_GROUPS]
    sinks = sinks_a[0].astype(F32)
    oa_p = _attn_a_call(qa, kva, sinks, n, s)
    ob_p = _attn_b_call(qb, kvb, n, s)

    def to_feature_major(c, heads):
        return jnp.transpose(c, (0, 2, 3, 4, 1)).reshape(c.shape[0], 2 * heads * HEAD_DIM, c.shape[1])

    def from_feature_major(c, heads):
        return jnp.transpose(c.reshape(c.shape[0], 2, heads, HEAD_DIM, c.shape[2]), (0, 4, 1, 2, 3))[None]

    new_a_p = from_feature_major(_tail_transpose_call("tail_a", kva, n, s, min(A_WINDOW, s)), A_KV_HEADS)
    new_b_p = [from_feature_major(_tail_transpose_call("tail_b%d" % g, kvb[g], n, s, min(win, s)), B_HPG)
               for g, (win, _) in enumerate(B_PATTERNS[:-1])]
    new_b_p.append(from_feature_major(kvb_last_t, B_HPG))

    tm_s = _pick_tile(tsmp, ROW_TILE)
    pos_s = PAST_LEN + (jnp.arange(tm_s, dtype=jnp.int32) % ts)
    qa_s, kva_s, qb_s, kvb_s = project(xs_in, pos_s, tm_s)
    caches = [to_feature_major(cache_a_kv[0], A_KV_HEADS)]
    caches += [to_feature_major(c[0], B_HPG) for c in (cache_b1_kv, cache_b2_kv, cache_b3_kv)]
    oa_s, ob_s, na, nb1, nb2, nb3 = _sample_call(sinks, qa_s, kva_s, qb_s, kvb_s, caches, ns)
    new_a_s = from_feature_major(na, A_KV_HEADS)
    new_b_s = [from_feature_major(c, B_HPG) for c in (nb1, nb2, nb3)]

    w_ba = w_branch_a[0].astype(BF16)
    w_bb = w_branch_b[0].astype(BF16)
    w_o = w_out[0].astype(BF16)
    wr = jnp.zeros((D_MODEL, LANES), F32)
    wr = wr.at[:, :N_EXPERT_GROUPS].set(w_group_router[0]).at[:, N_EXPERT_GROUPS:N_EXPERT_GROUPS + N_EXPERTS].set(w_expert_router[0])
    wr_hi = wr.astype(BF16)
    wr_lo = (wr - wr_hi.astype(F32)).astype(BF16)
    br = jnp.zeros((1, LANES), F32)
    br = br.at[0, :N_EXPERT_GROUPS].set(b_group_router[0]).at[0, N_EXPERT_GROUPS:N_EXPERT_GROUPS + N_EXPERTS].set(b_expert_router[0])

    tn = 512
    tm_mp, tm_ms = _pick_tile(tp, 1024), _pick_tile(tsmp, 1024)
    merged_p = _merge_call(xp, g_attn, oa_p, ob_p, w, 4 * blk, w_ba, w_bb, tm_mp, tn)
    merged_s = _merge_call(xs_in, g_attn, oa_s, ob_s, w, 4 * blk, w_ba, w_bb, tm_ms, tn)
    tm_o = _pick_tile(tsmp, ROW_TILE)
    assert tp % tm_o == 0
    x1_all, route = _outproj_call(merged_p, xp, merged_s, xs_in, w_o, g_ffn, wr_hi, wr_lo, br, tm_o)

    tm_e = MOE_TILE
    n_tiles = (2 * t_all + N_EXPERTS * (tm_e - 1)) // tm_e + 1
    pos, tile_expert, tile_valid, n_used, pad_tile = _routing_offsets(route, tm_e, n_tiles)
    tm_d = _pick_tile(tsmp, 2 * ROW_TILE)
    pos3 = pos.reshape(t_all // tm_d, 1, 2 * tm_d)
    xs_sorted = _dispatch_call(pos3, pad_tile, x1_all, g_ffn, n_tiles * tm_e, tm_d, tm_e)
    ys = _moe_call(tile_expert, tile_valid, n_used, xs_sorted, w_expert_gate[0], w_expert_up[0], w_expert_down[0], tm_e)
    y_p = _combine_call("combine_p", pos3, x1_all, route, ys, tp, 0, tm_d)
    y_s = _combine_call("combine_s", pos3, x1_all, route, ys, tsmp, tp, tm_d)

    return (y_p.reshape(n, s, D_MODEL), y_s.reshape(ns, ts, D_MODEL),
            new_a_p, new_a_s, new_b_p[0], new_b_s[0], new_b_p[1], new_b_s[1], new_b_p[2], new_b_s[2])
```

```python
import functools

import jax
import jax.numpy as jnp
import numpy as np
from jax import lax
from jax.experimental import pallas as pl
from jax.experimental.pallas import tpu as pltpu

F32 = jnp.float32
BF16 = jnp.bfloat16

D_MODEL = 2048
HEAD_DIM = 64
ROT_DIM = HEAD_DIM // 4
ROPE_THETA = 500000.0
NORM_EPS = 1e-6
BLOCK = 128
PAST_LEN = 16384

A_HEADS = 16
A_KV_HEADS = 4
A_WINDOW = 128
B_PATTERNS = ((128, 1), (512, 4), (2048, 16))
B_GROUPS = 3
B_HPG = 8
A_Q_W = A_HEADS * HEAD_DIM
A_KV_W = A_KV_HEADS * HEAD_DIM
B_GW = B_HPG * HEAD_DIM
B_W = B_GROUPS * B_GW

N_EXPERT_GROUPS = 4
EXPERTS_PER_GROUP = 8
N_EXPERTS = N_EXPERT_GROUPS * EXPERTS_PER_GROUP
D_EXPERT = 512

LANES = 128
SUBLANES = 8
VMEM_LIMIT = 56 * 1024 * 1024
SAMPLE_T = 8

MOE_TILE = 512
ROW_TILE = 256


def _cparams(sem, vmem=None):
    return pltpu.CompilerParams(dimension_semantics=sem, vmem_limit_bytes=vmem)


def _rmsnorm(x, g):
    return x * lax.rsqrt(jnp.mean(x * x, axis=-1, keepdims=True) + NORM_EPS) * g


def _split2(v):
    hi = v.astype(BF16)
    lo = (v - hi.astype(F32)).astype(BF16)
    return hi, lo


def _split3(v):
    hi = v.astype(BF16)
    r = v - hi.astype(F32)
    mid = r.astype(BF16)
    lo = (r - mid.astype(F32)).astype(BF16)
    return hi, mid, lo


def _dot(a, b):
    return jnp.dot(a, b, preferred_element_type=F32)


def _multiple_of(x, m):
    return x if isinstance(x, int) else pl.multiple_of(x, m)


def _dot_nt(a, b):
    return lax.dot_general(a, b, (((1,), (1,)), ((), ())), preferred_element_type=F32)


def _rope_tables(pos):
    half = ROT_DIM // 2
    inv_freq = ROPE_THETA ** (-jnp.arange(half, dtype=F32) / half)
    ang = pos.astype(F32)[:, None] * inv_freq[None, :]
    cos, sin = jnp.cos(ang), jnp.sin(ang)
    p = pos.shape[0]
    ones = jnp.ones((p, HEAD_DIM - ROT_DIM), F32)
    z8 = jnp.zeros((p, half), F32)
    z48 = jnp.zeros((p, HEAD_DIM - ROT_DIM), F32)
    c = jnp.concatenate([cos, cos, ones], axis=1)
    s_up = jnp.concatenate([-sin, z8, z48], axis=1)
    s_dn = jnp.concatenate([z8, sin, z48], axis=1)
    rep = LANES // HEAD_DIM
    return jnp.tile(c, (1, rep)), jnp.tile(s_up, (1, rep)), jnp.tile(s_dn, (1, rep))


PROJ_SEG = 512


def _proj_body(*refs, plan, n_w, transposed):
    x_ref, g_ref = refs[:2]
    w_refs = refs[2:2 + n_w]
    gain_ref, c_ref, su_ref, sd_ref = refs[2 + n_w:6 + n_w]
    out_refs = refs[6 + n_w:]
    h = _rmsnorm(x_ref[...], g_ref[...]).astype(BF16)
    low = lax.broadcasted_iota(jnp.int32, (x_ref.shape[0], LANES), 1) < HEAD_DIM
    for oi, segments in enumerate(plan):
        o_ref = out_refs[oi]
        ocol = 0
        for widx, col0, width, kind in segments:
            z_all = _dot(h, w_refs[widx][:, col0:col0 + width])
            for c in range(width // LANES):
                z = z_all[:, c * LANES:(c + 1) * LANES]
                if kind >= 0:
                    zz = z * z
                    ss = jnp.where(low, jnp.sum(jnp.where(low, zz, 0.0), axis=1, keepdims=True),
                                   jnp.sum(jnp.where(low, 0.0, zz), axis=1, keepdims=True))
                    z = z * lax.rsqrt(ss * (1.0 / HEAD_DIM) + NORM_EPS) * gain_ref[kind:kind + 1, :]
                    z = (z * c_ref[...] + pltpu.roll(z, LANES - ROT_DIM // 2, 1) * su_ref[...]
                         + pltpu.roll(z, ROT_DIM // 2, 1) * sd_ref[...])
                    if kind % 2 == 0:
                        z = z * (HEAD_DIM ** -0.5)
                o_ref[:, ocol:ocol + LANES] = z
                if transposed and oi == len(plan) - 1:
                    out_refs[-1][ocol:ocol + LANES, :] = z.T
                ocol += LANES


def _proj_call(name, x2d, g, w_full, w_blocks, gains, tables, plan, tm, transposed_seq=None):
    t = x2d.shape[0]
    pos_blocks = tables[0].shape[0] // tm
    tab_spec = pl.BlockSpec((tm, LANES), lambda i: (i % pos_blocks, 0))
    widths = [sum(seg[2] for seg in segments) for segments in plan]
    out_specs = [pl.BlockSpec((tm, w_), lambda i: (i, 0)) for w_ in widths]
    out_shape = [jax.ShapeDtypeStruct((t, w_), F32) for w_ in widths]
    if transposed_seq is not None:
        per_seq = transposed_seq // tm
        out_specs.append(pl.BlockSpec((None, widths[-1], tm), lambda i: (i // per_seq, 0, i % per_seq)))
        out_shape.append(jax.ShapeDtypeStruct((t // transposed_seq, widths[-1], transposed_seq), F32))
    return pl.pallas_call(
        functools.partial(_proj_body, plan=plan, n_w=len(w_blocks), transposed=transposed_seq is not None),
        name=name,
        grid=(t // tm,),
        in_specs=[
            pl.BlockSpec((tm, D_MODEL), lambda i: (i, 0)),
            pl.BlockSpec((1, D_MODEL), lambda i: (0, 0)),
            *[pl.BlockSpec((D_MODEL, bw), lambda i, bj=bj: (0, bj)) for bw, bj in w_blocks],
            pl.BlockSpec((4, LANES), lambda i: (0, 0)),
            tab_spec, tab_spec, tab_spec,
        ],
        out_specs=out_specs,
        out_shape=out_shape,
        compiler_params=_cparams(("parallel",), VMEM_LIMIT),
    )(x2d, g, *([w_full] * len(w_blocks)), gains, *tables)


def _band_mask(rows, nkeys, max_dist, has_prev):
    qi = lax.broadcasted_iota(jnp.int32, (rows, nkeys), 0) & (BLOCK - 1)
    kk = lax.broadcasted_iota(jnp.int32, (rows, nkeys), 1)
    if nkeys == BLOCK:
        dist = qi - kk
        return jnp.logical_and(dist >= 0, dist <= max_dist)
    dist = qi + BLOCK - kk
    band = jnp.logical_and(dist >= 0, dist <= max_dist)
    return jnp.logical_and(band, jnp.logical_or(kk >= BLOCK, has_prev))


UNIT_ROWS = 2 * BLOCK


def _band_bias(nkeys, max_dist, has_prev):
    return jnp.where(_band_mask(UNIT_ROWS, nkeys, max_dist, has_prev), 0.0, -jnp.inf).astype(F32)


def _softmax_stage(s_ref, bias, sink, p_ref, t_ref):
    s = s_ref[...] + bias
    m = jnp.max(s, axis=1, keepdims=True)
    if sink is not None:
        m = jnp.maximum(m, sink)
    p = jnp.exp(s - m)
    denom = jnp.sum(p, axis=1, keepdims=True)
    if sink is not None:
        denom = denom + jnp.exp(sink - m)
    p_ref[...] = p.astype(BF16)
    t_ref[:, :LANES] = jnp.broadcast_to(1.0 / denom, (UNIT_ROWS, LANES))
    t_ref[:, LANES:] = jnp.broadcast_to(m + jnp.log(denom), (UNIT_ROWS, LANES))


def _pipeline(n_units, qk, sm, pv):
    qk(0, 0)
    qk(1, 1)
    sm(0, 0)

    def body(j, carry):
        i = 2 * j
        pv(i - 2, 0)
        sm(i - 1, 1)
        qk(i, 0)
        pv(i - 1, 1)
        sm(i, 0)
        qk(i + 1, 1)
        return carry

    lax.fori_loop(1, n_units // 2, body, 0)
    pv(n_units - 2, 0)
    sm(n_units - 1, 1)
    pv(n_units - 1, 1)


def _attn_a_body(sink_ref, q_ref, kv_ref, o_ref, s_scr, p_scr, t_scr, bias_scr, *, s):
    nb = s // BLOCK
    low = lax.broadcasted_iota(jnp.int32, (BLOCK, LANES), 1) < HEAD_DIM
    high = jnp.logical_not(low)
    top = lax.broadcasted_iota(jnp.int32, (UNIT_ROWS, 1), 0) < BLOCK
    bias_scr[0] = _band_bias(2 * BLOCK, A_WINDOW - 1, False)
    bias_scr[1] = _band_bias(2 * BLOCK, A_WINDOW - 1, True)
    pairs = A_KV_HEADS // 2
    per_pair = A_HEADS // A_KV_HEADS
    units = [(c, hp) for c in range(pairs) for hp in range(per_pair)]

    def rows2(b, lanes):
        start_c = _multiple_of(b * BLOCK, BLOCK)
        start_p = _multiple_of(jnp.maximum(b - 1, 0) * BLOCK, BLOCK)
        return jnp.concatenate([kv_ref[pl.ds(start_p, BLOCK), lanes], kv_ref[pl.ds(start_c, BLOCK), lanes]], axis=0)

    def qk(b, slot):
        start_c = _multiple_of(b * BLOCK, BLOCK)
        for c in range(pairs):
            k2 = rows2(b, slice(c * LANES, (c + 1) * LANES)).astype(BF16)
            for hp in range(per_pair):
                col = c * per_pair + hp
                q = q_ref[pl.ds(start_c, BLOCK), col * LANES:(col + 1) * LANES]
                qr = pltpu.roll(q, HEAD_DIM, 1)
                if hp < per_pair // 2:
                    lhs = jnp.concatenate([jnp.where(low, q, 0.0), jnp.where(low, qr, 0.0)], axis=0)
                else:
                    lhs = jnp.concatenate([jnp.where(high, qr, 0.0), jnp.where(high, q, 0.0)], axis=0)
                s_scr[slot, units.index((c, hp))] = _dot_nt(lhs.astype(BF16), k2)

    def sm(b, slot):
        bias = bias_scr[jnp.minimum(b, 1)]
        for ui, (c, hp) in enumerate(units):
            col = c * per_pair + hp
            sink = jnp.where(top, sink_ref[2 * col], sink_ref[2 * col + 1])
            _softmax_stage(s_scr.at[slot, ui], bias, sink, p_scr.at[slot, ui], t_scr.at[slot, ui])

    def pv(b, slot):
        start_c = _multiple_of(b * BLOCK, BLOCK)
        for c in range(pairs):
            v2 = rows2(b, slice(A_KV_W + c * LANES, A_KV_W + (c + 1) * LANES)).astype(BF16)
            for hp in range(per_pair):
                ui = units.index((c, hp))
                col = c * per_pair + hp
                o2 = _dot(p_scr[slot, ui], v2) * t_scr[slot, ui, :, :LANES]
                if hp < per_pair // 2:
                    o = jnp.where(low, o2[:BLOCK], pltpu.roll(o2[BLOCK:], HEAD_DIM, 1))
                else:
                    o = jnp.where(low, pltpu.roll(o2[:BLOCK], HEAD_DIM, 1), o2[BLOCK:])
                o_ref[pl.ds(start_c, BLOCK), col * LANES:(col + 1) * LANES] = o.astype(o_ref.dtype)

    _pipeline(nb, qk, sm, pv)


def _attn_a_call(qa, kva, sinks, n, s):
    n_units = (A_KV_HEADS // 2) * (A_HEADS // A_KV_HEADS)
    return pl.pallas_call(
        functools.partial(_attn_a_body, s=s),
        name="attn_a",
        grid=(n,),
        in_specs=[
            pl.BlockSpec(memory_space=pltpu.SMEM),
            pl.BlockSpec((s, A_Q_W), lambda i: (i, 0)),
            pl.BlockSpec((s, 2 * A_KV_W), lambda i: (i, 0)),
        ],
        out_specs=pl.BlockSpec((s, A_Q_W), lambda i: (i, 0)),
        out_shape=jax.ShapeDtypeStruct((n * s, A_Q_W), BF16),
        scratch_shapes=[pltpu.VMEM((2, n_units, UNIT_ROWS, 2 * BLOCK), F32),
                        pltpu.VMEM((2, n_units, UNIT_ROWS, 2 * BLOCK), BF16),
                        pltpu.VMEM((2, n_units, UNIT_ROWS, 2 * LANES), F32),
                        pltpu.VMEM((2, UNIT_ROWS, 2 * BLOCK), F32)],
        compiler_params=_cparams(("parallel",), VMEM_LIMIT),
    )(sinks, qa, kva)


def _attn_b_body(q1, q2, q3, k1, v1, k2, v2, k3, v3, o_ref, o_scr, l_scr, s_scr, p_scr, t_scr, bias_scr, *, s):
    low = lax.broadcasted_iota(jnp.int32, (BLOCK, LANES), 1) < HEAD_DIM
    bias_scr[0] = _band_bias(2 * BLOCK, BLOCK, False)
    bias_scr[1] = _band_bias(2 * BLOCK, BLOCK, True)
    bias_cur = _band_bias(BLOCK, BLOCK, False)
    for g, ((_, dil), q_ref, k_ref, v_ref) in enumerate(zip(B_PATTERNS, (q1, q2, q3), (k1, k2, k3), (v1, v2, v3))):
        span = BLOCK * dil
        nblk = max(s // span, 1)
        use_prev = nblk > 1
        nk = 2 * BLOCK if use_prev else BLOCK

        def rows(ref, start, dil=dil):
            if dil == 1:
                return ref[pl.ds(_multiple_of(start, BLOCK), BLOCK), :]
            return ref[pl.ds(start, BLOCK, stride=dil), :]

        def starts(u, dil=dil, span=span):
            b = u // dil
            return b * span + u % dil, jnp.maximum(b - 1, 0) * span + u % dil, b

        def rows2(ref, u, rows=rows, starts=starts, use_prev=use_prev):
            start_c, start_p, _ = starts(u)
            if use_prev:
                return jnp.concatenate([rows(ref, start_p), rows(ref, start_c)], axis=0)
            return rows(ref, start_c)

        def qk(u, slot, q_ref=q_ref, k_ref=k_ref, rows=rows, starts=starts, rows2=rows2, nk=nk):
            q = rows(q_ref, starts(u)[0])
            lhs = jnp.concatenate([jnp.where(low, q, 0.0), jnp.where(low, 0.0, q)], axis=0).astype(BF16)
            s_scr[slot, :, :nk] = _dot_nt(lhs, rows2(k_ref, u).astype(BF16))

        def sm(u, slot, starts=starts, use_prev=use_prev, nk=nk):
            bias = bias_scr[jnp.minimum(starts(u)[2], 1)] if use_prev else bias_cur
            _softmax_stage(s_scr.at[slot, :, :nk], bias, None, p_scr.at[slot, :, :nk], t_scr.at[slot])

        def pv(u, slot, g=g, dil=dil, v_ref=v_ref, starts=starts, rows2=rows2, nk=nk):
            o2 = _dot(p_scr[slot, :, :nk], rows2(v_ref, u).astype(BF16)) * t_scr[slot, :, :LANES]
            l2 = t_scr[slot, :, LANES:]
            o = jnp.where(low, o2[:BLOCK], o2[BLOCK:])
            lse = jnp.where(low, l2[:BLOCK], l2[BLOCK:])
            start_c = starts(u)[0]
            if dil == 1:
                idx = pl.ds(_multiple_of(start_c, BLOCK), BLOCK)
            else:
                idx = pl.ds(start_c, BLOCK, stride=dil)
            o_scr[g, idx, :] = o
            l_scr[g, idx, :] = lse

        _pipeline(nblk * dil, qk, sm, pv)

    chunk = 256

    def comb(c, carry):
        r0 = pl.multiple_of(c * chunk, chunk)
        l0, l1, l2 = (l_scr[g, pl.ds(r0, chunk), :] for g in range(B_GROUPS))
        m = jnp.maximum(jnp.maximum(l0, l1), l2)
        w0, w1, w2 = jnp.exp(l0 - m), jnp.exp(l1 - m), jnp.exp(l2 - m)
        inv = 1.0 / (w0 + w1 + w2)
        acc = (o_scr[0, pl.ds(r0, chunk), :] * (w0 * inv) + o_scr[1, pl.ds(r0, chunk), :] * (w1 * inv)
               + o_scr[2, pl.ds(r0, chunk), :] * (w2 * inv))
        o_ref[pl.ds(r0, chunk), :] = acc.astype(o_ref.dtype)
        return carry

    lax.fori_loop(0, s // chunk, comb, 0)


def _attn_b_call(qb, kvs, n, s):
    pairs = B_GW // LANES
    in_specs = [pl.BlockSpec((s, LANES), lambda i, h, g=g: (i, g * pairs + h)) for g in range(B_GROUPS)]
    args = [qb, qb, qb]
    for kv in kvs:
        in_specs.append(pl.BlockSpec((s, LANES), lambda i, h: (i, h)))
        in_specs.append(pl.BlockSpec((s, LANES), lambda i, h: (i, pairs + h)))
        args += [kv, kv]
    return pl.pallas_call(
        functools.partial(_attn_b_body, s=s),
        name="attn_b",
        grid=(n, pairs),
        in_specs=in_specs,
        out_specs=pl.BlockSpec((s, LANES), lambda i, h: (i, h)),
        out_shape=jax.ShapeDtypeStruct((n * s, B_GW), BF16),
        scratch_shapes=[pltpu.VMEM((B_GROUPS, s, LANES), F32), pltpu.VMEM((B_GROUPS, s, LANES), F32),
                        pltpu.VMEM((2, UNIT_ROWS, 2 * BLOCK), F32), pltpu.VMEM((2, UNIT_ROWS, 2 * BLOCK), BF16),
                        pltpu.VMEM((2, UNIT_ROWS, 2 * LANES), F32), pltpu.VMEM((2, UNIT_ROWS, 2 * BLOCK), F32)],
        compiler_params=_cparams(("parallel", "arbitrary"), VMEM_LIMIT),
    )(*args)


def _shift_insert(old_ref, new_ref, out_ref, w):
    ncol = w // LANES
    keep = lax.broadcasted_iota(jnp.int32, (BLOCK, LANES), 1) < LANES - SAMPLE_T

    def strip(k, carry):
        r0 = pl.multiple_of(k * BLOCK, BLOCK)
        x = old_ref[pl.ds(r0, BLOCK), :]
        pad = jnp.concatenate([jnp.zeros((BLOCK - SAMPLE_T, BLOCK), F32), new_ref[:, pl.ds(r0, BLOCK)]], axis=0)
        nxt = pad.T
        for j in reversed(range(ncol)):
            cur = pltpu.roll(x[:, j * LANES:(j + 1) * LANES], LANES - SAMPLE_T, 1)
            out_ref[pl.ds(r0, BLOCK), j * LANES:(j + 1) * LANES] = jnp.where(keep, cur, nxt)
            nxt = cur
        return carry

    lax.fori_loop(0, old_ref.shape[0] // BLOCK, strip, 0)


def _window_scores(lhs, buf_ref, old_ref, krow, w, dil, min_old):
    r = lhs.shape[0]
    t_main = lax.broadcasted_iota(jnp.int32, (r, w), 0) & (SAMPLE_T - 1)
    delta = (w - SAMPLE_T) + t_main - lax.broadcasted_iota(jnp.int32, (r, w), 1)
    valid_main = jnp.logical_and(delta >= 0, (delta & (dil - 1)) == 0)
    t_old = lax.broadcasted_iota(jnp.int32, (r, LANES), 0) & (SAMPLE_T - 1)
    c_old = lax.broadcasted_iota(jnp.int32, (r, LANES), 1)
    valid_old = jnp.logical_and(jnp.logical_and(c_old < SAMPLE_T, c_old >= t_old + min_old),
                                ((w + t_old - c_old) & (dil - 1)) == 0)
    kt = buf_ref[krow:krow + LANES, :].astype(BF16)
    kx = old_ref[krow:krow + LANES, 0:LANES].astype(BF16)
    return jnp.where(valid_main, _dot(lhs, kt), -jnp.inf), jnp.where(valid_old, _dot(lhs, kx), -jnp.inf)


def _window_softmax(s, sx, sink):
    m = jnp.maximum(jnp.max(s, axis=1, keepdims=True), jnp.max(sx, axis=1, keepdims=True))
    if sink is not None:
        m = jnp.maximum(m, sink)
    p = jnp.exp(s - m)
    px = jnp.exp(sx - m)
    denom = jnp.sum(p, axis=1, keepdims=True) + jnp.sum(px, axis=1, keepdims=True)
    if sink is not None:
        denom = denom + jnp.exp(sink - m)
    inv = 1.0 / denom
    return (p * inv).astype(BF16), (px * inv).astype(BF16), m + jnp.log(denom)


def _window_values(p, px, buf_ref, old_ref, vrow):
    vt = buf_ref[vrow:vrow + LANES, :].astype(BF16)
    vx = old_ref[vrow:vrow + LANES, 0:LANES].astype(BF16)
    return _dot_nt(p, vt) + _dot_nt(px, vx)


def _sample_body(sink_ref, qa_ref, kva_ref, qb_ref, kvb1_ref, kvb2_ref, kvb3_ref, ca_ref, cb1_ref, cb2_ref, cb3_ref,
                 oa_ref, ob_ref, na_ref, nb1_ref, nb2_ref, nb3_ref):
    low = lax.broadcasted_iota(jnp.int32, (SAMPLE_T, LANES), 1) < HEAD_DIM
    new_refs = (kvb1_ref, kvb2_ref, kvb3_ref)
    old_refs = (cb1_ref, cb2_ref, cb3_ref)
    buf_refs = (nb1_ref, nb2_ref, nb3_ref)
    _shift_insert(ca_ref, kva_ref, na_ref, A_WINDOW)
    for g, (win, _) in enumerate(B_PATTERNS):
        _shift_insert(old_refs[g], new_refs[g], buf_refs[g], win)

    units = []
    qa = qa_ref[...]
    heads_per_pair = 2 * (A_HEADS // A_KV_HEADS)
    for c in range(A_KV_HEADS // 2):
        blocks, sinks = [], []
        for jj in range(heads_per_pair):
            j = c * heads_per_pair + jj
            want_low = jj < heads_per_pair // 2
            q = qa[:, (j // 2) * LANES:(j // 2 + 1) * LANES]
            if (j % 2 == 0) != want_low:
                q = pltpu.roll(q, HEAD_DIM, 1)
            blocks.append(jnp.where(low if want_low else jnp.logical_not(low), q, 0.0))
            sinks.append(jnp.full((SAMPLE_T, 1), sink_ref[j], F32))
        units.append((jnp.concatenate(blocks, axis=0).astype(BF16), na_ref, ca_ref, c * LANES, A_KV_W + c * LANES,
                      A_WINDOW, 1, 1, jnp.concatenate(sinks, axis=0)))
    qb = qb_ref[...]
    pairs = B_GW // LANES
    for g, (win, dil) in enumerate(B_PATTERNS):
        for c in range(pairs):
            q = qb[:, g * B_GW + c * LANES:g * B_GW + (c + 1) * LANES]
            lhs = jnp.concatenate([jnp.where(low, q, 0.0), jnp.where(low, 0.0, q)], axis=0).astype(BF16)
            units.append((lhs, buf_refs[g], old_refs[g], c * LANES, B_GW + c * LANES, win, dil, 0, None))

    scores = [_window_scores(lhs, buf, old, krow, w, dil, min_old) for lhs, buf, old, krow, _, w, dil, min_old, _ in units]
    probs = [_window_softmax(s, sx, u[8]) for (s, sx), u in zip(scores, units)]
    results = [(_window_values(p, px, u[1], u[2], u[4]), lse) for (p, px, lse), u in zip(probs, units)]

    outs = {}
    for c in range(A_KV_HEADS // 2):
        o = results[c][0]
        for jj in range(heads_per_pair):
            j = c * heads_per_pair + jj
            ob = o[jj * SAMPLE_T:(jj + 1) * SAMPLE_T]
            if (j % 2 == 0) != (jj < heads_per_pair // 2):
                ob = pltpu.roll(ob, HEAD_DIM, 1)
            outs[j] = ob
    for c in range(A_HEADS // 2):
        oa_ref[:, c * LANES:(c + 1) * LANES] = jnp.where(low, outs[2 * c], outs[2 * c + 1])

    o_g, l_g = [], []
    for g in range(B_GROUPS):
        o_cols, l_cols = [], []
        for c in range(pairs):
            o, lse = results[A_KV_HEADS // 2 + g * pairs + c]
            lse = jnp.broadcast_to(lse, (2 * SAMPLE_T, LANES))
            o_cols.append(jnp.where(low, o[:SAMPLE_T], o[SAMPLE_T:]))
            l_cols.append(jnp.where(low, lse[:SAMPLE_T], lse[SAMPLE_T:]))
        o_g.append(jnp.concatenate(o_cols, axis=1))
        l_g.append(jnp.concatenate(l_cols, axis=1))
    m = jnp.maximum(jnp.maximum(l_g[0], l_g[1]), l_g[2])
    w0, w1, w2 = jnp.exp(l_g[0] - m), jnp.exp(l_g[1] - m), jnp.exp(l_g[2] - m)
    inv = 1.0 / (w0 + w1 + w2)
    ob_ref[...] = o_g[0] * (w0 * inv) + o_g[1] * (w1 * inv) + o_g[2] * (w2 * inv)


def _sample_call(sinks, qa, kva, qb, kvbs, caches, ns):
    def rows(width):
        return pl.BlockSpec((SAMPLE_T, width), lambda i: (i, 0))

    def cache(c):
        return pl.BlockSpec((None,) + c.shape[1:], lambda i: (i, 0, 0))

    news = [qa, kva, qb, *kvbs]
    in_specs = [pl.BlockSpec(memory_space=pltpu.SMEM)] + [rows(a.shape[1]) for a in news] + [cache(c) for c in caches]
    out_shape = ([jax.ShapeDtypeStruct((ns * SAMPLE_T, A_Q_W), F32), jax.ShapeDtypeStruct((ns * SAMPLE_T, B_GW), F32)]
                 + [jax.ShapeDtypeStruct(c.shape, F32) for c in caches])
    out_specs = [rows(A_Q_W), rows(B_GW)] + [cache(c) for c in caches]
    return pl.pallas_call(
        _sample_body,
        name="sample_attn",
        grid=(ns,),
        in_specs=in_specs,
        out_specs=out_specs,
        out_shape=out_shape,
        compiler_params=_cparams(("parallel",), VMEM_LIMIT),
    )(sinks, *news, *caches)


def _tail_transpose_body(x_ref, o_ref):
    o_ref[...] = x_ref[...].T


def _tail_transpose_call(name, kv, n, s, win):
    c = kv.shape[1]
    tt = min(win, 2 * LANES)
    per_seq, first = s // tt, (s - win) // tt
    return pl.pallas_call(
        _tail_transpose_body,
        name=name,
        grid=(n, win // tt),
        in_specs=[pl.BlockSpec((tt, c), lambda i, j: (i * per_seq + first + j, 0))],
        out_specs=pl.BlockSpec((None, c, tt), lambda i, j: (i, 0, j)),
        out_shape=jax.ShapeDtypeStruct((n, c, win), F32),
        compiler_params=_cparams(("parallel", "parallel")),
    )(kv)


def _merge_body(x_ref, g_ref, oa_ref, ob_ref, wga_ref, wgb_ref, wba_ref, wbb_ref, o_ref, h_scr):
    @pl.when(pl.program_id(1) == 0)
    def _():
        h_scr[...] = _rmsnorm(x_ref[...], g_ref[...]).astype(BF16)

    h = h_scr[...]
    oa = oa_ref[...].astype(BF16)
    ob = ob_ref[...].astype(BF16)
    half = o_ref.shape[1] // 2
    for c in range(2):
        cols = slice(c * half, (c + 1) * half)
        ga = _dot(h, wga_ref[:, cols])
        gb = _dot(h, wgb_ref[:, cols])
        ya = _dot(oa, wba_ref[:, cols])
        yb = _dot(ob, wbb_ref[:, cols])
        o_ref[:, cols] = (jax.nn.sigmoid(ga) * ya + jax.nn.sigmoid(gb) * yb).astype(o_ref.dtype)


def _merge_call(x2d, g, oa, ob, w_full, gate_col0, w_ba, w_bb, tm, tn):
    t = x2d.shape[0]
    ncol = D_MODEL // tn
    gate0 = gate_col0 // tn
    return pl.pallas_call(
        _merge_body,
        name="merge",
        grid=(t // tm, ncol),
        in_specs=[
            pl.BlockSpec((tm, D_MODEL), lambda i, j: (i, 0)),
            pl.BlockSpec((1, D_MODEL), lambda i, j: (0, 0)),
            pl.BlockSpec((tm, A_Q_W), lambda i, j: (i, 0)),
            pl.BlockSpec((tm, B_GW), lambda i, j: (i, 0)),
            pl.BlockSpec((D_MODEL, tn), lambda i, j: (0, gate0 + j)),
            pl.BlockSpec((D_MODEL, tn), lambda i, j: (0, gate0 + ncol + j)),
            pl.BlockSpec((A_Q_W, tn), lambda i, j: (0, j)),
            pl.BlockSpec((B_GW, tn), lambda i, j: (0, j)),
        ],
        out_specs=pl.BlockSpec((tm, tn), lambda i, j: (i, j)),
        out_shape=jax.ShapeDtypeStruct((t, D_MODEL), BF16),
        scratch_shapes=[pltpu.VMEM((tm, D_MODEL), BF16)],
        compiler_params=_cparams(("parallel", "arbitrary"), VMEM_LIMIT),
    )(x2d, g, oa, ob, w_full, w_full, w_ba, w_bb)


def _route(logits):
    tm = logits.shape[0]
    lane = lax.broadcasted_iota(jnp.int32, (tm, LANES), 1)
    neg = -jnp.inf
    big = LANES

    def first_where(cond):
        return jnp.min(jnp.where(cond, lane, big), axis=1, keepdims=True)

    gl = jnp.where(lane < N_EXPERT_GROUPS, logits, neg)
    gmax = jnp.max(gl, axis=1, keepdims=True)
    gidx = first_where(gl == gmax)
    g_w = 1.0 / jnp.sum(jnp.exp(gl - gmax), axis=1, keepdims=True)
    lo = N_EXPERT_GROUPS + gidx * EXPERTS_PER_GROUP
    in_grp = jnp.logical_and(lane >= lo, lane < lo + EXPERTS_PER_GROUP)
    el = jnp.where(in_grp, logits, neg)
    ep = jnp.exp(el - jnp.max(el, axis=1, keepdims=True))
    prob = ep / jnp.sum(ep, axis=1, keepdims=True)
    prob = jnp.where(in_grp, prob, -1.0)
    p1 = jnp.max(prob, axis=1, keepdims=True)
    i1 = first_where(prob == p1)
    prob2 = jnp.where(lane == i1, -1.0, prob)
    p2 = jnp.max(prob2, axis=1, keepdims=True)
    i2 = first_where(prob2 == p2)
    tot = p1 + p2
    c1 = g_w * (p1 / tot)
    c2 = g_w * (p2 / tot)
    e1 = (i1 - N_EXPERT_GROUPS).astype(F32)
    e2 = (i2 - N_EXPERT_GROUPS).astype(F32)
    return e1, e2, c1, c2


def _outproj_body(mp_ref, xp_ref, ms_ref, xs_ref, wo_ref, g_ref, wrh_ref, wrl_ref, br_ref, x1_ref, route_ref, *, n_prompt):
    def compute(m_ref, x_ref):
        x1 = x_ref[...] + _dot(m_ref[...], wo_ref[...])
        x1_ref[...] = x1
        hi, lo = _split2(_rmsnorm(x1, g_ref[...]))
        logits = _dot(hi, wrh_ref[...]) + _dot(hi, wrl_ref[...]) + _dot(lo, wrh_ref[...]) + br_ref[...]
        e1, e2, c1, c2 = _route(logits)
        lane = lax.broadcasted_iota(jnp.int32, logits.shape, 1)
        route_ref[...] = jnp.where(lane == 0, e1, jnp.where(lane == 1, e2, jnp.where(lane == 2, c1, jnp.where(lane == 3, c2, 0.0))))

    i = pl.program_id(0)
    pl.when(i < n_prompt)(lambda: compute(mp_ref, xp_ref))
    pl.when(i >= n_prompt)(lambda: compute(ms_ref, xs_ref))


def _outproj_call(merged_p, xp, merged_s, xs, w_o, g, wr_hi, wr_lo, br, tm):
    n_p, n_s = xp.shape[0] // tm, xs.shape[0] // tm
    t_all = xp.shape[0] + xs.shape[0]

    def p_map(i):
        return (jnp.minimum(i, n_p - 1), 0)

    def s_map(i):
        return (jnp.maximum(i - n_p, 0), 0)

    def const(shape):
        return pl.BlockSpec(shape, lambda i: (0, 0))

    return pl.pallas_call(
        functools.partial(_outproj_body, n_prompt=n_p),
        name="outproj",
        grid=(n_p + n_s,),
        in_specs=[
            pl.BlockSpec((tm, D_MODEL), p_map), pl.BlockSpec((tm, D_MODEL), p_map),
            pl.BlockSpec((tm, D_MODEL), s_map), pl.BlockSpec((tm, D_MODEL), s_map),
            const((D_MODEL, D_MODEL)), const((1, D_MODEL)), const((D_MODEL, LANES)), const((D_MODEL, LANES)),
            const((1, LANES)),
        ],
        out_specs=[pl.BlockSpec((tm, D_MODEL), lambda i: (i, 0)), pl.BlockSpec((tm, LANES), lambda i: (i, 0))],
        out_shape=[jax.ShapeDtypeStruct((t_all, D_MODEL), F32), jax.ShapeDtypeStruct((t_all, LANES), F32)],
        compiler_params=_cparams(("arbitrary",), VMEM_LIMIT),
    )(merged_p, xp, merged_s, xs, w_o, g, wr_hi, wr_lo, br)


def _pack_bf16_pairs(x):
    n = x.shape[1] // 2
    bits = pltpu.bitcast(x.astype(BF16).astype(F32), jnp.uint32)
    return (bits[:, n:] & jnp.uint32(0xFFFF0000)) | (bits[:, :n] >> 16)


def _unpack_bf16_pairs(packed):
    lo = pltpu.bitcast(packed << 16, F32)
    hi = pltpu.bitcast(packed & jnp.uint32(0xFFFF0000), F32)
    return jnp.concatenate([lo, hi], axis=1).astype(BF16)


def _row_copy(src, src_row, dst, dst_row, sem):
    return pltpu.make_async_copy(src.at[pl.ds(src_row, 1), :], dst.at[pl.ds(dst_row, 1), :], sem)


def _dispatch_body(pos_ref, prev_pos_ref, pad_tile_ref, x1_ref, g_ref, xs_hbm, h_scr, sems, *, tile):
    tm = h_scr.shape[1]
    i = pl.program_id(0)
    slot = i % 2

    @pl.when(i == 0)
    def _():
        h_scr[0] = jnp.zeros(h_scr.shape[1:], h_scr.dtype)

        def pad_copy(t, k):
            row = pl.multiple_of(t * tile + k * tm, tm)
            return pltpu.make_async_copy(h_scr.at[0], xs_hbm.at[pl.ds(row, tm), :], sems.at[0])

        n_used = pad_tile_ref[N_EXPERTS]
        for phase in ("start", "wait"):
            def expert_pad(e, carry, phase=phase):
                @pl.when(pad_tile_ref[e] >= 0)
                def _():
                    for k in range(tile // tm):
                        getattr(pad_copy(pad_tile_ref[e], k), phase)()
                return carry
            lax.fori_loop(0, N_EXPERTS, expert_pad, 0)

            def unused_tile(t, carry, phase=phase):
                @pl.when(t >= n_used)
                def _():
                    for k in range(tile // tm):
                        getattr(pad_copy(t, k), phase)()
                return carry
            lax.fori_loop(0, xs_hbm.shape[0] // tile, unused_tile, 0)

    h_scr[slot] = _pack_bf16_pairs(_rmsnorm(x1_ref[...], g_ref[...]))

    def row_copies(p_ref, s, method):
        def body(j, carry):
            for k in range(2):
                getattr(_row_copy(h_scr.at[s], j, xs_hbm, p_ref[0, 0, 2 * j + k], sems.at[s]), method)()
            return carry
        lax.fori_loop(0, tm, body, 0, unroll=8)

    row_copies(pos_ref, slot, "start")
    pl.when(i > 0)(lambda: row_copies(prev_pos_ref, 1 - slot, "wait"))
    pl.when(i == pl.num_programs(0) - 1)(lambda: row_copies(pos_ref, slot, "wait"))


def _dispatch_call(pos3, pad_tile, x1, g, rows, tm, tile):
    t = x1.shape[0]
    assert tile % tm == 0
    return pl.pallas_call(
        functools.partial(_dispatch_body, tile=tile),
        name="dispatch",
        grid=(t // tm,),
        in_specs=[
            pl.BlockSpec((1, 1, 2 * tm), lambda i: (i, 0, 0), memory_space=pltpu.SMEM),
            pl.BlockSpec((1, 1, 2 * tm), lambda i: (jnp.maximum(i - 1, 0), 0, 0), memory_space=pltpu.SMEM),
            pl.BlockSpec(memory_space=pltpu.SMEM),
            pl.BlockSpec((tm, D_MODEL), lambda i: (i, 0)),
            pl.BlockSpec((1, D_MODEL), lambda i: (0, 0)),
        ],
        out_specs=pl.BlockSpec(memory_space=pl.ANY),
        out_shape=jax.ShapeDtypeStruct((rows, D_MODEL // 2), jnp.uint32),
        scratch_shapes=[pltpu.VMEM((2, tm, D_MODEL // 2), jnp.uint32), pltpu.SemaphoreType.DMA((2,))],
        compiler_params=_cparams(("arbitrary",), VMEM_LIMIT),
    )(pos3, pos3, pad_tile, x1, g)


def _moe_body(te_ref, tv_ref, nu_ref, xs_ref, wg_ref, wu_ref, wd_ref, ys_ref, wg_scr, wu_scr, wd_scr):
    i = pl.program_id(0)

    @pl.when(jnp.logical_and(i < nu_ref[0], jnp.logical_or(i == 0, te_ref[i] != te_ref[jnp.maximum(i - 1, 0)])))
    def _():
        wg_scr[...] = wg_ref[0].astype(BF16)
        wu_scr[...] = wu_ref[0].astype(BF16)
        wd_scr[...] = wd_ref[0].astype(BF16)

    @pl.when(i < nu_ref[0])
    def _():
        row = lax.broadcasted_iota(jnp.int32, xs_ref.shape, 0)
        x = _unpack_bf16_pairs(jnp.where(row < tv_ref[i], xs_ref[...], jnp.uint32(0)))
        a = jax.nn.silu(_dot(x, wg_scr[...])) * _dot(x, wu_scr[...])
        ys_ref[...] = _dot(a.astype(BF16), wd_scr[...])

    @pl.when(i >= nu_ref[0])
    def _():
        ys_ref[...] = jnp.zeros_like(ys_ref)


def _moe_call(tile_expert, tile_valid, n_used, xs, w_gate, w_up, w_down, tm):
    rows = xs.shape[0]

    def row_map(i, te, tv, nu):
        return (jnp.minimum(i, nu[0] - 1), 0)

    def out_map(i, te, tv, nu):
        return (i, 0)

    def w_map(i, te, tv, nu):
        return (te[i], 0, 0)

    return pl.pallas_call(
        _moe_body,
        name="moe",
        grid_spec=pltpu.PrefetchScalarGridSpec(
            num_scalar_prefetch=3,
            grid=(rows // tm,),
            in_specs=[
                pl.BlockSpec((tm, D_MODEL // 2), row_map),
                pl.BlockSpec((1, D_MODEL, D_EXPERT), w_map),
                pl.BlockSpec((1, D_MODEL, D_EXPERT), w_map),
                pl.BlockSpec((1, D_EXPERT, D_MODEL), w_map),
            ],
            out_specs=pl.BlockSpec((tm, D_MODEL), out_map),
            scratch_shapes=[pltpu.VMEM((D_MODEL, D_EXPERT), BF16), pltpu.VMEM((D_MODEL, D_EXPERT), BF16),
                            pltpu.VMEM((D_EXPERT, D_MODEL), BF16)],
        ),
        out_shape=jax.ShapeDtypeStruct((rows, D_MODEL), F32),
        compiler_params=_cparams(("arbitrary",), VMEM_LIMIT),
    )(tile_expert, tile_valid, n_used, xs, w_gate, w_up, w_down)


def _combine_body(pos_ref, next_pos_ref, x1_ref, route_ref, ys_hbm, y_ref, buf, sems):
    tm = buf.shape[2]
    i = pl.program_id(0)
    slot = i % 2

    def row_copies(p_ref, s, method):
        def body(j, carry):
            for k in range(2):
                getattr(_row_copy(ys_hbm, p_ref[0, 0, 2 * j + k], buf.at[s, k], j, sems.at[s]), method)()
            return carry
        lax.fori_loop(0, tm, body, 0, unroll=8)

    pl.when(i == 0)(lambda: row_copies(pos_ref, slot, "start"))
    pl.when(i + 1 < pl.num_programs(0))(lambda: row_copies(next_pos_ref, 1 - slot, "start"))
    row_copies(pos_ref, slot, "wait")
    route = route_ref[...]
    y_ref[...] = x1_ref[...] + (route[:, 2:3] * buf[slot, 0] + route[:, 3:4] * buf[slot, 1])


def _combine_call(name, pos3, x1_all, route, ys, t, row0, tm):
    off = row0 // tm
    last = off + t // tm - 1
    return pl.pallas_call(
        _combine_body,
        name=name,
        grid=(t // tm,),
        in_specs=[
            pl.BlockSpec((1, 1, 2 * tm), lambda i: (i + off, 0, 0), memory_space=pltpu.SMEM),
            pl.BlockSpec((1, 1, 2 * tm), lambda i: (jnp.minimum(i + 1 + off, last), 0, 0), memory_space=pltpu.SMEM),
            pl.BlockSpec((tm, D_MODEL), lambda i: (i + off, 0)),
            pl.BlockSpec((tm, LANES), lambda i: (i + off, 0)),
            pl.BlockSpec(memory_space=pl.ANY),
        ],
        out_specs=pl.BlockSpec((tm, D_MODEL), lambda i: (i, 0)),
        out_shape=jax.ShapeDtypeStruct((t, D_MODEL), F32),
        scratch_shapes=[pltpu.VMEM((2, 2, tm, D_MODEL), F32), pltpu.SemaphoreType.DMA((2,))],
        compiler_params=_cparams(("arbitrary",), VMEM_LIMIT),
    )(pos3, pos3, x1_all, route, ys)


def _routing_offsets(route, tm, n_tiles):
    experts = jnp.arange(N_EXPERTS, dtype=jnp.int32)[None, :]
    e = route[:, :2].astype(jnp.int32)
    hot = [e[:, k:k + 1] == experts for k in range(2)]
    onehot = hot[0].astype(jnp.int32) + hot[1].astype(jnp.int32)
    csum = jnp.cumsum(onehot, axis=0)
    counts = csum[-1]
    tiles_e = (counts + tm - 1) // tm
    tile_end = jnp.cumsum(tiles_e)
    tile_start = tile_end - tiles_e
    row0 = (csum - onehot) + (tile_start * tm)[None, :]
    pos = jnp.stack([jnp.sum(jnp.where(h, row0, 0), axis=1) for h in hot], axis=1)
    n_used = tile_end[-1]
    tile_id = jnp.minimum(jnp.arange(n_tiles, dtype=jnp.int32), n_used - 1)
    tile_expert = jnp.sum((tile_id[:, None] >= tile_end[None, :]).astype(jnp.int32), axis=1)
    tile_hot = tile_expert[:, None] == experts
    seg_end = jnp.sum(jnp.where(tile_hot, (tile_start * tm + counts)[None, :], 0), axis=1)
    tile_valid = jnp.clip(seg_end - tile_id * tm, 0, tm)
    pad_tile = jnp.concatenate([jnp.where(tiles_e > 0, tile_end - 1, -1), n_used.reshape(1)])
    i32 = jnp.int32
    return (pos.astype(i32), tile_expert.astype(i32), tile_valid.astype(i32), n_used.reshape(1).astype(i32),
            pad_tile.astype(i32))


def _pick_tile(t, pref):
    tm = min(pref, t)
    assert t % tm == 0, (t, tm)
    return tm


def kernel(x_prompt, x_sample, cache_a_kv, cache_b1_kv, cache_b2_kv, cache_b3_kv, g_attn_norm, w_in, q_norm_a, k_norm_a, q_norm_b, k_norm_b, sinks_a, w_branch_a, w_branch_b, w_out, g_ffn_norm, w_group_router, b_group_router, w_expert_router, b_expert_router, w_expert_gate, w_expert_up, w_expert_down):
    n, s, _ = x_prompt.shape
    ns, ts, _ = x_sample.shape
    assert ts == SAMPLE_T and s == B_PATTERNS[-1][0] and x_prompt.shape[2] == D_MODEL
    assert g_attn_norm.shape[0] == 1, "single layer"
    tp, tsmp = n * s, ns * ts
    t_all = tp + tsmp
    xp = x_prompt.reshape(tp, D_MODEL)
    xs_in = x_sample.reshape(tsmp, D_MODEL)

    w = w_in[0].astype(BF16)
    blk = A_Q_W + 2 * A_KV_W
    assert blk == B_W and blk % PROJ_SEG == 0 and B_GW == PROJ_SEG
    rep = LANES // HEAD_DIM
    gains = jnp.stack([jnp.tile(v[0], rep) for v in (q_norm_a, k_norm_a, q_norm_b, k_norm_b)]).astype(F32)
    g_attn = g_attn_norm.astype(F32)
    g_ffn = g_ffn_norm.astype(F32)
    plan_a = ([(0, c, PROJ_SEG, 0) for c in range(0, A_Q_W, PROJ_SEG)],
              [(0, A_Q_W, A_KV_W, 1), (0, A_Q_W + A_KV_W, A_KV_W, -1)])
    plan_q = ([(0, c, PROJ_SEG, 2) for c in range(0, B_W, PROJ_SEG)],)
    plan_kv = tuple([(0, g * B_GW, B_GW, 3), (1, g * B_GW, B_GW, -1)] for g in range(B_GROUPS))

    def project(x2d, pos, tm, transposed_seq=None):
        tables = _rope_tables(pos)
        qa, kva = _proj_call("proj_a", x2d, g_attn, w, [(blk, 0)], gains, tables, plan_a, tm)
        (qb,) = _proj_call("proj_qb", x2d, g_attn, w, [(blk, 1)], gains, tables, plan_q, tm)
        kvb = _proj_call("proj_kvb", x2d, g_attn, w, [(blk, 2), (blk, 3)], gains, tables, plan_kv, tm, transposed_seq)
        return qa, kva, qb, kvb

    tm_p = _pick_tile(s, ROW_TILE)
    assert B_PATTERNS[-1][0] >= s, "the last B group's new buffer is the whole sequence, written by proj_kvb"
    qa, kva, qb, kvb = project(xp, jnp.arange(s, dtype=jnp.int32), tm_p, s)
    kvb, kvb_last_t = kvb[:B_GROUPS], kvb[B_GROUPS]
    sinks = sinks_a[0].astype(F32)
    oa_p = _attn_a_call(qa, kva, sinks, n, s)
    ob_p = _attn_b_call(qb, kvb, n, s)

    def to_feature_major(c, heads):
        return jnp.transpose(c, (0, 2, 3, 4, 1)).reshape(c.shape[0], 2 * heads * HEAD_DIM, c.shape[1])

    def from_feature_major(c, heads):
        return jnp.transpose(c.reshape(c.shape[0], 2, heads, HEAD_DIM, c.shape[2]), (0, 4, 1, 2, 3))[None]

    new_a_p = from_feature_major(_tail_transpose_call("tail_a", kva, n, s, min(A_WINDOW, s)), A_KV_HEADS)
    new_b_p = [from_feature_major(_tail_transpose_call("tail_b%d" % g, kvb[g], n, s, min(win, s)), B_HPG)
               for g, (win, _) in enumerate(B_PATTERNS[:-1])]
    new_b_p.append(from_feature_major(kvb_last_t, B_HPG))

    tm_s = _pick_tile(tsmp, ROW_TILE)
    pos_s = PAST_LEN + (jnp.arange(tm_s, dtype=jnp.int32) % ts)
    qa_s, kva_s, qb_s, kvb_s = project(xs_in, pos_s, tm_s)
    caches = [to_feature_major(cache_a_kv[0], A_KV_HEADS)]
    caches += [to_feature_major(c[0], B_HPG) for c in (cache_b1_kv, cache_b2_kv, cache_b3_kv)]
    oa_s, ob_s, na, nb1, nb2, nb3 = _sample_call(sinks, qa_s, kva_s, qb_s, kvb_s, caches, ns)
    new_a_s = from_feature_major(na, A_KV_HEADS)
    new_b_s = [from_feature_major(c, B_HPG) for c in (nb1, nb2, nb3)]

    w_ba = w_branch_a[0].astype(BF16)
    w_bb = w_branch_b[0].astype(BF16)
    w_o = w_out[0].astype(BF16)
    wr = jnp.zeros((D_MODEL, LANES), F32)
    wr = wr.at[:, :N_EXPERT_GROUPS].set(w_group_router[0]).at[:, N_EXPERT_GROUPS:N_EXPERT_GROUPS + N_EXPERTS].set(w_expert_router[0])
    wr_hi = wr.astype(BF16)
    wr_lo = (wr - wr_hi.astype(F32)).astype(BF16)
    br = jnp.zeros((1, LANES), F32)
    br = br.at[0, :N_EXPERT_GROUPS].set(b_group_router[0]).at[0, N_EXPERT_GROUPS:N_EXPERT_GROUPS + N_EXPERTS].set(b_expert_router[0])

    tn = 512
    tm_mp, tm_ms = _pick_tile(tp, 1024), _pick_tile(tsmp, 1024)
    merged_p = _merge_call(xp, g_attn, oa_p, ob_p, w, 4 * blk, w_ba, w_bb, tm_mp, tn)
    merged_s = _merge_call(xs_in, g_attn, oa_s, ob_s, w, 4 * blk, w_ba, w_bb, tm_ms, tn)
    tm_o = _pick_tile(tsmp, ROW_TILE)
    assert tp % tm_o == 0
    x1_all, route = _outproj_call(merged_p, xp, merged_s, xs_in, w_o, g_ffn, wr_hi, wr_lo, br, tm_o)

    tm_e = MOE_TILE
    n_tiles = (2 * t_all + N_EXPERTS * (tm_e - 1)) // tm_e + 1
    pos, tile_expert, tile_valid, n_used, pad_tile = _routing_offsets(route, tm_e, n_tiles)
    tm_d = _pick_tile(tsmp, 2 * ROW_TILE)
    pos3 = pos.reshape(t_all // tm_d, 1, 2 * tm_d)
    xs_sorted = _dispatch_call(pos3, pad_tile, x1_all, g_ffn, n_tiles * tm_e, tm_d, tm_e)
    ys = _moe_call(tile_expert, tile_valid, n_used, xs_sorted, w_expert_gate[0], w_expert_up[0], w_expert_down[0], tm_e)
    y_p = _combine_call("combine_p", pos3, x1_all, route, ys, tp, 0, tm_d)
    y_s = _combine_call("combine_s", pos3, x1_all, route, ys, tsmp, tp, tm_d)

    return (y_p.reshape(n, s, D_MODEL), y_s.reshape(ns, ts, D_MODEL),
            new_a_p, new_a_s, new_b_p[0], new_b_s[0], new_b_p[1], new_b_s[1], new_b_p[2], new_b_s[2])
```

```python
import functools

import jax
import jax.numpy as jnp
import numpy as np
from jax import lax
from jax.experimental import pallas as pl
from jax.experimental.pallas import tpu as pltpu

F32 = jnp.float32
BF16 = jnp.bfloat16

D_MODEL = 2048
HEAD_DIM = 64
ROT_DIM = HEAD_DIM // 4
ROPE_THETA = 500000.0
NORM_EPS = 1e-6
BLOCK = 128
PAST_LEN = 16384

A_HEADS = 16
A_KV_HEADS = 4
A_WINDOW = 128
B_PATTERNS = ((128, 1), (512, 4), (2048, 16))
B_GROUPS = 3
B_HPG = 8
A_Q_W = A_HEADS * HEAD_DIM
A_KV_W = A_KV_HEADS * HEAD_DIM
B_GW = B_HPG * HEAD_DIM
B_W = B_GROUPS * B_GW

N_EXPERT_GROUPS = 4
EXPERTS_PER_GROUP = 8
N_EXPERTS = N_EXPERT_GROUPS * EXPERTS_PER_GROUP
D_EXPERT = 512

LANES = 128
SUBLANES = 8
VMEM_LIMIT = 56 * 1024 * 1024
SAMPLE_T = 8

MOE_TILE = 512
ROW_TILE = 256


def _cparams(sem, vmem=None):
    return pltpu.CompilerParams(dimension_semantics=sem, vmem_limit_bytes=vmem)


def _rmsnorm(x, g):
    return x * lax.rsqrt(jnp.mean(x * x, axis=-1, keepdims=True) + NORM_EPS) * g


def _split2(v):
    hi = v.astype(BF16)
    lo = (v - hi.astype(F32)).astype(BF16)
    return hi, lo


def _split3(v):
    hi = v.astype(BF16)
    r = v - hi.astype(F32)
    mid = r.astype(BF16)
    lo = (r - mid.astype(F32)).astype(BF16)
    return hi, mid, lo


def _dot(a, b):
    return jnp.dot(a, b, preferred_element_type=F32)


def _multiple_of(x, m):
    return x if isinstance(x, int) else pl.multiple_of(x, m)


def _dot_nt(a, b):
    return lax.dot_general(a, b, (((1,), (1,)), ((), ())), preferred_element_type=F32)


def _rope_tables(pos):
    half = ROT_DIM // 2
    inv_freq = ROPE_THETA ** (-jnp.arange(half, dtype=F32) / half)
    ang = pos.astype(F32)[:, None] * inv_freq[None, :]
    cos, sin = jnp.cos(ang), jnp.sin(ang)
    p = pos.shape[0]
    ones = jnp.ones((p, HEAD_DIM - ROT_DIM), F32)
    z8 = jnp.zeros((p, half), F32)
    z48 = jnp.zeros((p, HEAD_DIM - ROT_DIM), F32)
    c = jnp.concatenate([cos, cos, ones], axis=1)
    s_up = jnp.concatenate([-sin, z8, z48], axis=1)
    s_dn = jnp.concatenate([z8, sin, z48], axis=1)
    rep = LANES // HEAD_DIM
    return jnp.tile(c, (1, rep)), jnp.tile(s_up, (1, rep)), jnp.tile(s_dn, (1, rep))


PROJ_SEG = 512


def _proj_body(*refs, plan, n_w, transposed):
    x_ref, g_ref = refs[:2]
    w_refs = refs[2:2 + n_w]
    gain_ref, c_ref, su_ref, sd_ref = refs[2 + n_w:6 + n_w]
    out_refs = refs[6 + n_w:]
    h = _rmsnorm(x_ref[...], g_ref[...]).astype(BF16)
    low = lax.broadcasted_iota(jnp.int32, (x_ref.shape[0], LANES), 1) < HEAD_DIM
    for oi, segments in enumerate(plan):
        o_ref = out_refs[oi]
        ocol = 0
        for widx, col0, width, kind in segments:
            z_all = _dot(h, w_refs[widx][:, col0:col0 + width])
            for c in range(width // LANES):
                z = z_all[:, c * LANES:(c + 1) * LANES]
                if kind >= 0:
                    zz = z * z
                    ss = jnp.where(low, jnp.sum(jnp.where(low, zz, 0.0), axis=1, keepdims=True),
                                   jnp.sum(jnp.where(low, 0.0, zz), axis=1, keepdims=True))
                    z = z * lax.rsqrt(ss * (1.0 / HEAD_DIM) + NORM_EPS) * gain_ref[kind:kind + 1, :]
                    z = (z * c_ref[...] + pltpu.roll(z, LANES - ROT_DIM // 2, 1) * su_ref[...]
                         + pltpu.roll(z, ROT_DIM // 2, 1) * sd_ref[...])
                    if kind % 2 == 0:
                        z = z * (HEAD_DIM ** -0.5)
                o_ref[:, ocol:ocol + LANES] = z
                if transposed and oi == len(plan) - 1:
                    out_refs[-1][ocol:ocol + LANES, :] = z.T
                ocol += LANES


def _proj_call(name, x2d, g, w_full, w_blocks, gains, tables, plan, tm, transposed_seq=None):
    t = x2d.shape[0]
    pos_blocks = tables[0].shape[0] // tm
    tab_spec = pl.BlockSpec((tm, LANES), lambda i: (i % pos_blocks, 0))
    widths = [sum(seg[2] for seg in segments) for segments in plan]
    out_specs = [pl.BlockSpec((tm, w_), lambda i: (i, 0)) for w_ in widths]
    out_shape = [jax.ShapeDtypeStruct((t, w_), F32) for w_ in widths]
    if transposed_seq is not None:
        per_seq = transposed_seq // tm
        out_specs.append(pl.BlockSpec((None, widths[-1], tm), lambda i: (i // per_seq, 0, i % per_seq)))
        out_shape.append(jax.ShapeDtypeStruct((t // transposed_seq, widths[-1], transposed_seq), F32))
    return pl.pallas_call(
        functools.partial(_proj_body, plan=plan, n_w=len(w_blocks), transposed=transposed_seq is not None),
        name=name,
        grid=(t // tm,),
        in_specs=[
            pl.BlockSpec((tm, D_MODEL), lambda i: (i, 0)),
            pl.BlockSpec((1, D_MODEL), lambda i: (0, 0)),
            *[pl.BlockSpec((D_MODEL, bw), lambda i, bj=bj: (0, bj), pipeline_mode=pl.Buffered(1)) for bw, bj in w_blocks],
            pl.BlockSpec((4, LANES), lambda i: (0, 0)),
            tab_spec, tab_spec, tab_spec,
        ],
        out_specs=out_specs,
        out_shape=out_shape,
        compiler_params=_cparams(("parallel",), VMEM_LIMIT),
    )(x2d, g, *([w_full] * len(w_blocks)), gains, *tables)


def _band_mask(rows, nkeys, max_dist, has_prev):
    qi = lax.broadcasted_iota(jnp.int32, (rows, nkeys), 0) & (BLOCK - 1)
    kk = lax.broadcasted_iota(jnp.int32, (rows, nkeys), 1)
    if nkeys == BLOCK:
        dist = qi - kk
        return jnp.logical_and(dist >= 0, dist <= max_dist)
    dist = qi + BLOCK - kk
    band = jnp.logical_and(dist >= 0, dist <= max_dist)
    return jnp.logical_and(band, jnp.logical_or(kk >= BLOCK, has_prev))


UNIT_ROWS = 2 * BLOCK


def _band_bias(nkeys, max_dist, has_prev):
    return jnp.where(_band_mask(UNIT_ROWS, nkeys, max_dist, has_prev), 0.0, -jnp.inf).astype(F32)


def _softmax_stage(s_ref, bias, sink, p_ref, t_ref):
    s = s_ref[...] + bias
    m = jnp.max(s, axis=1, keepdims=True)
    if sink is not None:
        m = jnp.maximum(m, sink)
    p = jnp.exp(s - m)
    denom = jnp.sum(p, axis=1, keepdims=True)
    if sink is not None:
        denom = denom + jnp.exp(sink - m)
    p_ref[...] = p.astype(BF16)
    t_ref[:, :LANES] = jnp.broadcast_to(1.0 / denom, (UNIT_ROWS, LANES))
    t_ref[:, LANES:] = jnp.broadcast_to(m + jnp.log(denom), (UNIT_ROWS, LANES))


def _pipeline(n_units, qk, sm, pv):
    qk(0, 0)
    qk(1, 1)
    sm(0, 0)

    def body(j, carry):
        i = 2 * j
        pv(i - 2, 0)
        sm(i - 1, 1)
        qk(i, 0)
        pv(i - 1, 1)
        sm(i, 0)
        qk(i + 1, 1)
        return carry

    lax.fori_loop(1, n_units // 2, body, 0)
    pv(n_units - 2, 0)
    sm(n_units - 1, 1)
    pv(n_units - 1, 1)


def _attn_a_body(sink_ref, q_ref, kv_ref, o_ref, s_scr, p_scr, t_scr, bias_scr, *, s):
    nb = s // BLOCK
    low = lax.broadcasted_iota(jnp.int32, (BLOCK, LANES), 1) < HEAD_DIM
    high = jnp.logical_not(low)
    top = lax.broadcasted_iota(jnp.int32, (UNIT_ROWS, 1), 0) < BLOCK
    bias_scr[0] = _band_bias(2 * BLOCK, A_WINDOW - 1, False)
    bias_scr[1] = _band_bias(2 * BLOCK, A_WINDOW - 1, True)
    pairs = A_KV_HEADS // 2
    per_pair = A_HEADS // A_KV_HEADS
    units = [(c, hp) for c in range(pairs) for hp in range(per_pair)]

    def rows2(b, lanes):
        start_c = _multiple_of(b * BLOCK, BLOCK)
        start_p = _multiple_of(jnp.maximum(b - 1, 0) * BLOCK, BLOCK)
        return jnp.concatenate([kv_ref[pl.ds(start_p, BLOCK), lanes], kv_ref[pl.ds(start_c, BLOCK), lanes]], axis=0)

    def qk(b, slot):
        start_c = _multiple_of(b * BLOCK, BLOCK)
        for c in range(pairs):
            k2 = rows2(b, slice(c * LANES, (c + 1) * LANES)).astype(BF16)
            for hp in range(per_pair):
                col = c * per_pair + hp
                q = q_ref[pl.ds(start_c, BLOCK), col * LANES:(col + 1) * LANES]
                qr = pltpu.roll(q, HEAD_DIM, 1)
                if hp < per_pair // 2:
                    lhs = jnp.concatenate([jnp.where(low, q, 0.0), jnp.where(low, qr, 0.0)], axis=0)
                else:
                    lhs = jnp.concatenate([jnp.where(high, qr, 0.0), jnp.where(high, q, 0.0)], axis=0)
                s_scr[slot, units.index((c, hp))] = _dot_nt(lhs.astype(BF16), k2)

    def sm(b, slot):
        bias = bias_scr[jnp.minimum(b, 1)]
        for ui, (c, hp) in enumerate(units):
            col = c * per_pair + hp
            sink = jnp.where(top, sink_ref[2 * col], sink_ref[2 * col + 1])
            _softmax_stage(s_scr.at[slot, ui], bias, sink, p_scr.at[slot, ui], t_scr.at[slot, ui])

    def pv(b, slot):
        start_c = _multiple_of(b * BLOCK, BLOCK)
        for c in range(pairs):
            v2 = rows2(b, slice(A_KV_W + c * LANES, A_KV_W + (c + 1) * LANES)).astype(BF16)
            for hp in range(per_pair):
                ui = units.index((c, hp))
                col = c * per_pair + hp
                o2 = _dot(p_scr[slot, ui], v2) * t_scr[slot, ui, :, :LANES]
                if hp < per_pair // 2:
                    o = jnp.where(low, o2[:BLOCK], pltpu.roll(o2[BLOCK:], HEAD_DIM, 1))
                else:
                    o = jnp.where(low, pltpu.roll(o2[:BLOCK], HEAD_DIM, 1), o2[BLOCK:])
                o_ref[pl.ds(start_c, BLOCK), col * LANES:(col + 1) * LANES] = o.astype(o_ref.dtype)

    _pipeline(nb, qk, sm, pv)


def _attn_a_call(qa, kva, sinks, n, s):
    n_units = (A_KV_HEADS // 2) * (A_HEADS // A_KV_HEADS)
    return pl.pallas_call(
        functools.partial(_attn_a_body, s=s),
        name="attn_a",
        grid=(n,),
        in_specs=[
            pl.BlockSpec(memory_space=pltpu.SMEM),
            pl.BlockSpec((s, A_Q_W), lambda i: (i, 0)),
            pl.BlockSpec((s, 2 * A_KV_W), lambda i: (i, 0)),
        ],
        out_specs=pl.BlockSpec((s, A_Q_W), lambda i: (i, 0)),
        out_shape=jax.ShapeDtypeStruct((n * s, A_Q_W), BF16),
        scratch_shapes=[pltpu.VMEM((2, n_units, UNIT_ROWS, 2 * BLOCK), F32),
                        pltpu.VMEM((2, n_units, UNIT_ROWS, 2 * BLOCK), BF16),
                        pltpu.VMEM((2, n_units, UNIT_ROWS, 2 * LANES), F32),
                        pltpu.VMEM((2, UNIT_ROWS, 2 * BLOCK), F32)],
        compiler_params=_cparams(("parallel",), VMEM_LIMIT),
    )(sinks, qa, kva)


def _attn_b_body(q1, q2, q3, k1, v1, k2, v2, k3, v3, o_ref, o_scr, l_scr, s_scr, p_scr, t_scr, bias_scr, *, s):
    low = lax.broadcasted_iota(jnp.int32, (BLOCK, LANES), 1) < HEAD_DIM
    bias_scr[0] = _band_bias(2 * BLOCK, BLOCK, False)
    bias_scr[1] = _band_bias(2 * BLOCK, BLOCK, True)
    bias_cur = _band_bias(BLOCK, BLOCK, False)
    for g, ((_, dil), q_ref, k_ref, v_ref) in enumerate(zip(B_PATTERNS, (q1, q2, q3), (k1, k2, k3), (v1, v2, v3))):
        span = BLOCK * dil
        nblk = max(s // span, 1)
        use_prev = nblk > 1
        nk = 2 * BLOCK if use_prev else BLOCK

        def rows(ref, start, dil=dil):
            if dil == 1:
                return ref[pl.ds(_multiple_of(start, BLOCK), BLOCK), :]
            return ref[pl.ds(start, BLOCK, stride=dil), :]

        def starts(u, dil=dil, span=span):
            b = u // dil
            return b * span + u % dil, jnp.maximum(b - 1, 0) * span + u % dil, b

        def rows2(ref, u, rows=rows, starts=starts, use_prev=use_prev):
            start_c, start_p, _ = starts(u)
            if use_prev:
                return jnp.concatenate([rows(ref, start_p), rows(ref, start_c)], axis=0)
            return rows(ref, start_c)

        def qk(u, slot, q_ref=q_ref, k_ref=k_ref, rows=rows, starts=starts, rows2=rows2, nk=nk):
            q = rows(q_ref, starts(u)[0])
            lhs = jnp.concatenate([jnp.where(low, q, 0.0), jnp.where(low, 0.0, q)], axis=0).astype(BF16)
            s_scr[slot, :, :nk] = _dot_nt(lhs, rows2(k_ref, u).astype(BF16))

        def sm(u, slot, starts=starts, use_prev=use_prev, nk=nk):
            bias = bias_scr[jnp.minimum(starts(u)[2], 1)] if use_prev else bias_cur
            _softmax_stage(s_scr.at[slot, :, :nk], bias, None, p_scr.at[slot, :, :nk], t_scr.at[slot])

        def pv(u, slot, g=g, dil=dil, v_ref=v_ref, starts=starts, rows2=rows2, nk=nk):
            o2 = _dot(p_scr[slot, :, :nk], rows2(v_ref, u).astype(BF16)) * t_scr[slot, :, :LANES]
            l2 = t_scr[slot, :, LANES:]
            o = jnp.where(low, o2[:BLOCK], o2[BLOCK:])
            lse = jnp.where(low, l2[:BLOCK], l2[BLOCK:])
            start_c = starts(u)[0]
            if dil == 1:
                idx = pl.ds(_multiple_of(start_c, BLOCK), BLOCK)
            else:
                idx = pl.ds(start_c, BLOCK, stride=dil)
            o_scr[g, idx, :] = o
            l_scr[g, idx, :] = lse

        _pipeline(nblk * dil, qk, sm, pv)

    chunk = 256

    def comb(c, carry):
        r0 = pl.multiple_of(c * chunk, chunk)
        l0, l1, l2 = (l_scr[g, pl.ds(r0, chunk), :] for g in range(B_GROUPS))
        m = jnp.maximum(jnp.maximum(l0, l1), l2)
        w0, w1, w2 = jnp.exp(l0 - m), jnp.exp(l1 - m), jnp.exp(l2 - m)
        inv = 1.0 / (w0 + w1 + w2)
        acc = (o_scr[0, pl.ds(r0, chunk), :] * (w0 * inv) + o_scr[1, pl.ds(r0, chunk), :] * (w1 * inv)
               + o_scr[2, pl.ds(r0, chunk), :] * (w2 * inv))
        o_ref[pl.ds(r0, chunk), :] = acc.astype(o_ref.dtype)
        return carry

    lax.fori_loop(0, s // chunk, comb, 0)


def _attn_b_call(qb, kvs, n, s):
    pairs = B_GW // LANES
    in_specs = [pl.BlockSpec((s, LANES), lambda i, h, g=g: (i, g * pairs + h)) for g in range(B_GROUPS)]
    args = [qb, qb, qb]
    for kv in kvs:
        in_specs.append(pl.BlockSpec((s, LANES), lambda i, h: (i, h)))
        in_specs.append(pl.BlockSpec((s, LANES), lambda i, h: (i, pairs + h)))
        args += [kv, kv]
    return pl.pallas_call(
        functools.partial(_attn_b_body, s=s),
        name="attn_b",
        grid=(n, pairs),
        in_specs=in_specs,
        out_specs=pl.BlockSpec((s, LANES), lambda i, h: (i, h)),
        out_shape=jax.ShapeDtypeStruct((n * s, B_GW), BF16),
        scratch_shapes=[pltpu.VMEM((B_GROUPS, s, LANES), F32), pltpu.VMEM((B_GROUPS, s, LANES), F32),
                        pltpu.VMEM((2, UNIT_ROWS, 2 * BLOCK), F32), pltpu.VMEM((2, UNIT_ROWS, 2 * BLOCK), BF16),
                        pltpu.VMEM((2, UNIT_ROWS, 2 * LANES), F32), pltpu.VMEM((2, UNIT_ROWS, 2 * BLOCK), F32)],
        compiler_params=_cparams(("parallel", "arbitrary"), VMEM_LIMIT),
    )(*args)


def _shift_insert(old_ref, new_ref, out_ref, w):
    ncol = w // LANES
    keep = lax.broadcasted_iota(jnp.int32, (BLOCK, LANES), 1) < LANES - SAMPLE_T

    def strip(k, carry):
        r0 = pl.multiple_of(k * BLOCK, BLOCK)
        x = old_ref[pl.ds(r0, BLOCK), :]
        pad = jnp.concatenate([jnp.zeros((BLOCK - SAMPLE_T, BLOCK), F32), new_ref[:, pl.ds(r0, BLOCK)]], axis=0)
        nxt = pad.T
        for j in reversed(range(ncol)):
            cur = pltpu.roll(x[:, j * LANES:(j + 1) * LANES], LANES - SAMPLE_T, 1)
            out_ref[pl.ds(r0, BLOCK), j * LANES:(j + 1) * LANES] = jnp.where(keep, cur, nxt)
            nxt = cur
        return carry

    lax.fori_loop(0, old_ref.shape[0] // BLOCK, strip, 0)


def _window_scores(lhs, buf_ref, old_ref, krow, w, dil, min_old):
    r = lhs.shape[0]
    t_main = lax.broadcasted_iota(jnp.int32, (r, w), 0) & (SAMPLE_T - 1)
    delta = (w - SAMPLE_T) + t_main - lax.broadcasted_iota(jnp.int32, (r, w), 1)
    valid_main = jnp.logical_and(delta >= 0, (delta & (dil - 1)) == 0)
    t_old = lax.broadcasted_iota(jnp.int32, (r, LANES), 0) & (SAMPLE_T - 1)
    c_old = lax.broadcasted_iota(jnp.int32, (r, LANES), 1)
    valid_old = jnp.logical_and(jnp.logical_and(c_old < SAMPLE_T, c_old >= t_old + min_old),
                                ((w + t_old - c_old) & (dil - 1)) == 0)
    kt = buf_ref[krow:krow + LANES, :].astype(BF16)
    kx = old_ref[krow:krow + LANES, 0:LANES].astype(BF16)
    return jnp.where(valid_main, _dot(lhs, kt), -jnp.inf), jnp.where(valid_old, _dot(lhs, kx), -jnp.inf)


def _window_softmax(s, sx, sink):
    m = jnp.maximum(jnp.max(s, axis=1, keepdims=True), jnp.max(sx, axis=1, keepdims=True))
    if sink is not None:
        m = jnp.maximum(m, sink)
    p = jnp.exp(s - m)
    px = jnp.exp(sx - m)
    denom = jnp.sum(p, axis=1, keepdims=True) + jnp.sum(px, axis=1, keepdims=True)
    if sink is not None:
        denom = denom + jnp.exp(sink - m)
    inv = 1.0 / denom
    return (p * inv).astype(BF16), (px * inv).astype(BF16), m + jnp.log(denom)


def _window_values(p, px, buf_ref, old_ref, vrow):
    vt = buf_ref[vrow:vrow + LANES, :].astype(BF16)
    vx = old_ref[vrow:vrow + LANES, 0:LANES].astype(BF16)
    return _dot_nt(p, vt) + _dot_nt(px, vx)


def _sample_body(sink_ref, qa_ref, kva_ref, qb_ref, kvb1_ref, kvb2_ref, kvb3_ref, ca_ref, cb1_ref, cb2_ref, cb3_ref,
                 oa_ref, ob_ref, na_ref, nb1_ref, nb2_ref, nb3_ref):
    low = lax.broadcasted_iota(jnp.int32, (SAMPLE_T, LANES), 1) < HEAD_DIM
    new_refs = (kvb1_ref, kvb2_ref, kvb3_ref)
    old_refs = (cb1_ref, cb2_ref, cb3_ref)
    buf_refs = (nb1_ref, nb2_ref, nb3_ref)
    _shift_insert(ca_ref, kva_ref, na_ref, A_WINDOW)
    for g, (win, _) in enumerate(B_PATTERNS):
        _shift_insert(old_refs[g], new_refs[g], buf_refs[g], win)

    units = []
    qa = qa_ref[...]
    heads_per_pair = 2 * (A_HEADS // A_KV_HEADS)
    for c in range(A_KV_HEADS // 2):
        blocks, sinks = [], []
        for jj in range(heads_per_pair):
            j = c * heads_per_pair + jj
            want_low = jj < heads_per_pair // 2
            q = qa[:, (j // 2) * LANES:(j // 2 + 1) * LANES]
            if (j % 2 == 0) != want_low:
                q = pltpu.roll(q, HEAD_DIM, 1)
            blocks.append(jnp.where(low if want_low else jnp.logical_not(low), q, 0.0))
            sinks.append(jnp.full((SAMPLE_T, 1), sink_ref[j], F32))
        units.append((jnp.concatenate(blocks, axis=0).astype(BF16), na_ref, ca_ref, c * LANES, A_KV_W + c * LANES,
                      A_WINDOW, 1, 1, jnp.concatenate(sinks, axis=0)))
    qb = qb_ref[...]
    pairs = B_GW // LANES
    for g, (win, dil) in enumerate(B_PATTERNS):
        for c in range(pairs):
            q = qb[:, g * B_GW + c * LANES:g * B_GW + (c + 1) * LANES]
            lhs = jnp.concatenate([jnp.where(low, q, 0.0), jnp.where(low, 0.0, q)], axis=0).astype(BF16)
            units.append((lhs, buf_refs[g], old_refs[g], c * LANES, B_GW + c * LANES, win, dil, 0, None))

    scores = [_window_scores(lhs, buf, old, krow, w, dil, min_old) for lhs, buf, old, krow, _, w, dil, min_old, _ in units]
    probs = [_window_softmax(s, sx, u[8]) for (s, sx), u in zip(scores, units)]
    results = [(_window_values(p, px, u[1], u[2], u[4]), lse) for (p, px, lse), u in zip(probs, units)]

    outs = {}
    for c in range(A_KV_HEADS // 2):
        o = results[c][0]
        for jj in range(heads_per_pair):
            j = c * heads_per_pair + jj
            ob = o[jj * SAMPLE_T:(jj + 1) * SAMPLE_T]
            if (j % 2 == 0) != (jj < heads_per_pair // 2):
                ob = pltpu.roll(ob, HEAD_DIM, 1)
            outs[j] = ob
    for c in range(A_HEADS // 2):
        oa_ref[:, c * LANES:(c + 1) * LANES] = jnp.where(low, outs[2 * c], outs[2 * c + 1])

    o_g, l_g = [], []
    for g in range(B_GROUPS):
        o_cols, l_cols = [], []
        for c in range(pairs):
            o, lse = results[A_KV_HEADS // 2 + g * pairs + c]
            lse = jnp.broadcast_to(lse, (2 * SAMPLE_T, LANES))
            o_cols.append(jnp.where(low, o[:SAMPLE_T], o[SAMPLE_T:]))
            l_cols.append(jnp.where(low, lse[:SAMPLE_T], lse[SAMPLE_T:]))
        o_g.append(jnp.concatenate(o_cols, axis=1))
        l_g.append(jnp.concatenate(l_cols, axis=1))
    m = jnp.maximum(jnp.maximum(l_g[0], l_g[1]), l_g[2])
    w0, w1, w2 = jnp.exp(l_g[0] - m), jnp.exp(l_g[1] - m), jnp.exp(l_g[2] - m)
    inv = 1.0 / (w0 + w1 + w2)
    ob_ref[...] = o_g[0] * (w0 * inv) + o_g[1] * (w1 * inv) + o_g[2] * (w2 * inv)


def _sample_call(sinks, qa, kva, qb, kvbs, caches, ns):
    def rows(width):
        return pl.BlockSpec((SAMPLE_T, width), lambda i: (i, 0))

    def cache(c):
        return pl.BlockSpec((None,) + c.shape[1:], lambda i: (i, 0, 0))

    news = [qa, kva, qb, *kvbs]
    in_specs = [pl.BlockSpec(memory_space=pltpu.SMEM)] + [rows(a.shape[1]) for a in news] + [cache(c) for c in caches]
    out_shape = ([jax.ShapeDtypeStruct((ns * SAMPLE_T, A_Q_W), F32), jax.ShapeDtypeStruct((ns * SAMPLE_T, B_GW), F32)]
                 + [jax.ShapeDtypeStruct(c.shape, F32) for c in caches])
    out_specs = [rows(A_Q_W), rows(B_GW)] + [cache(c) for c in caches]
    return pl.pallas_call(
        _sample_body,
        name="sample_attn",
        grid=(ns,),
        in_specs=in_specs,
        out_specs=out_specs,
        out_shape=out_shape,
        compiler_params=_cparams(("parallel",), VMEM_LIMIT),
    )(sinks, *news, *caches)


def _tail_transpose_body(x_ref, o_ref):
    o_ref[...] = x_ref[...].T


def _tail_transpose_call(name, kv, n, s, win):
    c = kv.shape[1]
    tt = min(win, 2 * LANES)
    per_seq, first = s // tt, (s - win) // tt
    return pl.pallas_call(
        _tail_transpose_body,
        name=name,
        grid=(n, win // tt),
        in_specs=[pl.BlockSpec((tt, c), lambda i, j: (i * per_seq + first + j, 0))],
        out_specs=pl.BlockSpec((None, c, tt), lambda i, j: (i, 0, j)),
        out_shape=jax.ShapeDtypeStruct((n, c, win), F32),
        compiler_params=_cparams(("parallel", "parallel")),
    )(kv)


def _merge_body(x_ref, g_ref, oa_ref, ob_ref, wga_ref, wgb_ref, wba_ref, wbb_ref, o_ref, h_scr):
    @pl.when(pl.program_id(1) == 0)
    def _():
        h_scr[...] = _rmsnorm(x_ref[...], g_ref[...]).astype(BF16)

    h = h_scr[...]
    oa = oa_ref[...].astype(BF16)
    ob = ob_ref[...].astype(BF16)
    half = o_ref.shape[1] // 2
    for c in range(2):
        cols = slice(c * half, (c + 1) * half)
        ga = _dot(h, wga_ref[:, cols])
        gb = _dot(h, wgb_ref[:, cols])
        ya = _dot(oa, wba_ref[:, cols])
        yb = _dot(ob, wbb_ref[:, cols])
        o_ref[:, cols] = (jax.nn.sigmoid(ga) * ya + jax.nn.sigmoid(gb) * yb).astype(o_ref.dtype)


def _merge_call(x2d, g, oa, ob, w_full, gate_col0, w_ba, w_bb, tm, tn):
    t = x2d.shape[0]
    ncol = D_MODEL // tn
    gate0 = gate_col0 // tn
    return pl.pallas_call(
        _merge_body,
        name="merge",
        grid=(t // tm, ncol),
        in_specs=[
            pl.BlockSpec((tm, D_MODEL), lambda i, j: (i, 0)),
            pl.BlockSpec((1, D_MODEL), lambda i, j: (0, 0)),
            pl.BlockSpec((tm, A_Q_W), lambda i, j: (i, 0)),
            pl.BlockSpec((tm, B_GW), lambda i, j: (i, 0)),
            pl.BlockSpec((D_MODEL, tn), lambda i, j: (0, gate0 + j)),
            pl.BlockSpec((D_MODEL, tn), lambda i, j: (0, gate0 + ncol + j)),
            pl.BlockSpec((A_Q_W, tn), lambda i, j: (0, j)),
            pl.BlockSpec((B_GW, tn), lambda i, j: (0, j)),
        ],
        out_specs=pl.BlockSpec((tm, tn), lambda i, j: (i, j)),
        out_shape=jax.ShapeDtypeStruct((t, D_MODEL), BF16),
        scratch_shapes=[pltpu.VMEM((tm, D_MODEL), BF16)],
        compiler_params=_cparams(("parallel", "arbitrary"), VMEM_LIMIT),
    )(x2d, g, oa, ob, w_full, w_full, w_ba, w_bb)


def _route(logits):
    tm = logits.shape[0]
    lane = lax.broadcasted_iota(jnp.int32, (tm, LANES), 1)
    neg = -jnp.inf
    big = LANES

    def first_where(cond):
        return jnp.min(jnp.where(cond, lane, big), axis=1, keepdims=True)

    gl = jnp.where(lane < N_EXPERT_GROUPS, logits, neg)
    gmax = jnp.max(gl, axis=1, keepdims=True)
    gidx = first_where(gl == gmax)
    g_w = 1.0 / jnp.sum(jnp.exp(gl - gmax), axis=1, keepdims=True)
    lo = N_EXPERT_GROUPS + gidx * EXPERTS_PER_GROUP
    in_grp = jnp.logical_and(lane >= lo, lane < lo + EXPERTS_PER_GROUP)
    el = jnp.where(in_grp, logits, neg)
    ep = jnp.exp(el - jnp.max(el, axis=1, keepdims=True))
    prob = ep / jnp.sum(ep, axis=1, keepdims=True)
    prob = jnp.where(in_grp, prob, -1.0)
    p1 = jnp.max(prob, axis=1, keepdims=True)
    i1 = first_where(prob == p1)
    prob2 = jnp.where(lane == i1, -1.0, prob)
    p2 = jnp.max(prob2, axis=1, keepdims=True)
    i2 = first_where(prob2 == p2)
    tot = p1 + p2
    c1 = g_w * (p1 / tot)
    c2 = g_w * (p2 / tot)
    e1 = (i1 - N_EXPERT_GROUPS).astype(F32)
    e2 = (i2 - N_EXPERT_GROUPS).astype(F32)
    return e1, e2, c1, c2


def _outproj_body(mp_ref, xp_ref, ms_ref, xs_ref, wo_ref, g_ref, wrh_ref, wrl_ref, br_ref, x1_ref, route_ref, *, n_prompt):
    def compute(m_ref, x_ref):
        x1 = x_ref[...] + _dot(m_ref[...], wo_ref[...])
        x1_ref[...] = x1
        hi, lo = _split2(_rmsnorm(x1, g_ref[...]))
        logits = _dot(hi, wrh_ref[...]) + _dot(hi, wrl_ref[...]) + _dot(lo, wrh_ref[...]) + br_ref[...]
        e1, e2, c1, c2 = _route(logits)
        lane = lax.broadcasted_iota(jnp.int32, logits.shape, 1)
        route_ref[...] = jnp.where(lane == 0, e1, jnp.where(lane == 1, e2, jnp.where(lane == 2, c1, jnp.where(lane == 3, c2, 0.0))))

    i = pl.program_id(0)
    pl.when(i < n_prompt)(lambda: compute(mp_ref, xp_ref))
    pl.when(i >= n_prompt)(lambda: compute(ms_ref, xs_ref))


def _outproj_call(merged_p, xp, merged_s, xs, w_o, g, wr_hi, wr_lo, br, tm):
    n_p, n_s = xp.shape[0] // tm, xs.shape[0] // tm
    t_all = xp.shape[0] + xs.shape[0]

    def p_map(i):
        return (jnp.minimum(i, n_p - 1), 0)

    def s_map(i):
        return (jnp.maximum(i - n_p, 0), 0)

    def const(shape):
        return pl.BlockSpec(shape, lambda i: (0, 0))

    return pl.pallas_call(
        functools.partial(_outproj_body, n_prompt=n_p),
        name="outproj",
        grid=(n_p + n_s,),
        in_specs=[
            pl.BlockSpec((tm, D_MODEL), p_map), pl.BlockSpec((tm, D_MODEL), p_map),
            pl.BlockSpec((tm, D_MODEL), s_map), pl.BlockSpec((tm, D_MODEL), s_map),
            const((D_MODEL, D_MODEL)), const((1, D_MODEL)), const((D_MODEL, LANES)), const((D_MODEL, LANES)),
            const((1, LANES)),
        ],
        out_specs=[pl.BlockSpec((tm, D_MODEL), lambda i: (i, 0)), pl.BlockSpec((tm, LANES), lambda i: (i, 0))],
        out_shape=[jax.ShapeDtypeStruct((t_all, D_MODEL), F32), jax.ShapeDtypeStruct((t_all, LANES), F32)],
        compiler_params=_cparams(("arbitrary",), VMEM_LIMIT),
    )(merged_p, xp, merged_s, xs, w_o, g, wr_hi, wr_lo, br)


def _pack_bf16_pairs(x):
    n = x.shape[1] // 2
    bits = pltpu.bitcast(x.astype(BF16).astype(F32), jnp.uint32)
    return (bits[:, n:] & jnp.uint32(0xFFFF0000)) | (bits[:, :n] >> 16)


def _unpack_bf16_pairs(packed):
    lo = pltpu.bitcast(packed << 16, F32)
    hi = pltpu.bitcast(packed & jnp.uint32(0xFFFF0000), F32)
    return jnp.concatenate([lo, hi], axis=1).astype(BF16)


def _row_copy(src, src_row, dst, dst_row, sem):
    return pltpu.make_async_copy(src.at[pl.ds(src_row, 1), :], dst.at[pl.ds(dst_row, 1), :], sem)


def _dispatch_body(pos_ref, prev_pos_ref, pad_tile_ref, x1_ref, g_ref, xs_hbm, h_scr, sems, *, tile):
    tm = h_scr.shape[1]
    i = pl.program_id(0)
    slot = i % 2

    @pl.when(i == 0)
    def _():
        h_scr[0] = jnp.zeros(h_scr.shape[1:], h_scr.dtype)

        def pad_copy(t, k):
            row = pl.multiple_of(t * tile + k * tm, tm)
            return pltpu.make_async_copy(h_scr.at[0], xs_hbm.at[pl.ds(row, tm), :], sems.at[0])

        n_used = pad_tile_ref[N_EXPERTS]
        for phase in ("start", "wait"):
            def expert_pad(e, carry, phase=phase):
                @pl.when(pad_tile_ref[e] >= 0)
                def _():
                    for k in range(tile // tm):
                        getattr(pad_copy(pad_tile_ref[e], k), phase)()
                return carry
            lax.fori_loop(0, N_EXPERTS, expert_pad, 0)

            def unused_tile(t, carry, phase=phase):
                @pl.when(t >= n_used)
                def _():
                    for k in range(tile // tm):
                        getattr(pad_copy(t, k), phase)()
                return carry
            lax.fori_loop(0, xs_hbm.shape[0] // tile, unused_tile, 0)

    h_scr[slot] = _pack_bf16_pairs(_rmsnorm(x1_ref[...], g_ref[...]))

    def row_copies(p_ref, s, method):
        def body(j, carry):
            for k in range(2):
                getattr(_row_copy(h_scr.at[s], j, xs_hbm, p_ref[0, 0, 2 * j + k], sems.at[s]), method)()
            return carry
        lax.fori_loop(0, tm, body, 0, unroll=8)

    row_copies(pos_ref, slot, "start")
    pl.when(i > 0)(lambda: row_copies(prev_pos_ref, 1 - slot, "wait"))
    pl.when(i == pl.num_programs(0) - 1)(lambda: row_copies(pos_ref, slot, "wait"))


def _dispatch_call(pos3, pad_tile, x1, g, rows, tm, tile):
    t = x1.shape[0]
    assert tile % tm == 0
    return pl.pallas_call(
        functools.partial(_dispatch_body, tile=tile),
        name="dispatch",
        grid=(t // tm,),
        in_specs=[
            pl.BlockSpec((1, 1, 2 * tm), lambda i: (i, 0, 0), memory_space=pltpu.SMEM),
            pl.BlockSpec((1, 1, 2 * tm), lambda i: (jnp.maximum(i - 1, 0), 0, 0), memory_space=pltpu.SMEM),
            pl.BlockSpec(memory_space=pltpu.SMEM),
            pl.BlockSpec((tm, D_MODEL), lambda i: (i, 0)),
            pl.BlockSpec((1, D_MODEL), lambda i: (0, 0)),
        ],
        out_specs=pl.BlockSpec(memory_space=pl.ANY),
        out_shape=jax.ShapeDtypeStruct((rows, D_MODEL // 2), jnp.uint32),
        scratch_shapes=[pltpu.VMEM((2, tm, D_MODEL // 2), jnp.uint32), pltpu.SemaphoreType.DMA((2,))],
        compiler_params=_cparams(("arbitrary",), VMEM_LIMIT),
    )(pos3, pos3, pad_tile, x1, g)


def _moe_body(te_ref, tv_ref, nu_ref, xs_ref, wg_ref, wu_ref, wd_ref, ys_ref, wg_scr, wu_scr, wd_scr):
    i = pl.program_id(0)

    @pl.when(jnp.logical_and(i < nu_ref[0], jnp.logical_or(i == 0, te_ref[i] != te_ref[jnp.maximum(i - 1, 0)])))
    def _():
        wg_scr[...] = wg_ref[0].astype(BF16)
        wu_scr[...] = wu_ref[0].astype(BF16)
        wd_scr[...] = wd_ref[0].astype(BF16)

    @pl.when(i < nu_ref[0])
    def _():
        row = lax.broadcasted_iota(jnp.int32, xs_ref.shape, 0)
        x = _unpack_bf16_pairs(jnp.where(row < tv_ref[i], xs_ref[...], jnp.uint32(0)))
        a = jax.nn.silu(_dot(x, wg_scr[...])) * _dot(x, wu_scr[...])
        ys_ref[...] = _dot(a.astype(BF16), wd_scr[...])

    @pl.when(i >= nu_ref[0])
    def _():
        ys_ref[...] = jnp.zeros_like(ys_ref)


def _moe_call(tile_expert, tile_valid, n_used, xs, w_gate, w_up, w_down, tm):
    rows = xs.shape[0]

    def row_map(i, te, tv, nu):
        return (jnp.minimum(i, nu[0] - 1), 0)

    def out_map(i, te, tv, nu):
        return (i, 0)

    def w_map(i, te, tv, nu):
        return (te[i], 0, 0)

    return pl.pallas_call(
        _moe_body,
        name="moe",
        grid_spec=pltpu.PrefetchScalarGridSpec(
            num_scalar_prefetch=3,
            grid=(rows // tm,),
            in_specs=[
                pl.BlockSpec((tm, D_MODEL // 2), row_map),
                pl.BlockSpec((1, D_MODEL, D_EXPERT), w_map),
                pl.BlockSpec((1, D_MODEL, D_EXPERT), w_map),
                pl.BlockSpec((1, D_EXPERT, D_MODEL), w_map),
            ],
            out_specs=pl.BlockSpec((tm, D_MODEL), out_map),
            scratch_shapes=[pltpu.VMEM((D_MODEL, D_EXPERT), BF16), pltpu.VMEM((D_MODEL, D_EXPERT), BF16),
                            pltpu.VMEM((D_EXPERT, D_MODEL), BF16)],
        ),
        out_shape=jax.ShapeDtypeStruct((rows, D_MODEL), F32),
        compiler_params=_cparams(("arbitrary",), VMEM_LIMIT),
    )(tile_expert, tile_valid, n_used, xs, w_gate, w_up, w_down)


def _combine_body(pos_ref, next_pos_ref, x1_ref, route_ref, ys_hbm, y_ref, buf, sems):
    tm = buf.shape[2]
    i = pl.program_id(0)
    slot = i % 2

    def row_copies(p_ref, s, method):
        def body(j, carry):
            for k in range(2):
                getattr(_row_copy(ys_hbm, p_ref[0, 0, 2 * j + k], buf.at[s, k], j, sems.at[s]), method)()
            return carry
        lax.fori_loop(0, tm, body, 0, unroll=8)

    pl.when(i == 0)(lambda: row_copies(pos_ref, slot, "start"))
    pl.when(i + 1 < pl.num_programs(0))(lambda: row_copies(next_pos_ref, 1 - slot, "start"))
    row_copies(pos_ref, slot, "wait")
    route = route_ref[...]
    y_ref[...] = x1_ref[...] + (route[:, 2:3] * buf[slot, 0] + route[:, 3:4] * buf[slot, 1])


def _combine_call(name, pos3, x1_all, route, ys, t, row0, tm):
    off = row0 // tm
    last = off + t // tm - 1
    return pl.pallas_call(
        _combine_body,
        name=name,
        grid=(t // tm,),
        in_specs=[
            pl.BlockSpec((1, 1, 2 * tm), lambda i: (i + off, 0, 0), memory_space=pltpu.SMEM),
            pl.BlockSpec((1, 1, 2 * tm), lambda i: (jnp.minimum(i + 1 + off, last), 0, 0), memory_space=pltpu.SMEM),
            pl.BlockSpec((tm, D_MODEL), lambda i: (i + off, 0)),
            pl.BlockSpec((tm, LANES), lambda i: (i + off, 0)),
            pl.BlockSpec(memory_space=pl.ANY),
        ],
        out_specs=pl.BlockSpec((tm, D_MODEL), lambda i: (i, 0)),
        out_shape=jax.ShapeDtypeStruct((t, D_MODEL), F32),
        scratch_shapes=[pltpu.VMEM((2, 2, tm, D_MODEL), F32), pltpu.SemaphoreType.DMA((2,))],
        compiler_params=_cparams(("arbitrary",), VMEM_LIMIT),
    )(pos3, pos3, x1_all, route, ys)


def _routing_offsets(route, tm, n_tiles):
    experts = jnp.arange(N_EXPERTS, dtype=jnp.int32)[None, :]
    e = route[:, :2].astype(jnp.int32)
    hot = [e[:, k:k + 1] == experts for k in range(2)]
    onehot = hot[0].astype(jnp.int32) + hot[1].astype(jnp.int32)
    csum = jnp.cumsum(onehot, axis=0)
    counts = csum[-1]
    tiles_e = (counts + tm - 1) // tm
    tile_end = jnp.cumsum(tiles_e)
    tile_start = tile_end - tiles_e
    row0 = (csum - onehot) + (tile_start * tm)[None, :]
    pos = jnp.stack([jnp.sum(jnp.where(h, row0, 0), axis=1) for h in hot], axis=1)
    n_used = tile_end[-1]
    tile_id = jnp.minimum(jnp.arange(n_tiles, dtype=jnp.int32), n_used - 1)
    tile_expert = jnp.sum((tile_id[:, None] >= tile_end[None, :]).astype(jnp.int32), axis=1)
    tile_hot = tile_expert[:, None] == experts
    seg_end = jnp.sum(jnp.where(tile_hot, (tile_start * tm + counts)[None, :], 0), axis=1)
    tile_valid = jnp.clip(seg_end - tile_id * tm, 0, tm)
    pad_tile = jnp.concatenate([jnp.where(tiles_e > 0, tile_end - 1, -1), n_used.reshape(1)])
    i32 = jnp.int32
    return (pos.astype(i32), tile_expert.astype(i32), tile_valid.astype(i32), n_used.reshape(1).astype(i32),
            pad_tile.astype(i32))


def _pick_tile(t, pref):
    tm = min(pref, t)
    assert t % tm == 0, (t, tm)
    return tm


def kernel(x_prompt, x_sample, cache_a_kv, cache_b1_kv, cache_b2_kv, cache_b3_kv, g_attn_norm, w_in, q_norm_a, k_norm_a, q_norm_b, k_norm_b, sinks_a, w_branch_a, w_branch_b, w_out, g_ffn_norm, w_group_router, b_group_router, w_expert_router, b_expert_router, w_expert_gate, w_expert_up, w_expert_down):
    n, s, _ = x_prompt.shape
    ns, ts, _ = x_sample.shape
    assert ts == SAMPLE_T and s == B_PATTERNS[-1][0] and x_prompt.shape[2] == D_MODEL
    assert g_attn_norm.shape[0] == 1, "single layer"
    tp, tsmp = n * s, ns * ts
    t_all = tp + tsmp
    xp = x_prompt.reshape(tp, D_MODEL)
    xs_in = x_sample.reshape(tsmp, D_MODEL)

    w = w_in[0].astype(BF16)
    blk = A_Q_W + 2 * A_KV_W
    assert blk == B_W and blk % PROJ_SEG == 0 and B_GW == PROJ_SEG
    rep = LANES // HEAD_DIM
    gains = jnp.stack([jnp.tile(v[0], rep) for v in (q_norm_a, k_norm_a, q_norm_b, k_norm_b)]).astype(F32)
    g_attn = g_attn_norm.astype(F32)
    g_ffn = g_ffn_norm.astype(F32)
    plan_a = ([(0, c, PROJ_SEG, 0) for c in range(0, A_Q_W, PROJ_SEG)],
              [(0, A_Q_W, A_KV_W, 1), (0, A_Q_W + A_KV_W, A_KV_W, -1)])
    plan_q = ([(0, c, PROJ_SEG, 2) for c in range(0, B_W, PROJ_SEG)],)
    plan_kv = tuple([(0, g * B_GW, B_GW, 3), (1, g * B_GW, B_GW, -1)] for g in range(B_GROUPS))

    def shift(plan, by):
        return tuple([(widx + by, col0, width, kind) for widx, col0, width, kind in segs] for segs in plan)

    plan_all = plan_a + shift(plan_q, 1) + shift(plan_kv, 2)

    def project(x2d, pos, tm, transposed_seq=None):
        outs = _proj_call("proj", x2d, g_attn, w, [(blk, j) for j in range(4)], gains, _rope_tables(pos), plan_all, tm,
                          transposed_seq)
        return outs[0], outs[1], outs[2], outs[3:]

    tm_p = _pick_tile(s, ROW_TILE)
    assert B_PATTERNS[-1][0] >= s, "the last B group's new buffer is the whole sequence, written by proj_kvb"
    qa, kva, qb, kvb = project(xp, jnp.arange(s, dtype=jnp.int32), tm_p, s)
    kvb, kvb_last_t = kvb[:B_GROUPS], kvb[B_GROUPS]
    sinks = sinks_a[0].astype(F32)
    oa_p = _attn_a_call(qa, kva, sinks, n, s)
    ob_p = _attn_b_call(qb, kvb, n, s)

    def to_feature_major(c, heads):
        return jnp.transpose(c, (0, 2, 3, 4, 1)).reshape(c.shape[0], 2 * heads * HEAD_DIM, c.shape[1])

    def from_feature_major(c, heads):
        return jnp.transpose(c.reshape(c.shape[0], 2, heads, HEAD_DIM, c.shape[2]), (0, 4, 1, 2, 3))[None]

    new_a_p = from_feature_major(_tail_transpose_call("tail_a", kva, n, s, min(A_WINDOW, s)), A_KV_HEADS)
    new_b_p = [from_feature_major(_tail_transpose_call("tail_b%d" % g, kvb[g], n, s, min(win, s)), B_HPG)
               for g, (win, _) in enumerate(B_PATTERNS[:-1])]
    new_b_p.append(from_feature_major(kvb_last_t, B_HPG))

    tm_s = _pick_tile(tsmp, ROW_TILE)
    pos_s = PAST_LEN + (jnp.arange(tm_s, dtype=jnp.int32) % ts)
    qa_s, kva_s, qb_s, kvb_s = project(xs_in, pos_s, tm_s)
    caches = [to_feature_major(cache_a_kv[0], A_KV_HEADS)]
    caches += [to_feature_major(c[0], B_HPG) for c in (cache_b1_kv, cache_b2_kv, cache_b3_kv)]
    oa_s, ob_s, na, nb1, nb2, nb3 = _sample_call(sinks, qa_s, kva_s, qb_s, kvb_s, caches, ns)
    new_a_s = from_feature_major(na, A_KV_HEADS)
    new_b_s = [from_feature_major(c, B_HPG) for c in (nb1, nb2, nb3)]

    w_ba = w_branch_a[0].astype(BF16)
    w_bb = w_branch_b[0].astype(BF16)
    w_o = w_out[0].astype(BF16)
    wr = jnp.zeros((D_MODEL, LANES), F32)
    wr = wr.at[:, :N_EXPERT_GROUPS].set(w_group_router[0]).at[:, N_EXPERT_GROUPS:N_EXPERT_GROUPS + N_EXPERTS].set(w_expert_router[0])
    wr_hi = wr.astype(BF16)
    wr_lo = (wr - wr_hi.astype(F32)).astype(BF16)
    br = jnp.zeros((1, LANES), F32)
    br = br.at[0, :N_EXPERT_GROUPS].set(b_group_router[0]).at[0, N_EXPERT_GROUPS:N_EXPERT_GROUPS + N_EXPERTS].set(b_expert_router[0])

    tn = 512
    tm_mp, tm_ms = _pick_tile(tp, 1024), _pick_tile(tsmp, 1024)
    merged_p = _merge_call(xp, g_attn, oa_p, ob_p, w, 4 * blk, w_ba, w_bb, tm_mp, tn)
    merged_s = _merge_call(xs_in, g_attn, oa_s, ob_s, w, 4 * blk, w_ba, w_bb, tm_ms, tn)
    tm_o = _pick_tile(tsmp, ROW_TILE)
    assert tp % tm_o == 0
    x1_all, route = _outproj_call(merged_p, xp, merged_s, xs_in, w_o, g_ffn, wr_hi, wr_lo, br, tm_o)

    tm_e = MOE_TILE
    n_tiles = (2 * t_all + N_EXPERTS * (tm_e - 1)) // tm_e + 1
    pos, tile_expert, tile_valid, n_used, pad_tile = _routing_offsets(route, tm_e, n_tiles)
    tm_d = _pick_tile(tsmp, 2 * ROW_TILE)
    pos3 = pos.reshape(t_all // tm_d, 1, 2 * tm_d)
    xs_sorted = _dispatch_call(pos3, pad_tile, x1_all, g_ffn, n_tiles * tm_e, tm_d, tm_e)
    ys = _moe_call(tile_expert, tile_valid, n_used, xs_sorted, w_expert_gate[0], w_expert_up[0], w_expert_down[0], tm_e)
    y_p = _combine_call("combine_p", pos3, x1_all, route, ys, tp, 0, tm_d)
    y_s = _combine_call("combine_s", pos3, x1_all, route, ys, tsmp, tp, tm_d)

    return (y_p.reshape(n, s, D_MODEL), y_s.reshape(ns, ts, D_MODEL),
            new_a_p, new_a_s, new_b_p[0], new_b_s[0], new_b_p[1], new_b_s[1], new_b_p[2], new_b_s[2])
```

```python
import functools

import jax
import jax.numpy as jnp
from jax import lax
from jax.experimental import pallas as pl
from jax.experimental.pallas import tpu as pltpu

F32 = jnp.float32
BF16 = jnp.bfloat16

D_MODEL = 2048
HEAD_DIM = 64
ROT_DIM = HEAD_DIM // 4
ROPE_THETA = 500000.0
NORM_EPS = 1e-6
BLOCK = 128
PAST_LEN = 16384

A_HEADS = 16
A_KV_HEADS = 4
A_WINDOW = 128
B_PATTERNS = ((128, 1), (512, 4), (2048, 16))
B_GROUPS = 3
B_HPG = 8
A_Q_W = A_HEADS * HEAD_DIM
A_KV_W = A_KV_HEADS * HEAD_DIM
B_GW = B_HPG * HEAD_DIM
B_W = B_GROUPS * B_GW

N_EXPERT_GROUPS = 4
EXPERTS_PER_GROUP = 8
N_EXPERTS = N_EXPERT_GROUPS * EXPERTS_PER_GROUP
D_EXPERT = 512

LANES = 128
VMEM_LIMIT = 56 * 1024 * 1024
SAMPLE_T = 8

MOE_TILE = 512
ROW_TILE = 256


def _cparams(sem, vmem=None):
    return pltpu.CompilerParams(dimension_semantics=sem, vmem_limit_bytes=vmem)


def _rmsnorm(x, g):
    return x * lax.rsqrt(jnp.mean(x * x, axis=-1, keepdims=True) + NORM_EPS) * g


def _split2(v):
    hi = v.astype(BF16)
    lo = (v - hi.astype(F32)).astype(BF16)
    return hi, lo


def _dot(a, b):
    return jnp.dot(a, b, preferred_element_type=F32)


def _multiple_of(x, m):
    return x if isinstance(x, int) else pl.multiple_of(x, m)


def _dot_nt(a, b):
    return lax.dot_general(a, b, (((1,), (1,)), ((), ())), preferred_element_type=F32)


def _rope_tables(pos):
    half = ROT_DIM // 2
    inv_freq = ROPE_THETA ** (-jnp.arange(half, dtype=F32) / half)
    ang = pos.astype(F32)[:, None] * inv_freq[None, :]
    cos, sin = jnp.cos(ang), jnp.sin(ang)
    p = pos.shape[0]
    ones = jnp.ones((p, HEAD_DIM - ROT_DIM), F32)
    z8 = jnp.zeros((p, half), F32)
    z48 = jnp.zeros((p, HEAD_DIM - ROT_DIM), F32)
    c = jnp.concatenate([cos, cos, ones], axis=1)
    s_up = jnp.concatenate([-sin, z8, z48], axis=1)
    s_dn = jnp.concatenate([z8, sin, z48], axis=1)
    rep = LANES // HEAD_DIM
    return jnp.tile(c, (1, rep)), jnp.tile(s_up, (1, rep)), jnp.tile(s_dn, (1, rep))


PROJ_SEG = 512


def _proj_body(*refs, plan, n_w, transposed):
    x_ref, g_ref = refs[:2]
    w_refs = refs[2:2 + n_w]
    gain_ref, c_ref, su_ref, sd_ref = refs[2 + n_w:6 + n_w]
    out_refs = refs[6 + n_w:]
    h = _rmsnorm(x_ref[...], g_ref[...]).astype(BF16)
    low = lax.broadcasted_iota(jnp.int32, (x_ref.shape[0], LANES), 1) < HEAD_DIM
    for oi, segments in enumerate(plan):
        o_ref = out_refs[oi]
        ocol = 0
        for widx, col0, width, kind in segments:
            z_all = _dot(h, w_refs[widx][:, col0:col0 + width])
            for c in range(width // LANES):
                z = z_all[:, c * LANES:(c + 1) * LANES]
                if kind >= 0:
                    zz = z * z
                    ss = jnp.where(low, jnp.sum(jnp.where(low, zz, 0.0), axis=1, keepdims=True),
                                   jnp.sum(jnp.where(low, 0.0, zz), axis=1, keepdims=True))
                    z = z * lax.rsqrt(ss * (1.0 / HEAD_DIM) + NORM_EPS) * gain_ref[kind:kind + 1, :]
                    z = (z * c_ref[...] + pltpu.roll(z, LANES - ROT_DIM // 2, 1) * su_ref[...]
                         + pltpu.roll(z, ROT_DIM // 2, 1) * sd_ref[...])
                    if kind % 2 == 0:
                        z = z * (HEAD_DIM ** -0.5)
                o_ref[:, ocol:ocol + LANES] = z
                if transposed and oi == len(plan) - 1:
                    out_refs[-1][ocol:ocol + LANES, :] = z.T
                ocol += LANES


def _proj_call(name, x2d, g, w_full, w_blocks, gains, tables, plan, tm, transposed_seq=None):
    t = x2d.shape[0]
    pos_blocks = tables[0].shape[0] // tm
    tab_spec = pl.BlockSpec((tm, LANES), lambda i: (i % pos_blocks, 0))
    widths = [sum(seg[2] for seg in segments) for segments in plan]
    out_specs = [pl.BlockSpec((tm, w_), lambda i: (i, 0)) for w_ in widths]
    out_shape = [jax.ShapeDtypeStruct((t, w_), F32) for w_ in widths]
    if transposed_seq is not None:
        per_seq = transposed_seq // tm
        out_specs.append(pl.BlockSpec((None, widths[-1], tm), lambda i: (i // per_seq, 0, i % per_seq)))
        out_shape.append(jax.ShapeDtypeStruct((t // transposed_seq, widths[-1], transposed_seq), F32))
    return pl.pallas_call(
        functools.partial(_proj_body, plan=plan, n_w=len(w_blocks), transposed=transposed_seq is not None),
        name=name,
        grid=(t // tm,),
        in_specs=[
            pl.BlockSpec((tm, D_MODEL), lambda i: (i, 0)),
            pl.BlockSpec((1, D_MODEL), lambda i: (0, 0)),
            *[pl.BlockSpec((D_MODEL, bw), lambda i, bj=bj: (0, bj), pipeline_mode=pl.Buffered(1)) for bw, bj in w_blocks],
            pl.BlockSpec((4, LANES), lambda i: (0, 0)),
            tab_spec, tab_spec, tab_spec,
        ],
        out_specs=out_specs,
        out_shape=out_shape,
        compiler_params=_cparams(("parallel",), VMEM_LIMIT),
    )(x2d, g, *([w_full] * len(w_blocks)), gains, *tables)


def _band_mask(rows, nkeys, max_dist, has_prev):
    qi = lax.broadcasted_iota(jnp.int32, (rows, nkeys), 0) & (BLOCK - 1)
    kk = lax.broadcasted_iota(jnp.int32, (rows, nkeys), 1)
    if nkeys == BLOCK:
        dist = qi - kk
        return jnp.logical_and(dist >= 0, dist <= max_dist)
    dist = qi + BLOCK - kk
    band = jnp.logical_and(dist >= 0, dist <= max_dist)
    return jnp.logical_and(band, jnp.logical_or(kk >= BLOCK, has_prev))


UNIT_ROWS = 2 * BLOCK


def _band_bias(nkeys, max_dist, has_prev):
    return jnp.where(_band_mask(UNIT_ROWS, nkeys, max_dist, has_prev), 0.0, -jnp.inf).astype(F32)


def _softmax_stage(s_ref, bias, sink, p_ref, t_ref):
    s = s_ref[...] + bias
    m = jnp.max(s, axis=1, keepdims=True)
    if sink is not None:
        m = jnp.maximum(m, sink)
    p = jnp.exp(s - m)
    denom = jnp.sum(p, axis=1, keepdims=True)
    if sink is not None:
        denom = denom + jnp.exp(sink - m)
    p_ref[...] = p.astype(BF16)
    t_ref[:, :LANES] = jnp.broadcast_to(1.0 / denom, (UNIT_ROWS, LANES))
    t_ref[:, LANES:] = jnp.broadcast_to(m + jnp.log(denom), (UNIT_ROWS, LANES))


def _pipeline(n_units, qk, sm, pv):
    qk(0, 0)
    qk(1, 1)
    sm(0, 0)

    def body(j, carry):
        i = 2 * j
        pv(i - 2, 0)
        sm(i - 1, 1)
        qk(i, 0)
        pv(i - 1, 1)
        sm(i, 0)
        qk(i + 1, 1)
        return carry

    lax.fori_loop(1, n_units // 2, body, 0)
    pv(n_units - 2, 0)
    sm(n_units - 1, 1)
    pv(n_units - 1, 1)


def _attn_a_body(sink_ref, q_ref, kv_ref, o_ref, s_scr, p_scr, t_scr, bias_scr, *, s):
    nb = s // BLOCK
    low = lax.broadcasted_iota(jnp.int32, (BLOCK, LANES), 1) < HEAD_DIM
    high = jnp.logical_not(low)
    top = lax.broadcasted_iota(jnp.int32, (UNIT_ROWS, 1), 0) < BLOCK
    bias_scr[0] = _band_bias(2 * BLOCK, A_WINDOW - 1, False)
    bias_scr[1] = _band_bias(2 * BLOCK, A_WINDOW - 1, True)
    pairs = A_KV_HEADS // 2
    per_pair = A_HEADS // A_KV_HEADS
    units = [(c, hp) for c in range(pairs) for hp in range(per_pair)]

    def rows2(b, lanes):
        start_c = _multiple_of(b * BLOCK, BLOCK)
        start_p = _multiple_of(jnp.maximum(b - 1, 0) * BLOCK, BLOCK)
        return jnp.concatenate([kv_ref[pl.ds(start_p, BLOCK), lanes], kv_ref[pl.ds(start_c, BLOCK), lanes]], axis=0)

    def qk(b, slot):
        start_c = _multiple_of(b * BLOCK, BLOCK)
        for c in range(pairs):
            k2 = rows2(b, slice(c * LANES, (c + 1) * LANES)).astype(BF16)
            for hp in range(per_pair):
                col = c * per_pair + hp
                q = q_ref[pl.ds(start_c, BLOCK), col * LANES:(col + 1) * LANES]
                qr = pltpu.roll(q, HEAD_DIM, 1)
                if hp < per_pair // 2:
                    lhs = jnp.concatenate([jnp.where(low, q, 0.0), jnp.where(low, qr, 0.0)], axis=0)
                else:
                    lhs = jnp.concatenate([jnp.where(high, qr, 0.0), jnp.where(high, q, 0.0)], axis=0)
                s_scr[slot, units.index((c, hp))] = _dot_nt(lhs.astype(BF16), k2)

    def sm(b, slot):
        bias = bias_scr[jnp.minimum(b, 1)]
        for ui, (c, hp) in enumerate(units):
            col = c * per_pair + hp
            sink = jnp.where(top, sink_ref[2 * col], sink_ref[2 * col + 1])
            _softmax_stage(s_scr.at[slot, ui], bias, sink, p_scr.at[slot, ui], t_scr.at[slot, ui])

    def pv(b, slot):
        start_c = _multiple_of(b * BLOCK, BLOCK)
        for c in range(pairs):
            v2 = rows2(b, slice(A_KV_W + c * LANES, A_KV_W + (c + 1) * LANES)).astype(BF16)
            for hp in range(per_pair):
                ui = units.index((c, hp))
                col = c * per_pair + hp
                o2 = _dot(p_scr[slot, ui], v2) * t_scr[slot, ui, :, :LANES]
                if hp < per_pair // 2:
                    o = jnp.where(low, o2[:BLOCK], pltpu.roll(o2[BLOCK:], HEAD_DIM, 1))
                else:
                    o = jnp.where(low, pltpu.roll(o2[:BLOCK], HEAD_DIM, 1), o2[BLOCK:])
                o_ref[pl.ds(start_c, BLOCK), col * LANES:(col + 1) * LANES] = o.astype(o_ref.dtype)

    _pipeline(nb, qk, sm, pv)


def _attn_a_call(qa, kva, sinks, n, s):
    n_units = (A_KV_HEADS // 2) * (A_HEADS // A_KV_HEADS)
    return pl.pallas_call(
        functools.partial(_attn_a_body, s=s),
        name="attn_a",
        grid=(n,),
        in_specs=[
            pl.BlockSpec(memory_space=pltpu.SMEM),
            pl.BlockSpec((s, A_Q_W), lambda i: (i, 0)),
            pl.BlockSpec((s, 2 * A_KV_W), lambda i: (i, 0)),
        ],
        out_specs=pl.BlockSpec((s, A_Q_W), lambda i: (i, 0)),
        out_shape=jax.ShapeDtypeStruct((n * s, A_Q_W), BF16),
        scratch_shapes=[pltpu.VMEM((2, n_units, UNIT_ROWS, 2 * BLOCK), F32),
                        pltpu.VMEM((2, n_units, UNIT_ROWS, 2 * BLOCK), BF16),
                        pltpu.VMEM((2, n_units, UNIT_ROWS, 2 * LANES), F32),
                        pltpu.VMEM((2, UNIT_ROWS, 2 * BLOCK), F32)],
        compiler_params=_cparams(("parallel",), VMEM_LIMIT),
    )(sinks, qa, kva)


def _attn_b_body(q1, q2, q3, k1, v1, k2, v2, k3, v3, o_ref, o_scr, l_scr, s_scr, p_scr, t_scr, bias_scr, *, s):
    low = lax.broadcasted_iota(jnp.int32, (BLOCK, LANES), 1) < HEAD_DIM
    bias_scr[0] = _band_bias(2 * BLOCK, BLOCK, False)
    bias_scr[1] = _band_bias(2 * BLOCK, BLOCK, True)
    bias_cur = _band_bias(BLOCK, BLOCK, False)
    for g, ((_, dil), q_ref, k_ref, v_ref) in enumerate(zip(B_PATTERNS, (q1, q2, q3), (k1, k2, k3), (v1, v2, v3))):
        span = BLOCK * dil
        nblk = max(s // span, 1)
        use_prev = nblk > 1
        nk = 2 * BLOCK if use_prev else BLOCK

        def rows(ref, start, dil=dil):
            if dil == 1:
                return ref[pl.ds(_multiple_of(start, BLOCK), BLOCK), :]
            return ref[pl.ds(start, BLOCK, stride=dil), :]

        def starts(u, dil=dil, span=span):
            b = u // dil
            return b * span + u % dil, jnp.maximum(b - 1, 0) * span + u % dil, b

        def rows2(ref, u, rows=rows, starts=starts, use_prev=use_prev):
            start_c, start_p, _ = starts(u)
            if use_prev:
                return jnp.concatenate([rows(ref, start_p), rows(ref, start_c)], axis=0)
            return rows(ref, start_c)

        def qk(u, slot, q_ref=q_ref, k_ref=k_ref, rows=rows, starts=starts, rows2=rows2, nk=nk):
            q = rows(q_ref, starts(u)[0])
            lhs = jnp.concatenate([jnp.where(low, q, 0.0), jnp.where(low, 0.0, q)], axis=0).astype(BF16)
            s_scr[slot, :, :nk] = _dot_nt(lhs, rows2(k_ref, u).astype(BF16))

        def sm(u, slot, starts=starts, use_prev=use_prev, nk=nk):
            bias = bias_scr[jnp.minimum(starts(u)[2], 1)] if use_prev else bias_cur
            _softmax_stage(s_scr.at[slot, :, :nk], bias, None, p_scr.at[slot, :, :nk], t_scr.at[slot])

        def pv(u, slot, g=g, dil=dil, v_ref=v_ref, starts=starts, rows2=rows2, nk=nk):
            o2 = _dot(p_scr[slot, :, :nk], rows2(v_ref, u).astype(BF16)) * t_scr[slot, :, :LANES]
            l2 = t_scr[slot, :, LANES:]
            o = jnp.where(low, o2[:BLOCK], o2[BLOCK:])
            lse = jnp.where(low, l2[:BLOCK], l2[BLOCK:])
            start_c = starts(u)[0]
            if dil == 1:
                idx = pl.ds(_multiple_of(start_c, BLOCK), BLOCK)
            else:
                idx = pl.ds(start_c, BLOCK, stride=dil)
            o_scr[g, idx, :] = o
            l_scr[g, idx, :] = lse

        _pipeline(nblk * dil, qk, sm, pv)

    chunk = 256

    def comb(c, carry):
        r0 = pl.multiple_of(c * chunk, chunk)
        l0, l1, l2 = (l_scr[g, pl.ds(r0, chunk), :] for g in range(B_GROUPS))
        m = jnp.maximum(jnp.maximum(l0, l1), l2)
        w0, w1, w2 = jnp.exp(l0 - m), jnp.exp(l1 - m), jnp.exp(l2 - m)
        inv = 1.0 / (w0 + w1 + w2)
        acc = (o_scr[0, pl.ds(r0, chunk), :] * (w0 * inv) + o_scr[1, pl.ds(r0, chunk), :] * (w1 * inv)
               + o_scr[2, pl.ds(r0, chunk), :] * (w2 * inv))
        o_ref[pl.ds(r0, chunk), :] = acc.astype(o_ref.dtype)
        return carry

    lax.fori_loop(0, s // chunk, comb, 0)


def _attn_b_call(qb, kvs, n, s):
    pairs = B_GW // LANES
    in_specs = [pl.BlockSpec((s, LANES), lambda i, h, g=g: (i, g * pairs + h)) for g in range(B_GROUPS)]
    args = [qb, qb, qb]
    for kv in kvs:
        in_specs.append(pl.BlockSpec((s, LANES), lambda i, h: (i, h)))
        in_specs.append(pl.BlockSpec((s, LANES), lambda i, h: (i, pairs + h)))
        args += [kv, kv]
    return pl.pallas_call(
        functools.partial(_attn_b_body, s=s),
        name="attn_b",
        grid=(n, pairs),
        in_specs=in_specs,
        out_specs=pl.BlockSpec((s, LANES), lambda i, h: (i, h)),
        out_shape=jax.ShapeDtypeStruct((n * s, B_GW), BF16),
        scratch_shapes=[pltpu.VMEM((B_GROUPS, s, LANES), F32), pltpu.VMEM((B_GROUPS, s, LANES), F32),
                        pltpu.VMEM((2, UNIT_ROWS, 2 * BLOCK), F32), pltpu.VMEM((2, UNIT_ROWS, 2 * BLOCK), BF16),
                        pltpu.VMEM((2, UNIT_ROWS, 2 * LANES), F32), pltpu.VMEM((2, UNIT_ROWS, 2 * BLOCK), F32)],
        compiler_params=_cparams(("parallel", "arbitrary"), VMEM_LIMIT),
    )(*args)


def _shift_insert(old_ref, new_ref, out_ref, w):
    ncol = w // LANES
    keep = lax.broadcasted_iota(jnp.int32, (BLOCK, LANES), 1) < LANES - SAMPLE_T

    def strip(k, carry):
        r0 = pl.multiple_of(k * BLOCK, BLOCK)
        x = old_ref[pl.ds(r0, BLOCK), :]
        pad = jnp.concatenate([jnp.zeros((BLOCK - SAMPLE_T, BLOCK), F32), new_ref[:, pl.ds(r0, BLOCK)]], axis=0)
        nxt = pad.T
        for j in reversed(range(ncol)):
            cur = pltpu.roll(x[:, j * LANES:(j + 1) * LANES], LANES - SAMPLE_T, 1)
            out_ref[pl.ds(r0, BLOCK), j * LANES:(j + 1) * LANES] = jnp.where(keep, cur, nxt)
            nxt = cur
        return carry

    lax.fori_loop(0, old_ref.shape[0] // BLOCK, strip, 0)


def _window_scores(lhs, buf_ref, old_ref, krow, w, dil, min_old):
    r = lhs.shape[0]
    t_main = lax.broadcasted_iota(jnp.int32, (r, w), 0) & (SAMPLE_T - 1)
    delta = (w - SAMPLE_T) + t_main - lax.broadcasted_iota(jnp.int32, (r, w), 1)
    valid_main = jnp.logical_and(delta >= 0, (delta & (dil - 1)) == 0)
    t_old = lax.broadcasted_iota(jnp.int32, (r, LANES), 0) & (SAMPLE_T - 1)
    c_old = lax.broadcasted_iota(jnp.int32, (r, LANES), 1)
    valid_old = jnp.logical_and(jnp.logical_and(c_old < SAMPLE_T, c_old >= t_old + min_old),
                                ((w + t_old - c_old) & (dil - 1)) == 0)
    kt = buf_ref[krow:krow + LANES, :].astype(BF16)
    kx = old_ref[krow:krow + LANES, 0:LANES].astype(BF16)
    return jnp.where(valid_main, _dot(lhs, kt), -jnp.inf), jnp.where(valid_old, _dot(lhs, kx), -jnp.inf)


def _window_softmax(s, sx, sink):
    m = jnp.maximum(jnp.max(s, axis=1, keepdims=True), jnp.max(sx, axis=1, keepdims=True))
    if sink is not None:
        m = jnp.maximum(m, sink)
    p = jnp.exp(s - m)
    px = jnp.exp(sx - m)
    denom = jnp.sum(p, axis=1, keepdims=True) + jnp.sum(px, axis=1, keepdims=True)
    if sink is not None:
        denom = denom + jnp.exp(sink - m)
    inv = 1.0 / denom
    return (p * inv).astype(BF16), (px * inv).astype(BF16), m + jnp.log(denom)


def _window_values(p, px, buf_ref, old_ref, vrow):
    vt = buf_ref[vrow:vrow + LANES, :].astype(BF16)
    vx = old_ref[vrow:vrow + LANES, 0:LANES].astype(BF16)
    return _dot_nt(p, vt) + _dot_nt(px, vx)


def _sample_body(sink_ref, qa_ref, kva_ref, qb_ref, kvb1_ref, kvb2_ref, kvb3_ref, ca_ref, cb1_ref, cb2_ref, cb3_ref,
                 oa_ref, ob_ref, na_ref, nb1_ref, nb2_ref, nb3_ref):
    low = lax.broadcasted_iota(jnp.int32, (SAMPLE_T, LANES), 1) < HEAD_DIM
    new_refs = (kvb1_ref, kvb2_ref, kvb3_ref)
    old_refs = (cb1_ref, cb2_ref, cb3_ref)
    buf_refs = (nb1_ref, nb2_ref, nb3_ref)
    _shift_insert(ca_ref, kva_ref, na_ref, A_WINDOW)
    for g, (win, _) in enumerate(B_PATTERNS):
        _shift_insert(old_refs[g], new_refs[g], buf_refs[g], win)

    units = []
    qa = qa_ref[...]
    heads_per_pair = 2 * (A_HEADS // A_KV_HEADS)
    for c in range(A_KV_HEADS // 2):
        blocks, sinks = [], []
        for jj in range(heads_per_pair):
            j = c * heads_per_pair + jj
            want_low = jj < heads_per_pair // 2
            q = qa[:, (j // 2) * LANES:(j // 2 + 1) * LANES]
            if (j % 2 == 0) != want_low:
                q = pltpu.roll(q, HEAD_DIM, 1)
            blocks.append(jnp.where(low if want_low else jnp.logical_not(low), q, 0.0))
            sinks.append(jnp.full((SAMPLE_T, 1), sink_ref[j], F32))
        units.append((jnp.concatenate(blocks, axis=0).astype(BF16), na_ref, ca_ref, c * LANES, A_KV_W + c * LANES,
                      A_WINDOW, 1, 1, jnp.concatenate(sinks, axis=0)))
    qb = qb_ref[...]
    pairs = B_GW // LANES
    for g, (win, dil) in enumerate(B_PATTERNS):
        for c in range(pairs):
            q = qb[:, g * B_GW + c * LANES:g * B_GW + (c + 1) * LANES]
            lhs = jnp.concatenate([jnp.where(low, q, 0.0), jnp.where(low, 0.0, q)], axis=0).astype(BF16)
            units.append((lhs, buf_refs[g], old_refs[g], c * LANES, B_GW + c * LANES, win, dil, 0, None))

    scores = [_window_scores(lhs, buf, old, krow, w, dil, min_old) for lhs, buf, old, krow, _, w, dil, min_old, _ in units]
    probs = [_window_softmax(s, sx, u[8]) for (s, sx), u in zip(scores, units)]
    results = [(_window_values(p, px, u[1], u[2], u[4]), lse) for (p, px, lse), u in zip(probs, units)]

    outs = {}
    for c in range(A_KV_HEADS // 2):
        o = results[c][0]
        for jj in range(heads_per_pair):
            j = c * heads_per_pair + jj
            ob = o[jj * SAMPLE_T:(jj + 1) * SAMPLE_T]
            if (j % 2 == 0) != (jj < heads_per_pair // 2):
                ob = pltpu.roll(ob, HEAD_DIM, 1)
            outs[j] = ob
    for c in range(A_HEADS // 2):
        oa_ref[:, c * LANES:(c + 1) * LANES] = jnp.where(low, outs[2 * c], outs[2 * c + 1])

    o_g, l_g = [], []
    for g in range(B_GROUPS):
        o_cols, l_cols = [], []
        for c in range(pairs):
            o, lse = results[A_KV_HEADS // 2 + g * pairs + c]
            lse = jnp.broadcast_to(lse, (2 * SAMPLE_T, LANES))
            o_cols.append(jnp.where(low, o[:SAMPLE_T], o[SAMPLE_T:]))
            l_cols.append(jnp.where(low, lse[:SAMPLE_T], lse[SAMPLE_T:]))
        o_g.append(jnp.concatenate(o_cols, axis=1))
        l_g.append(jnp.concatenate(l_cols, axis=1))
    m = jnp.maximum(jnp.maximum(l_g[0], l_g[1]), l_g[2])
    w0, w1, w2 = jnp.exp(l_g[0] - m), jnp.exp(l_g[1] - m), jnp.exp(l_g[2] - m)
    inv = 1.0 / (w0 + w1 + w2)
    ob_ref[...] = o_g[0] * (w0 * inv) + o_g[1] * (w1 * inv) + o_g[2] * (w2 * inv)


def _sample_call(sinks, qa, kva, qb, kvbs, caches, ns):
    def rows(width):
        return pl.BlockSpec((SAMPLE_T, width), lambda i: (i, 0))

    def cache(c):
        return pl.BlockSpec((None,) + c.shape[1:], lambda i: (i, 0, 0))

    news = [qa, kva, qb, *kvbs]
    in_specs = [pl.BlockSpec(memory_space=pltpu.SMEM)] + [rows(a.shape[1]) for a in news] + [cache(c) for c in caches]
    out_shape = ([jax.ShapeDtypeStruct((ns * SAMPLE_T, A_Q_W), F32), jax.ShapeDtypeStruct((ns * SAMPLE_T, B_GW), F32)]
                 + [jax.ShapeDtypeStruct(c.shape, F32) for c in caches])
    out_specs = [rows(A_Q_W), rows(B_GW)] + [cache(c) for c in caches]
    return pl.pallas_call(
        _sample_body,
        name="sample_attn",
        grid=(ns,),
        in_specs=in_specs,
        out_specs=out_specs,
        out_shape=out_shape,
        compiler_params=_cparams(("parallel",), VMEM_LIMIT),
    )(sinks, *news, *caches)


def _tail_transpose_body(x_ref, o_ref):
    o_ref[...] = x_ref[...].T


def _tail_transpose_call(name, kv, n, s, win):
    c = kv.shape[1]
    tt = min(win, 2 * LANES)
    per_seq, first = s // tt, (s - win) // tt
    return pl.pallas_call(
        _tail_transpose_body,
        name=name,
        grid=(n, win // tt),
        in_specs=[pl.BlockSpec((tt, c), lambda i, j: (i * per_seq + first + j, 0))],
        out_specs=pl.BlockSpec((None, c, tt), lambda i, j: (i, 0, j)),
        out_shape=jax.ShapeDtypeStruct((n, c, win), F32),
        compiler_params=_cparams(("parallel", "parallel")),
    )(kv)


def _merge_body(x_ref, g_ref, oa_ref, ob_ref, wga_ref, wgb_ref, wba_ref, wbb_ref, o_ref, h_scr):
    @pl.when(pl.program_id(1) == 0)
    def _():
        h_scr[...] = _rmsnorm(x_ref[...], g_ref[...]).astype(BF16)

    h = h_scr[...]
    oa = oa_ref[...].astype(BF16)
    ob = ob_ref[...].astype(BF16)
    half = o_ref.shape[1] // 2
    for c in range(2):
        cols = slice(c * half, (c + 1) * half)
        ga = _dot(h, wga_ref[:, cols])
        gb = _dot(h, wgb_ref[:, cols])
        ya = _dot(oa, wba_ref[:, cols])
        yb = _dot(ob, wbb_ref[:, cols])
        o_ref[:, cols] = (jax.nn.sigmoid(ga) * ya + jax.nn.sigmoid(gb) * yb).astype(o_ref.dtype)


def _merge_call(x2d, g, oa, ob, w_full, gate_col0, w_ba, w_bb, tm, tn):
    t = x2d.shape[0]
    ncol = D_MODEL // tn
    gate0 = gate_col0 // tn
    return pl.pallas_call(
        _merge_body,
        name="merge",
        grid=(t // tm, ncol),
        in_specs=[
            pl.BlockSpec((tm, D_MODEL), lambda i, j: (i, 0)),
            pl.BlockSpec((1, D_MODEL), lambda i, j: (0, 0)),
            pl.BlockSpec((tm, A_Q_W), lambda i, j: (i, 0)),
            pl.BlockSpec((tm, B_GW), lambda i, j: (i, 0)),
            pl.BlockSpec((D_MODEL, tn), lambda i, j: (0, gate0 + j)),
            pl.BlockSpec((D_MODEL, tn), lambda i, j: (0, gate0 + ncol + j)),
            pl.BlockSpec((A_Q_W, tn), lambda i, j: (0, j)),
            pl.BlockSpec((B_GW, tn), lambda i, j: (0, j)),
        ],
        out_specs=pl.BlockSpec((tm, tn), lambda i, j: (i, j)),
        out_shape=jax.ShapeDtypeStruct((t, D_MODEL), BF16),
        scratch_shapes=[pltpu.VMEM((tm, D_MODEL), BF16)],
        compiler_params=_cparams(("parallel", "arbitrary"), VMEM_LIMIT),
    )(x2d, g, oa, ob, w_full, w_full, w_ba, w_bb)


def _route(logits):
    tm = logits.shape[0]
    lane = lax.broadcasted_iota(jnp.int32, (tm, LANES), 1)
    neg = -jnp.inf
    big = LANES

    def first_where(cond):
        return jnp.min(jnp.where(cond, lane, big), axis=1, keepdims=True)

    gl = jnp.where(lane < N_EXPERT_GROUPS, logits, neg)
    gmax = jnp.max(gl, axis=1, keepdims=True)
    gidx = first_where(gl == gmax)
    g_w = 1.0 / jnp.sum(jnp.exp(gl - gmax), axis=1, keepdims=True)
    lo = N_EXPERT_GROUPS + gidx * EXPERTS_PER_GROUP
    in_grp = jnp.logical_and(lane >= lo, lane < lo + EXPERTS_PER_GROUP)
    el = jnp.where(in_grp, logits, neg)
    ep = jnp.exp(el - jnp.max(el, axis=1, keepdims=True))
    prob = ep / jnp.sum(ep, axis=1, keepdims=True)
    prob = jnp.where(in_grp, prob, -1.0)
    p1 = jnp.max(prob, axis=1, keepdims=True)
    i1 = first_where(prob == p1)
    prob2 = jnp.where(lane == i1, -1.0, prob)
    p2 = jnp.max(prob2, axis=1, keepdims=True)
    i2 = first_where(prob2 == p2)
    tot = p1 + p2
    c1 = g_w * (p1 / tot)
    c2 = g_w * (p2 / tot)
    e1 = (i1 - N_EXPERT_GROUPS).astype(F32)
    e2 = (i2 - N_EXPERT_GROUPS).astype(F32)
    return e1, e2, c1, c2


def _outproj_body(mp_ref, xp_ref, ms_ref, xs_ref, wo_ref, g_ref, wrh_ref, wrl_ref, br_ref, x1_ref, route_ref, *, n_prompt):
    def compute(m_ref, x_ref):
        x1 = x_ref[...] + _dot(m_ref[...], wo_ref[...])
        x1_ref[...] = x1
        hi, lo = _split2(_rmsnorm(x1, g_ref[...]))
        logits = _dot(hi, wrh_ref[...]) + _dot(hi, wrl_ref[...]) + _dot(lo, wrh_ref[...]) + br_ref[...]
        e1, e2, c1, c2 = _route(logits)
        lane = lax.broadcasted_iota(jnp.int32, logits.shape, 1)
        route_ref[...] = jnp.where(lane == 0, e1, jnp.where(lane == 1, e2, jnp.where(lane == 2, c1, jnp.where(lane == 3, c2, 0.0))))

    i = pl.program_id(0)
    pl.when(i < n_prompt)(lambda: compute(mp_ref, xp_ref))
    pl.when(i >= n_prompt)(lambda: compute(ms_ref, xs_ref))


def _outproj_call(merged_p, xp, merged_s, xs, w_o, g, wr_hi, wr_lo, br, tm):
    n_p, n_s = xp.shape[0] // tm, xs.shape[0] // tm
    t_all = xp.shape[0] + xs.shape[0]

    def p_map(i):
        return (jnp.minimum(i, n_p - 1), 0)

    def s_map(i):
        return (jnp.maximum(i - n_p, 0), 0)

    def const(shape):
        return pl.BlockSpec(shape, lambda i: (0, 0))

    return pl.pallas_call(
        functools.partial(_outproj_body, n_prompt=n_p),
        name="outproj",
        grid=(n_p + n_s,),
        in_specs=[
            pl.BlockSpec((tm, D_MODEL), p_map), pl.BlockSpec((tm, D_MODEL), p_map),
            pl.BlockSpec((tm, D_MODEL), s_map), pl.BlockSpec((tm, D_MODEL), s_map),
            const((D_MODEL, D_MODEL)), const((1, D_MODEL)), const((D_MODEL, LANES)), const((D_MODEL, LANES)),
            const((1, LANES)),
        ],
        out_specs=[pl.BlockSpec((tm, D_MODEL), lambda i: (i, 0)), pl.BlockSpec((tm, LANES), lambda i: (i, 0))],
        out_shape=[jax.ShapeDtypeStruct((t_all, D_MODEL), F32), jax.ShapeDtypeStruct((t_all, LANES), F32)],
        compiler_params=_cparams(("arbitrary",), VMEM_LIMIT),
    )(merged_p, xp, merged_s, xs, w_o, g, wr_hi, wr_lo, br)


def _pack_bf16_pairs(x):
    n = x.shape[1] // 2
    bits = pltpu.bitcast(x.astype(BF16).astype(F32), jnp.uint32)
    return (bits[:, n:] & jnp.uint32(0xFFFF0000)) | (bits[:, :n] >> 16)


def _unpack_bf16_pairs(packed):
    lo = pltpu.bitcast(packed << 16, F32)
    hi = pltpu.bitcast(packed & jnp.uint32(0xFFFF0000), F32)
    return jnp.concatenate([lo, hi], axis=1).astype(BF16)


def _row_copy(src, src_row, dst, dst_row, sem):
    return pltpu.make_async_copy(src.at[pl.ds(src_row, 1), :], dst.at[pl.ds(dst_row, 1), :], sem)


def _dispatch_body(pos_ref, prev_pos_ref, pad_tile_ref, x1_ref, g_ref, xs_hbm, h_scr, sems, *, tile):
    tm = h_scr.shape[1]
    i = pl.program_id(0)
    slot = i % 2

    @pl.when(i == 0)
    def _():
        h_scr[0] = jnp.zeros(h_scr.shape[1:], h_scr.dtype)

        def pad_copy(t, k):
            row = pl.multiple_of(t * tile + k * tm, tm)
            return pltpu.make_async_copy(h_scr.at[0], xs_hbm.at[pl.ds(row, tm), :], sems.at[0])

        n_used = pad_tile_ref[N_EXPERTS]
        for phase in ("start", "wait"):
            def expert_pad(e, carry, phase=phase):
                @pl.when(pad_tile_ref[e] >= 0)
                def _():
                    for k in range(tile // tm):
                        getattr(pad_copy(pad_tile_ref[e], k), phase)()
                return carry
            lax.fori_loop(0, N_EXPERTS, expert_pad, 0)

            def unused_tile(t, carry, phase=phase):
                @pl.when(t >= n_used)
                def _():
                    for k in range(tile // tm):
                        getattr(pad_copy(t, k), phase)()
                return carry
            lax.fori_loop(0, xs_hbm.shape[0] // tile, unused_tile, 0)

    h_scr[slot] = _pack_bf16_pairs(_rmsnorm(x1_ref[...], g_ref[...]))

    def row_copies(p_ref, s, method):
        def body(j, carry):
            for k in range(2):
                getattr(_row_copy(h_scr.at[s], j, xs_hbm, p_ref[0, 0, 2 * j + k], sems.at[s]), method)()
            return carry
        lax.fori_loop(0, tm, body, 0, unroll=8)

    row_copies(pos_ref, slot, "start")
    pl.when(i > 0)(lambda: row_copies(prev_pos_ref, 1 - slot, "wait"))
    pl.when(i == pl.num_programs(0) - 1)(lambda: row_copies(pos_ref, slot, "wait"))


def _dispatch_call(pos3, pad_tile, x1, g, rows, tm, tile):
    t = x1.shape[0]
    assert tile % tm == 0
    return pl.pallas_call(
        functools.partial(_dispatch_body, tile=tile),
        name="dispatch",
        grid=(t // tm,),
        in_specs=[
            pl.BlockSpec((1, 1, 2 * tm), lambda i: (i, 0, 0), memory_space=pltpu.SMEM),
            pl.BlockSpec((1, 1, 2 * tm), lambda i: (jnp.maximum(i - 1, 0), 0, 0), memory_space=pltpu.SMEM),
            pl.BlockSpec(memory_space=pltpu.SMEM),
            pl.BlockSpec((tm, D_MODEL), lambda i: (i, 0)),
            pl.BlockSpec((1, D_MODEL), lambda i: (0, 0)),
        ],
        out_specs=pl.BlockSpec(memory_space=pl.ANY),
        out_shape=jax.ShapeDtypeStruct((rows, D_MODEL // 2), jnp.uint32),
        scratch_shapes=[pltpu.VMEM((2, tm, D_MODEL // 2), jnp.uint32), pltpu.SemaphoreType.DMA((2,))],
        compiler_params=_cparams(("arbitrary",), VMEM_LIMIT),
    )(pos3, pos3, pad_tile, x1, g)


def _moe_body(te_ref, tv_ref, nu_ref, xs_ref, wg_ref, wu_ref, wd_ref, ys_ref, wg_scr, wu_scr, wd_scr):
    i = pl.program_id(0)

    @pl.when(jnp.logical_and(i < nu_ref[0], jnp.logical_or(i == 0, te_ref[i] != te_ref[jnp.maximum(i - 1, 0)])))
    def _():
        wg_scr[...] = wg_ref[0].astype(BF16)
        wu_scr[...] = wu_ref[0].astype(BF16)
        wd_scr[...] = wd_ref[0].astype(BF16)

    @pl.when(i < nu_ref[0])
    def _():
        row = lax.broadcasted_iota(jnp.int32, xs_ref.shape, 0)
        x = _unpack_bf16_pairs(jnp.where(row < tv_ref[i], xs_ref[...], jnp.uint32(0)))
        a = jax.nn.silu(_dot(x, wg_scr[...])) * _dot(x, wu_scr[...])
        ys_ref[...] = _dot(a.astype(BF16), wd_scr[...])

    @pl.when(i >= nu_ref[0])
    def _():
        ys_ref[...] = jnp.zeros_like(ys_ref)


def _moe_call(tile_expert, tile_valid, n_used, xs, w_gate, w_up, w_down, tm):
    rows = xs.shape[0]

    def row_map(i, te, tv, nu):
        return (jnp.minimum(i, nu[0] - 1), 0)

    def out_map(i, te, tv, nu):
        return (i, 0)

    def w_map(i, te, tv, nu):
        return (te[i], 0, 0)

    return pl.pallas_call(
        _moe_body,
        name="moe",
        grid_spec=pltpu.PrefetchScalarGridSpec(
            num_scalar_prefetch=3,
            grid=(rows // tm,),
            in_specs=[
                pl.BlockSpec((tm, D_MODEL // 2), row_map),
                pl.BlockSpec((1, D_MODEL, D_EXPERT), w_map),
                pl.BlockSpec((1, D_MODEL, D_EXPERT), w_map),
                pl.BlockSpec((1, D_EXPERT, D_MODEL), w_map),
            ],
            out_specs=pl.BlockSpec((tm, D_MODEL), out_map),
            scratch_shapes=[pltpu.VMEM((D_MODEL, D_EXPERT), BF16), pltpu.VMEM((D_MODEL, D_EXPERT), BF16),
                            pltpu.VMEM((D_EXPERT, D_MODEL), BF16)],
        ),
        out_shape=jax.ShapeDtypeStruct((rows, D_MODEL), F32),
        compiler_params=_cparams(("arbitrary",), VMEM_LIMIT),
    )(tile_expert, tile_valid, n_used, xs, w_gate, w_up, w_down)


def _combine_body(pos_ref, next_pos_ref, x1_ref, route_ref, ys_hbm, y_ref, buf, sems):
    tm = buf.shape[2]
    i = pl.program_id(0)
    slot = i % 2

    def row_copies(p_ref, s, method):
        def body(j, carry):
            for k in range(2):
                getattr(_row_copy(ys_hbm, p_ref[0, 0, 2 * j + k], buf.at[s, k], j, sems.at[s]), method)()
            return carry
        lax.fori_loop(0, tm, body, 0, unroll=8)

    pl.when(i == 0)(lambda: row_copies(pos_ref, slot, "start"))
    pl.when(i + 1 < pl.num_programs(0))(lambda: row_copies(next_pos_ref, 1 - slot, "start"))
    row_copies(pos_ref, slot, "wait")
    route = route_ref[...]
    y_ref[...] = x1_ref[...] + (route[:, 2:3] * buf[slot, 0] + route[:, 3:4] * buf[slot, 1])


def _combine_call(name, pos3, x1_all, route, ys, t, row0, tm):
    off = row0 // tm
    last = off + t // tm - 1
    return pl.pallas_call(
        _combine_body,
        name=name,
        grid=(t // tm,),
        in_specs=[
            pl.BlockSpec((1, 1, 2 * tm), lambda i: (i + off, 0, 0), memory_space=pltpu.SMEM),
            pl.BlockSpec((1, 1, 2 * tm), lambda i: (jnp.minimum(i + 1 + off, last), 0, 0), memory_space=pltpu.SMEM),
            pl.BlockSpec((tm, D_MODEL), lambda i: (i + off, 0)),
            pl.BlockSpec((tm, LANES), lambda i: (i + off, 0)),
            pl.BlockSpec(memory_space=pl.ANY),
        ],
        out_specs=pl.BlockSpec((tm, D_MODEL), lambda i: (i, 0)),
        out_shape=jax.ShapeDtypeStruct((t, D_MODEL), F32),
        scratch_shapes=[pltpu.VMEM((2, 2, tm, D_MODEL), F32), pltpu.SemaphoreType.DMA((2,))],
        compiler_params=_cparams(("arbitrary",), VMEM_LIMIT),
    )(pos3, pos3, x1_all, route, ys)


def _routing_offsets(route, tm, n_tiles):
    experts = jnp.arange(N_EXPERTS, dtype=jnp.int32)[None, :]
    e = route[:, :2].astype(jnp.int32)
    hot = [e[:, k:k + 1] == experts for k in range(2)]
    onehot = hot[0].astype(jnp.int32) + hot[1].astype(jnp.int32)
    csum = jnp.cumsum(onehot, axis=0)
    counts = csum[-1]
    tiles_e = (counts + tm - 1) // tm
    tile_end = jnp.cumsum(tiles_e)
    tile_start = tile_end - tiles_e
    row0 = (csum - onehot) + (tile_start * tm)[None, :]
    pos = jnp.stack([jnp.sum(jnp.where(h, row0, 0), axis=1) for h in hot], axis=1)
    n_used = tile_end[-1]
    tile_id = jnp.minimum(jnp.arange(n_tiles, dtype=jnp.int32), n_used - 1)
    tile_expert = jnp.sum((tile_id[:, None] >= tile_end[None, :]).astype(jnp.int32), axis=1)
    tile_hot = tile_expert[:, None] == experts
    seg_end = jnp.sum(jnp.where(tile_hot, (tile_start * tm + counts)[None, :], 0), axis=1)
    tile_valid = jnp.clip(seg_end - tile_id * tm, 0, tm)
    pad_tile = jnp.concatenate([jnp.where(tiles_e > 0, tile_end - 1, -1), n_used.reshape(1)])
    i32 = jnp.int32
    return (pos.astype(i32), tile_expert.astype(i32), tile_valid.astype(i32), n_used.reshape(1).astype(i32),
            pad_tile.astype(i32))


def _pick_tile(t, pref):
    tm = min(pref, t)
    assert t % tm == 0, (t, tm)
    return tm


def kernel(x_prompt, x_sample, cache_a_kv, cache_b1_kv, cache_b2_kv, cache_b3_kv, g_attn_norm, w_in, q_norm_a, k_norm_a, q_norm_b, k_norm_b, sinks_a, w_branch_a, w_branch_b, w_out, g_ffn_norm, w_group_router, b_group_router, w_expert_router, b_expert_router, w_expert_gate, w_expert_up, w_expert_down):
    n, s, _ = x_prompt.shape
    ns, ts, _ = x_sample.shape
    assert ts == SAMPLE_T and s == B_PATTERNS[-1][0] and x_prompt.shape[2] == D_MODEL
    assert g_attn_norm.shape[0] == 1, "single layer"
    tp, tsmp = n * s, ns * ts
    t_all = tp + tsmp
    xp = x_prompt.reshape(tp, D_MODEL)
    xs_in = x_sample.reshape(tsmp, D_MODEL)

    w = w_in[0].astype(BF16)
    blk = A_Q_W + 2 * A_KV_W
    assert blk == B_W and blk % PROJ_SEG == 0 and B_GW == PROJ_SEG
    rep = LANES // HEAD_DIM
    gains = jnp.stack([jnp.tile(v[0], rep) for v in (q_norm_a, k_norm_a, q_norm_b, k_norm_b)]).astype(F32)
    g_attn = g_attn_norm.astype(F32)
    g_ffn = g_ffn_norm.astype(F32)
    plan_a = ([(0, c, PROJ_SEG, 0) for c in range(0, A_Q_W, PROJ_SEG)],
              [(0, A_Q_W, A_KV_W, 1), (0, A_Q_W + A_KV_W, A_KV_W, -1)])
    plan_q = ([(0, c, PROJ_SEG, 2) for c in range(0, B_W, PROJ_SEG)],)
    plan_kv = tuple([(0, g * B_GW, B_GW, 3), (1, g * B_GW, B_GW, -1)] for g in range(B_GROUPS))

    def shift(plan, by):
        return tuple([(widx + by, col0, width, kind) for widx, col0, width, kind in segs] for segs in plan)

    plan_all = plan_a + shift(plan_q, 1) + shift(plan_kv, 2)

    def project(x2d, pos, tm, transposed_seq=None):
        outs = _proj_call("proj", x2d, g_attn, w, [(blk, j) for j in range(4)], gains, _rope_tables(pos), plan_all, tm,
                          transposed_seq)
        return outs[0], outs[1], outs[2], outs[3:]

    tm_p = _pick_tile(s, ROW_TILE)
    assert B_PATTERNS[-1][0] >= s, "the last B group's new buffer is the whole sequence, written by proj_kvb"
    qa, kva, qb, kvb = project(xp, jnp.arange(s, dtype=jnp.int32), tm_p, s)
    kvb, kvb_last_t = kvb[:B_GROUPS], kvb[B_GROUPS]
    sinks = sinks_a[0].astype(F32)
    oa_p = _attn_a_call(qa, kva, sinks, n, s)
    ob_p = _attn_b_call(qb, kvb, n, s)

    def to_feature_major(c, heads):
        return jnp.transpose(c, (0, 2, 3, 4, 1)).reshape(c.shape[0], 2 * heads * HEAD_DIM, c.shape[1])

    def from_feature_major(c, heads):
        return jnp.transpose(c.reshape(c.shape[0], 2, heads, HEAD_DIM, c.shape[2]), (0, 4, 1, 2, 3))[None]

    new_a_p = from_feature_major(_tail_transpose_call("tail_a", kva, n, s, min(A_WINDOW, s)), A_KV_HEADS)
    new_b_p = [from_feature_major(_tail_transpose_call("tail_b%d" % g, kvb[g], n, s, min(win, s)), B_HPG)
               for g, (win, _) in enumerate(B_PATTERNS[:-1])]
    new_b_p.append(from_feature_major(kvb_last_t, B_HPG))

    tm_s = _pick_tile(tsmp, ROW_TILE)
    pos_s = PAST_LEN + (jnp.arange(tm_s, dtype=jnp.int32) % ts)
    qa_s, kva_s, qb_s, kvb_s = project(xs_in, pos_s, tm_s)
    caches = [to_feature_major(cache_a_kv[0], A_KV_HEADS)]
    caches += [to_feature_major(c[0], B_HPG) for c in (cache_b1_kv, cache_b2_kv, cache_b3_kv)]
    oa_s, ob_s, na, nb1, nb2, nb3 = _sample_call(sinks, qa_s, kva_s, qb_s, kvb_s, caches, ns)
    new_a_s = from_feature_major(na, A_KV_HEADS)
    new_b_s = [from_feature_major(c, B_HPG) for c in (nb1, nb2, nb3)]

    w_ba = w_branch_a[0].astype(BF16)
    w_bb = w_branch_b[0].astype(BF16)
    w_o = w_out[0].astype(BF16)
    wr = jnp.zeros((D_MODEL, LANES), F32)
    wr = wr.at[:, :N_EXPERT_GROUPS].set(w_group_router[0]).at[:, N_EXPERT_GROUPS:N_EXPERT_GROUPS + N_EXPERTS].set(w_expert_router[0])
    wr_hi = wr.astype(BF16)
    wr_lo = (wr - wr_hi.astype(F32)).astype(BF16)
    br = jnp.zeros((1, LANES), F32)
    br = br.at[0, :N_EXPERT_GROUPS].set(b_group_router[0]).at[0, N_EXPERT_GROUPS:N_EXPERT_GROUPS + N_EXPERTS].set(b_expert_router[0])

    tn = 512
    tm_mp, tm_ms = _pick_tile(tp, 1024), _pick_tile(tsmp, 1024)
    merged_p = _merge_call(xp, g_attn, oa_p, ob_p, w, 4 * blk, w_ba, w_bb, tm_mp, tn)
    merged_s = _merge_call(xs_in, g_attn, oa_s, ob_s, w, 4 * blk, w_ba, w_bb, tm_ms, tn)
    tm_o = _pick_tile(tsmp, ROW_TILE)
    assert tp % tm_o == 0
    x1_all, route = _outproj_call(merged_p, xp, merged_s, xs_in, w_o, g_ffn, wr_hi, wr_lo, br, tm_o)

    tm_e = MOE_TILE
    n_tiles = (2 * t_all + N_EXPERTS * (tm_e - 1)) // tm_e + 1
    pos, tile_expert, tile_valid, n_used, pad_tile = _routing_offsets(route, tm_e, n_tiles)
    tm_d = _pick_tile(tsmp, 2 * ROW_TILE)
    pos3 = pos.reshape(t_all // tm_d, 1, 2 * tm_d)
    xs_sorted = _dispatch_call(pos3, pad_tile, x1_all, g_ffn, n_tiles * tm_e, tm_d, tm_e)
    ys = _moe_call(tile_expert, tile_valid, n_used, xs_sorted, w_expert_gate[0], w_expert_up[0], w_expert_down[0], tm_e)
    y_p = _combine_call("combine_p", pos3, x1_all, route, ys, tp, 0, tm_d)
    y_s = _combine_call("combine_s", pos3, x1_all, route, ys, tsmp, tp, tm_d)

    return (y_p.reshape(n, s, D_MODEL), y_s.reshape(ns, ts, D_MODEL),
            new_a_p, new_a_s, new_b_p[0], new_b_s[0], new_b_p[1], new_b_s[1], new_b_p[2], new_b_s[2])
```
